```python
import jax, jax.numpy as jnp
from jax import lax
import numpy as np

D_MODEL = 2048
BATCH = 4
SEQ = 2048
DEPTH = 1

GDN_HEADS = 16
GDN_HEAD_K = 128
GDN_HEAD_V = 128
GDN_CONV = 4
GDN_CHUNK = 64
QK_W = GDN_HEADS * GDN_HEAD_K
V_W = GDN_HEADS * GDN_HEAD_V
SC_WIDTH = 2048
SC_CONV = 3
D_FF = -(-8 * D_MODEL // (3 * 256)) * 256
EPS = 1e-6

IN_SIZES = (QK_W, QK_W, V_W, V_W, GDN_HEADS, GDN_HEADS, SC_WIDTH, SC_WIDTH, SC_WIDTH, D_MODEL, D_MODEL)
IN_WIDTH = sum(IN_SIZES)

kernel_name = "hybrid_gdn_shortconv_gated_merge"


def rms_norm(x, g):
    xf = x.astype(jnp.float32)
    y = xf * lax.rsqrt(jnp.mean(xf * xf, axis=-1, keepdims=True) + EPS)
    return (y * g.astype(jnp.float32)).astype(x.dtype)


def l2_norm(x):
    return x * lax.rsqrt(jnp.sum(x * x, axis=-1, keepdims=True) + EPS)


def partition(t, sizes):
    out, start = [], 0
    for s in sizes:
        out.append(t[..., start:start + s])
        start += s
    return out


def causal_dwconv(x, w):
    K = w.shape[0]
    T = x.shape[1]
    xp = jnp.pad(x, ((0, 0), (K - 1, 0), (0, 0)))
    y = xp[:, 0:T] * w[0]
    for i in range(1, K):
        y = y + xp[:, i:i + T] * w[i]
    return y


def gated_delta_chunked(q, k, v, g, beta):
    Bn, T, H, Dk = q.shape
    Dv = v.shape[-1]
    C = GDN_CHUNK
    N = T // C

    def to_chunks(t):
        t = jnp.moveaxis(t, 2, 1)
        return t.reshape(t.shape[:2] + (N, C) + t.shape[3:])

    q, k, v, g, beta = map(to_chunks, (q * (Dk ** -0.5), k, v, g, beta))
    g = jnp.cumsum(g, axis=-1)
    causal = jnp.tril(jnp.ones((C, C), dtype=bool))
    strict = jnp.tril(jnp.ones((C, C), dtype=bool), -1)
    decay = jnp.exp(jnp.where(causal, g[..., :, None] - g[..., None, :], -jnp.inf))
    k_beta = k * beta[..., None]
    a_low = jnp.where(strict, jnp.einsum('bhnid,bhnjd->bhnij', k_beta, k) * decay, 0.0)
    t_mat = a_low + jnp.eye(C, dtype=a_low.dtype)
    rhs = jnp.concatenate([v * beta[..., None], k_beta * jnp.exp(g)[..., None]], axis=-1)
    sol = lax.linalg.triangular_solve(t_mat, rhs, left_side=True, lower=True, unit_diagonal=True)
    u, w = sol[..., :Dv], sol[..., Dv:]
    attn_intra = jnp.einsum('bhnid,bhnjd->bhnij', q, k) * decay
    q_dec = q * jnp.exp(g)[..., None]
    g_last = g[..., -1]
    k_dec = k * jnp.exp(g_last[..., None] - g)[..., None]

    def step(S, xs):
        q_c, k_c, u_c, w_c, attn_c, gl_c = xs
        v_new = u_c - jnp.einsum('bhcd,bhdv->bhcv', w_c, S)
        o = jnp.einsum('bhcd,bhdv->bhcv', q_c, S) + jnp.einsum('bhij,bhjv->bhiv', attn_c, v_new)
        S = S * jnp.exp(gl_c)[..., None, None] + jnp.einsum('bhcd,bhcv->bhdv', k_c, v_new)
        return S, o

    xs = tuple(jnp.moveaxis(t, 2, 0) for t in (q_dec, k_dec, u, w, attn_intra, g_last))
    S0 = jnp.zeros((Bn, H, Dk, Dv), dtype=jnp.float32)
    _, o = lax.scan(step, S0, xs)
    o = jnp.moveaxis(o, 0, 2).reshape(Bn, H, T, Dv)
    return jnp.moveaxis(o, 1, 2)


def gdn_branch(q, k, v, z, b, a, conv_w, A_log, dt_bias, norm_g):
    Bn, T, _ = q.shape
    dtype = q.dtype
    qkv = jax.nn.silu(causal_dwconv(jnp.concatenate([q, k, v], axis=-1), conv_w))
    q, k, v = partition(qkv.astype(jnp.float32), (QK_W, QK_W, V_W))
    q = l2_norm(q.reshape(Bn, T, GDN_HEADS, GDN_HEAD_K))
    k = l2_norm(k.reshape(Bn, T, GDN_HEADS, GDN_HEAD_K))
    v = v.reshape(Bn, T, GDN_HEADS, GDN_HEAD_V)
    beta = jax.nn.sigmoid(b.astype(jnp.float32))
    g = -jnp.exp(A_log.astype(jnp.float32)) * jax.nn.softplus(a.astype(jnp.float32) + dt_bias.astype(jnp.float32))
    o = gated_delta_chunked(q, k, v, g, beta)
    o = rms_norm(o, norm_g) * jax.nn.silu(z.reshape(Bn, T, GDN_HEADS, GDN_HEAD_V).astype(jnp.float32))
    return o.reshape(Bn, T, V_W).astype(dtype)


def setup_inputs(seed: int = 0) -> dict:
    key = jax.random.key(seed)
    ks = jax.random.split(key, 17)
    L = DEPTH
    nrm = lambda k, shape, fan_in: jax.random.normal(k, shape, jnp.float32) * (fan_in ** -0.5)
    gain = lambda k, shape: 1.0 + 0.02 * jax.random.normal(k, shape, jnp.float32)
    return {
        "x": jax.random.normal(ks[0], (BATCH, SEQ, D_MODEL), jnp.float32),
        "ln_mix_g": gain(ks[1], (L, D_MODEL)),
        "w_in": nrm(ks[2], (L, D_MODEL, IN_WIDTH), D_MODEL),
        "conv_qkv_w": nrm(ks[3], (L, GDN_CONV, 2 * QK_W + V_W), GDN_CONV),
        "A_log": jnp.log(jax.random.uniform(ks[4], (L, GDN_HEADS), jnp.float32, 1.0, 16.0)),
        "dt_bias": 0.1 * jax.random.normal(ks[5], (L, GDN_HEADS), jnp.float32),
        "gdn_norm_g": gain(ks[6], (L, GDN_HEAD_V)),
        "w_proj_a": nrm(ks[7], (L, V_W, D_MODEL), V_W),
        "conv_sc_w": nrm(ks[8], (L, SC_CONV, SC_WIDTH), SC_CONV),
        "w_proj_b": nrm(ks[9], (L, SC_WIDTH, D_MODEL), SC_WIDTH),
        "w_out": nrm(ks[10], (L, D_MODEL, D_MODEL), D_MODEL),
        "ln_ffn_g": gain(ks[11], (L, D_MODEL)),
        "w_gate": nrm(ks[12], (L, D_MODEL, D_FF), D_MODEL),
        "w_up": nrm(ks[13], (L, D_MODEL, D_FF), D_MODEL),
        "w_down": nrm(ks[14], (L, D_FF, D_MODEL), D_FF),
        "ln_final_g": gain(ks[15], (D_MODEL,)),
    }


def reference(x, ln_mix_g, w_in, conv_qkv_w, A_log, dt_bias, gdn_norm_g, w_proj_a, conv_sc_w,
              w_proj_b, w_out, ln_ffn_g, w_gate, w_up, w_down, ln_final_g):
    h = x
    for l in range(DEPTH):
        xn = rms_norm(h, ln_mix_g[l])
        proj = jnp.einsum('btd,de->bte', xn, w_in[l])
        q, k, v, z, b, a, sc_b, sc_c, sc_h, gate_a, gate_b = partition(proj, IN_SIZES)
        o_a = gdn_branch(q, k, v, z, b, a, conv_qkv_w[l], A_log[l], dt_bias[l], gdn_norm_g[l])
        o_b = sc_b * causal_dwconv(sc_c * sc_h, conv_sc_w[l])
        merged = (jax.nn.sigmoid(gate_a) * jnp.einsum('btv,vd->btd', o_a, w_proj_a[l])
                  + jax.nn.sigmoid(gate_b) * jnp.einsum('btc,cd->btd', o_b, w_proj_b[l]))
        h = h + jnp.einsum('btd,de->bte', merged, w_out[l])
        hn = rms_norm(h, ln_ffn_g[l])
        ff = jax.nn.silu(jnp.einsum('btd,df->btf', hn, w_gate[l])) * jnp.einsum('btd,df->btf', hn, w_up[l])
        h = h + jnp.einsum('btf,fd->btd', ff, w_down[l])
    return rms_norm(h, ln_final_g)
```

```python
import functools

import jax
import jax.numpy as jnp
from jax import lax
from jax.experimental import pallas as pl
from jax.experimental.pallas import tpu as pltpu

EPS = 1e-6
F32 = jnp.float32
BF16 = jnp.bfloat16

LANES = 128
V7X_VMEM_CAP_BYTES = 56 * 1024 * 1024
GDN_CHUNK = 128
GDN_HEADS_PER_STEP = 4
GATE_LANES = LANES


def _vmem_limit(block_bytes, scratch_bytes=0, temp_bytes=0):
    need = 2 * block_bytes + scratch_bytes + temp_bytes + (4 << 20)
    return int(min(V7X_VMEM_CAP_BYTES, max(need, 16 << 20)))


def _pick_tile(n, target, align):
    t = min(n, target)
    t -= t % align
    while t >= align:
        if n % t == 0:
            return t
        t -= align
    return n


def _sigmoid(x):
    return 1.0 / (1.0 + jnp.exp(-x))


def _silu(x):
    return x * _sigmoid(x)


def _dot(a, b):
    return jnp.dot(a, b, preferred_element_type=F32)


def _dot_nt(a, b):
    return lax.dot_general(a, b, (((1,), (1,)), ((), ())), preferred_element_type=F32)


def _dot_tn(a, b):
    return lax.dot_general(a, b, (((0,), (0,)), ((), ())), preferred_element_type=F32)


def _rms(x, g):
    return x * lax.rsqrt(jnp.mean(x * x, axis=-1, keepdims=True) + EPS) * g


def _inproj_kernel(x_ref, g_ref, w_ref, wg_hi_ref, wg_lo_ref, out_ref, gates_ref, xn_ref):
    @pl.when(pl.program_id(1) == 0)
    def _():
        y = _rms(x_ref[...], g_ref[...])
        hi = y.astype(BF16)
        xn_ref[...] = hi
        lo = (y - hi.astype(F32)).astype(BF16)
        gates_ref[...] = (_dot(hi, wg_hi_ref[...]) + _dot(hi, wg_lo_ref[...])
                          + _dot(lo, wg_hi_ref[...]))

    out_ref[...] = _dot(xn_ref[...], w_ref[...])


def _inproj(x2, g, w_main, wg_hi, wg_lo):
    m, d = x2.shape
    n = w_main.shape[1]
    tm = _pick_tile(m, 1024, 8)
    tn = _pick_tile(n, 512, LANES)
    blocks = tm * d * 4 + d * tn * 2 + tm * tn * 4 + tm * GATE_LANES * 4 + 2 * d * GATE_LANES * 2
    return pl.pallas_call(
        _inproj_kernel,
        grid=(m // tm, n // tn),
        in_specs=[
            pl.BlockSpec((tm, d), lambda i, j: (i, 0)),
            pl.BlockSpec((1, d), lambda i, j: (0, 0)),
            pl.BlockSpec((d, tn), lambda i, j: (0, j)),
            pl.BlockSpec((d, GATE_LANES), lambda i, j: (0, 0)),
            pl.BlockSpec((d, GATE_LANES), lambda i, j: (0, 0)),
        ],
        out_specs=[
            pl.BlockSpec((tm, tn), lambda i, j: (i, j)),
            pl.BlockSpec((tm, GATE_LANES), lambda i, j: (i, 0)),
        ],
        out_shape=[
            jax.ShapeDtypeStruct((m, n), F32),
            jax.ShapeDtypeStruct((m, GATE_LANES), F32),
        ],
        scratch_shapes=[pltpu.VMEM((tm, d), BF16)],
        compiler_params=pltpu.CompilerParams(
            dimension_semantics=("parallel", "arbitrary"),
            vmem_limit_bytes=_vmem_limit(blocks, tm * d * 2, tm * d * 8)),
        name="inproj",
    )(x2, g, w_main, wg_hi, wg_lo)


def _inv_unit_lower_minus_eye(a, rowi, coli):
    a0 = jnp.where((rowi >> 4) == (coli >> 4), a, 0.0)
    x = a0
    e = -a0
    for _ in range(3):
        xb = x.astype(BF16)
        x = _dot(xb, xb)
        e = e + x + _dot(e.astype(BF16), x.astype(BF16))
    level = 5
    while (1 << (level - 1)) < GDN_CHUNK:
        same_hi = (rowi >> level) == (coli >> level)
        same_lo = (rowi >> (level - 1)) == (coli >> (level - 1))
        y = jnp.where(jnp.logical_and(same_hi, jnp.logical_not(same_lo)), a, 0.0)
        z = y + _dot(y.astype(BF16), e.astype(BF16))
        e = e - z - _dot(e.astype(BF16), z.astype(BF16))
        level += 1
    return e


def _gdn_kernel(q_ref, k_ref, v_ref, z_ref, gates_ref, cwq_ref, cwk_ref, cwv_ref, gp_ref, ng_ref,
                o_ref, s_ref, halo_ref, gct_ref, *, n_heads, head_dim):
    c_len = GDN_CHUNK
    hb = GDN_HEADS_PER_STEP
    tb = q_ref.shape[1]
    n_chunks = tb // c_len
    hg = pl.program_id(1)

    @pl.when(pl.program_id(2) == 0)
    def _():
        s_ref[...] = jnp.zeros_like(s_ref)
        halo_ref[...] = jnp.zeros_like(halo_ref)

    gates = gates_ref[0]
    gp = gp_ref[...]
    beta_all = _sigmoid(gates)
    xa = gates + gp[1:2, :]
    softplus = jnp.maximum(xa, 0.0) + jnp.log(1.0 + jnp.exp(-jnp.abs(xa)))
    gcum = -jnp.exp(gp[0:1, :]) * softplus
    row_in_chunk = lax.broadcasted_iota(jnp.int32, (tb, GATE_LANES), 0) & (c_len - 1)
    shift = 1
    while shift < c_len:
        gcum = gcum + jnp.where(row_in_chunk >= shift, pltpu.roll(gcum, shift, 0), 0.0)
        shift *= 2
    for c in range(n_chunks):
        gct_ref[c] = gcum[c * c_len:(c + 1) * c_len, :].T
    gate_lane = lax.broadcasted_iota(jnp.int32, (tb, GATE_LANES), 1)

    def conv_silu(x_ref, cw_ref, slot):
        x = x_ref[0]
        xc = jnp.concatenate([halo_ref[slot], x], axis=0)
        cw = cw_ref[...]
        taps = cw.shape[0]
        y = xc * cw[taps - 1:taps, :]
        for s in range(1, taps):
            y = y + pltpu.roll(xc, s, 0) * cw[taps - 1 - s:taps - s, :]
        halo_ref[slot] = x[tb - 8:tb, :]
        return _silu(y[8:, :])

    yq = conv_silu(q_ref, cwq_ref, 0)
    yk = conv_silu(k_ref, cwk_ref, 1)
    yv = conv_silu(v_ref, cwv_ref, 2)
    z = z_ref[0]

    rowi = lax.broadcasted_iota(jnp.int32, (c_len, c_len), 0)
    coli = lax.broadcasted_iota(jnp.int32, (c_len, c_len), 1)
    incl = rowi >= coli
    strict = rowi > coli
    ng = ng_ref[...]

    for j in range(hb):
        lanes = slice(j * head_dim, (j + 1) * head_dim)
        head = hg * hb + j
        qh = yq[:, lanes]
        kh = yk[:, lanes]
        vh = yv[:, lanes]
        qh = qh * (lax.rsqrt(jnp.sum(qh * qh, axis=1, keepdims=True) + EPS) * head_dim ** -0.5)
        kh = kh * lax.rsqrt(jnp.sum(kh * kh, axis=1, keepdims=True) + EPS)
        beta = jnp.sum(jnp.where(gate_lane == head, beta_all, 0.0), axis=1, keepdims=True)
        gcol = jnp.sum(jnp.where(gate_lane == head + n_heads, gcum, 0.0), axis=1, keepdims=True)

        pre = []
        for c in range(n_chunks):
            rows = slice(c * c_len, (c + 1) * c_len)
            g_c = gcol[rows]
            g_r = gct_ref[c, pl.ds(head + n_heads, 1), :]
            g_last = g_c[c_len - 1:c_len, :]
            diff = g_c - g_r
            decay = jnp.where(incl, jnp.exp(jnp.where(incl, diff, 0.0)), 0.0)
            e_g = jnp.exp(g_c)
            kc = kh[rows]
            qc = qh[rows]
            bb = beta[rows]
            kb = kc * bb
            kbf = kc.astype(BF16)
            kk = _dot_nt(kb.astype(BF16), kbf)
            qk = _dot_nt(qc.astype(BF16), kbf)
            a_low = jnp.where(strict, kk * decay, 0.0)
            attn = qk * decay
            e_inv = _inv_unit_lower_minus_eye(a_low, rowi, coli)
            rhs = jnp.concatenate([vh[rows] * bb, kb * e_g], axis=1)
            sol = rhs + _dot(e_inv.astype(BF16), rhs.astype(BF16))
            u = sol[:, :head_dim]
            w = sol[:, head_dim:]
            wq = jnp.concatenate([w, qc * e_g], axis=0).astype(BF16)
            k_dec = (kc * jnp.exp(g_last - g_c)).astype(BF16)
            pre.append((u, wq, attn.astype(BF16), k_dec, jnp.exp(g_last)))

        s = s_ref[j]
        for c in range(n_chunks):
            rows = slice(c * c_len, (c + 1) * c_len)
            u, wq, attn, k_dec, e_last = pre[c]
            ws = _dot(wq, s.astype(BF16))
            v_new = u - ws[:c_len]
            v_bf = v_new.astype(BF16)
            o = ws[c_len:] + _dot(attn, v_bf)
            s = s * e_last + _dot_tn(k_dec, v_bf)
            zc = z[rows, lanes]
            o_ref[0, rows, lanes] = (_rms(o, ng) * _silu(zc)).astype(o_ref.dtype)
        s_ref[j] = s


def _gdn(proj3, gates3, conv_w, gate_params, norm_g, *, n_heads, head_dim, col_q, col_k, col_v,
         col_z, v_width):
    b, t, _ = proj3.shape
    hb = GDN_HEADS_PER_STEP
    gw = hb * head_dim
    tb = _pick_tile(t, 2 * GDN_CHUNK, GDN_CHUNK)
    taps = conv_w.shape[0]
    qb, kb_, vb, zb = col_q // gw, col_k // gw, col_v // gw, col_z // gw
    blocks = 4 * tb * gw * 4 + tb * GATE_LANES * 4 + 3 * 8 * gw * 4 + tb * gw * 2
    scratch = hb * head_dim * head_dim * 4 + 3 * 8 * gw * 4 + (tb // GDN_CHUNK) * GDN_CHUNK * GATE_LANES * 4
    kern = functools.partial(_gdn_kernel, n_heads=n_heads, head_dim=head_dim)
    return pl.pallas_call(
        kern,
        grid=(b, n_heads // hb, t // tb),
        in_specs=[
            pl.BlockSpec((1, tb, gw), lambda i, h, s: (i, s, qb + h)),
            pl.BlockSpec((1, tb, gw), lambda i, h, s: (i, s, kb_ + h)),
            pl.BlockSpec((1, tb, gw), lambda i, h, s: (i, s, vb + h)),
            pl.BlockSpec((1, tb, gw), lambda i, h, s: (i, s, zb + h)),
            pl.BlockSpec((1, tb, GATE_LANES), lambda i, h, s: (i, s, 0)),
            pl.BlockSpec((taps, gw), lambda i, h, s: (0, qb + h)),
            pl.BlockSpec((taps, gw), lambda i, h, s: (0, kb_ + h)),
            pl.BlockSpec((taps, gw), lambda i, h, s: (0, vb + h)),
            pl.BlockSpec((8, GATE_LANES), lambda i, h, s: (0, 0)),
            pl.BlockSpec((1, head_dim), lambda i, h, s: (0, 0)),
        ],
        out_specs=pl.BlockSpec((1, tb, gw), lambda i, h, s: (i, s, h)),
        out_shape=jax.ShapeDtypeStruct((b, t, v_width), BF16),
        scratch_shapes=[
            pltpu.VMEM((hb, head_dim, head_dim), F32),
            pltpu.VMEM((3, 8, gw), F32),
            pltpu.VMEM((tb // GDN_CHUNK, GATE_LANES, GDN_CHUNK), F32),
        ],
        compiler_params=pltpu.CompilerParams(
            dimension_semantics=("parallel", "parallel", "arbitrary"),
            vmem_limit_bytes=_vmem_limit(blocks, scratch, 16 << 20)),
        name="gdn",
    )(proj3, proj3, proj3, proj3, gates3, conv_w, conv_w, conv_w, gate_params, norm_g)


def _shortconv_kernel(b_ref, c_ref, h_ref, cw_ref, o_ref):
    ch = c_ref[0] * h_ref[0]
    cw = cw_ref[...]
    taps = cw.shape[0]
    row = lax.broadcasted_iota(jnp.int32, ch.shape, 0)
    y = ch * cw[taps - 1:taps, :]
    for s in range(1, taps):
        y = y + jnp.where(row >= s, pltpu.roll(ch, s, 0), 0.0) * cw[taps - 1 - s:taps - s, :]
    o_ref[0] = (b_ref[0] * y).astype(o_ref.dtype)


def _shortconv(proj3, conv_w, *, col_b, col_c, col_h, width):
    b, t, _ = proj3.shape
    wb = _pick_tile(width, 2 * LANES, LANES)
    taps = conv_w.shape[0]
    bb, cb, hb = col_b // wb, col_c // wb, col_h // wb
    blocks = 3 * t * wb * 4 + t * wb * 2 + 8 * wb * 4
    return pl.pallas_call(
        _shortconv_kernel,
        grid=(b, width // wb),
        in_specs=[
            pl.BlockSpec((1, t, wb), lambda i, j: (i, 0, bb + j)),
            pl.BlockSpec((1, t, wb), lambda i, j: (i, 0, cb + j)),
            pl.BlockSpec((1, t, wb), lambda i, j: (i, 0, hb + j)),
            pl.BlockSpec((taps, wb), lambda i, j: (0, j)),
        ],
        out_specs=pl.BlockSpec((1, t, wb), lambda i, j: (i, 0, j)),
        out_shape=jax.ShapeDtypeStruct((b, t, width), BF16),
        compiler_params=pltpu.CompilerParams(
            dimension_semantics=("parallel", "parallel"),
            vmem_limit_bytes=_vmem_limit(blocks, 0, 4 * t * wb * 4)),
        name="shortconv",
    )(proj3, proj3, proj3, conv_w)


def _merge_kernel(a_ref, b_ref, wa_ref, wb_ref, ga_ref, gb_ref, o_ref):
    pa = _dot(a_ref[...], wa_ref[...])
    pb = _dot(b_ref[...], wb_ref[...])
    o_ref[...] = (_sigmoid(ga_ref[...]) * pa + _sigmoid(gb_ref[...]) * pb).astype(o_ref.dtype)


def _merge(o_a, o_b, wa, wb, proj2, *, col_ga, col_gb):
    m, ka = o_a.shape
    kb_ = o_b.shape[1]
    d = wa.shape[1]
    tm = _pick_tile(m, 1024, 16)
    tn = _pick_tile(d, 512, LANES)
    ga0, gb0 = col_ga // tn, col_gb // tn
    blocks = tm * (ka + kb_) * 2 + (ka + kb_) * tn * 2 + 2 * tm * tn * 4 + tm * tn * 2
    return pl.pallas_call(
        _merge_kernel,
        grid=(m // tm, d // tn),
        in_specs=[
            pl.BlockSpec((tm, ka), lambda i, j: (i, 0)),
            pl.BlockSpec((tm, kb_), lambda i, j: (i, 0)),
            pl.BlockSpec((ka, tn), lambda i, j: (0, j)),
            pl.BlockSpec((kb_, tn), lambda i, j: (0, j)),
            pl.BlockSpec((tm, tn), lambda i, j: (i, ga0 + j)),
            pl.BlockSpec((tm, tn), lambda i, j: (i, gb0 + j)),
        ],
        out_specs=pl.BlockSpec((tm, tn), lambda i, j: (i, j)),
        out_shape=jax.ShapeDtypeStruct((m, d), BF16),
        compiler_params=pltpu.CompilerParams(
            dimension_semantics=("parallel", "parallel"),
            vmem_limit_bytes=_vmem_limit(blocks, 0, 4 * tm * tn * 4)),
        name="merge",
    )(o_a, o_b, wa, wb, proj2, proj2)


def _outproj_kernel(m_ref, w_ref, x_ref, g_ref, h_ref, hn_ref):
    h = x_ref[...] + _dot(m_ref[...], w_ref[...])
    h_ref[...] = h
    hn_ref[...] = _rms(h, g_ref[...]).astype(hn_ref.dtype)


def _outproj(merged, w_out, x2, g):
    m, d = x2.shape
    k = merged.shape[1]
    tm = _pick_tile(m, 512, 16)
    blocks = tm * k * 2 + k * d * 2 + 2 * tm * d * 4 + tm * d * 2 + d * 4
    return pl.pallas_call(
        _outproj_kernel,
        grid=(m // tm,),
        in_specs=[
            pl.BlockSpec((tm, k), lambda i: (i, 0)),
            pl.BlockSpec((k, d), lambda i: (0, 0)),
            pl.BlockSpec((tm, d), lambda i: (i, 0)),
            pl.BlockSpec((1, d), lambda i: (0, 0)),
        ],
        out_specs=[
            pl.BlockSpec((tm, d), lambda i: (i, 0)),
            pl.BlockSpec((tm, d), lambda i: (i, 0)),
        ],
        out_shape=[
            jax.ShapeDtypeStruct((m, d), F32),
            jax.ShapeDtypeStruct((m, d), BF16),
        ],
        compiler_params=pltpu.CompilerParams(
            dimension_semantics=("parallel",),
            vmem_limit_bytes=_vmem_limit(blocks, 0, 3 * tm * d * 4)),
        name="outproj",
    )(merged, w_out, x2, g)


def _ffn_kernel(hn_ref, wg_ref, wu_ref, wd_ref, h_ref, g_ref, o_ref, acc_ref, *, final_norm):
    f = pl.program_id(1)

    @pl.when(f == 0)
    def _():
        acc_ref[...] = jnp.zeros_like(acc_ref)

    hn = hn_ref[...]
    ff = _silu(_dot(hn, wg_ref[...])) * _dot(hn, wu_ref[...])
    acc_ref[...] += _dot(ff.astype(BF16), wd_ref[...])

    @pl.when(f == pl.num_programs(1) - 1)
    def _():
        h = h_ref[...] + acc_ref[...]
        o_ref[...] = _rms(h, g_ref[...]) if final_norm else h


def _ffn(hn, wg, wu, wd, h, g, *, final_norm):
    m, d = h.shape
    dff = wg.shape[1]
    tm = _pick_tile(m, 512, 16)
    tf = _pick_tile(dff, 512, LANES)
    blocks = tm * d * 2 + 3 * d * tf * 2 + 2 * tm * d * 4 + d * 4
    kern = functools.partial(_ffn_kernel, final_norm=final_norm)
    return pl.pallas_call(
        kern,
        grid=(m // tm, dff // tf),
        in_specs=[
            pl.BlockSpec((tm, d), lambda i, f: (i, 0)),
            pl.BlockSpec((d, tf), lambda i, f: (0, f)),
            pl.BlockSpec((d, tf), lambda i, f: (0, f)),
            pl.BlockSpec((tf, d), lambda i, f: (f, 0)),
            pl.BlockSpec((tm, d), lambda i, f: (i, 0)),
            pl.BlockSpec((1, d), lambda i, f: (0, 0)),
        ],
        out_specs=pl.BlockSpec((tm, d), lambda i, f: (i, 0)),
        out_shape=jax.ShapeDtypeStruct((m, d), F32),
        scratch_shapes=[pltpu.VMEM((tm, d), F32)],
        compiler_params=pltpu.CompilerParams(
            dimension_semantics=("parallel", "arbitrary"),
            vmem_limit_bytes=_vmem_limit(blocks, tm * d * 4, 4 * tm * tf * 4 + tm * d * 4)),
        name="ffn",
    )(hn, wg, wu, wd, h, g)


def kernel(x, ln_mix_g, w_in, conv_qkv_w, A_log, dt_bias, gdn_norm_g, w_proj_a, conv_sc_w, w_proj_b,
           w_out, ln_ffn_g, w_gate, w_up, w_down, ln_final_g):
    b, t, d = x.shape
    depth = w_in.shape[0]
    n_heads = A_log.shape[1]
    v_width = w_proj_a.shape[1]
    qk_width = (conv_qkv_w.shape[2] - v_width) // 2
    sc_width = w_proj_b.shape[1]
    head_dim = gdn_norm_g.shape[1]
    assert qk_width == n_heads * head_dim and v_width == n_heads * head_dim
    assert head_dim == LANES and n_heads % GDN_HEADS_PER_STEP == 0 and 2 * n_heads <= GATE_LANES
    assert t % GDN_CHUNK == 0

    col_q, col_k, col_v = 0, qk_width, 2 * qk_width
    col_z = col_v + v_width
    col_gates = col_z + v_width
    rest = col_gates + 2 * n_heads
    col_b = col_gates
    col_c = col_b + sc_width
    col_h = col_c + sc_width
    col_ga = col_h + sc_width
    col_gb = col_ga + d

    h2 = x.reshape(b * t, d)
    for l in range(depth):
        w_l = w_in[l]
        w_main = jnp.concatenate([w_l[:, :col_gates], w_l[:, rest:]], axis=1).astype(BF16)
        w_g = jnp.pad(w_l[:, col_gates:rest], ((0, 0), (0, GATE_LANES - 2 * n_heads)))
        w_g_hi = w_g.astype(BF16)
        w_g_lo = (w_g - w_g_hi.astype(F32)).astype(BF16)
        gate_params = jnp.zeros((8, GATE_LANES), F32)
        gate_params = gate_params.at[0, n_heads:2 * n_heads].set(A_log[l])
        gate_params = gate_params.at[1, n_heads:2 * n_heads].set(dt_bias[l])

        proj2, gates2 = _inproj(h2, ln_mix_g[l][None, :], w_main, w_g_hi, w_g_lo)
        proj3 = proj2.reshape(b, t, -1)
        o_a = _gdn(proj3, gates2.reshape(b, t, GATE_LANES), conv_qkv_w[l], gate_params,
                   gdn_norm_g[l][None, :], n_heads=n_heads, head_dim=head_dim,
                   col_q=col_q, col_k=col_k, col_v=col_v, col_z=col_z, v_width=v_width)
        o_b = _shortconv(proj3, conv_sc_w[l], col_b=col_b, col_c=col_c, col_h=col_h, width=sc_width)
        merged = _merge(o_a.reshape(b * t, v_width), o_b.reshape(b * t, sc_width),
                        w_proj_a[l].astype(BF16), w_proj_b[l].astype(BF16), proj2,
                        col_ga=col_ga, col_gb=col_gb)
        h_mid, hn = _outproj(merged, w_out[l].astype(BF16), h2, ln_ffn_g[l][None, :])
        last = l == depth - 1
        g_last = ln_final_g[None, :] if last else jnp.ones((1, d), F32)
        h2 = _ffn(hn, w_gate[l].astype(BF16), w_up[l].astype(BF16), w_down[l].astype(BF16),
                  h_mid, g_last, final_norm=last)
    return h2.reshape(b, t, d)
```

```python
import functools

import jax
import jax.numpy as jnp
from jax import lax
from jax.experimental import pallas as pl
from jax.experimental.pallas import tpu as pltpu

EPS = 1e-6
F32 = jnp.float32
BF16 = jnp.bfloat16

LANES = 128
V7X_VMEM_CAP_BYTES = 56 * 1024 * 1024
GDN_CHUNK = 128
GDN_HEADS_PER_STEP = 4
GATE_LANES = LANES


def _vmem_limit(block_bytes, scratch_bytes=0, temp_bytes=0):
    need = 2 * block_bytes + scratch_bytes + temp_bytes + (4 << 20)
    return int(min(V7X_VMEM_CAP_BYTES, max(need, 16 << 20)))


def _pick_tile(n, target, align):
    t = min(n, target)
    t -= t % align
    while t >= align:
        if n % t == 0:
            return t
        t -= align
    return n


def _sigmoid(x):
    return 1.0 / (1.0 + jnp.exp(-x))


def _silu(x):
    return x * _sigmoid(x)


def _dot(a, b):
    return jnp.dot(a, b, preferred_element_type=F32)


def _dot_nt(a, b):
    return lax.dot_general(a, b, (((1,), (1,)), ((), ())), preferred_element_type=F32)


def _dot_tn(a, b):
    return lax.dot_general(a, b, (((0,), (0,)), ((), ())), preferred_element_type=F32)


def _rms(x, g):
    return x * lax.rsqrt(jnp.mean(x * x, axis=-1, keepdims=True) + EPS) * g


def _inproj_kernel(x_ref, g_ref, w_ref, wg_hi_ref, wg_lo_ref, out_ref, gates_ref, xn_ref):
    @pl.when(pl.program_id(1) == 0)
    def _():
        y = _rms(x_ref[...], g_ref[...])
        hi = y.astype(BF16)
        xn_ref[...] = hi
        lo = (y - hi.astype(F32)).astype(BF16)
        gates_ref[...] = (_dot(hi, wg_hi_ref[...]) + _dot(hi, wg_lo_ref[...])
                          + _dot(lo, wg_hi_ref[...]))

    out_ref[...] = _dot(xn_ref[...], w_ref[...])


def _inproj(x2, g, w_main, wg_hi, wg_lo):
    m, d = x2.shape
    n = w_main.shape[1]
    tm = _pick_tile(m, 1024, 8)
    tn = _pick_tile(n, 512, LANES)
    blocks = tm * d * 4 + d * tn * 2 + tm * tn * 4 + tm * GATE_LANES * 4 + 2 * d * GATE_LANES * 2
    return pl.pallas_call(
        _inproj_kernel,
        grid=(m // tm, n // tn),
        in_specs=[
            pl.BlockSpec((tm, d), lambda i, j: (i, 0)),
            pl.BlockSpec((1, d), lambda i, j: (0, 0)),
            pl.BlockSpec((d, tn), lambda i, j: (0, j)),
            pl.BlockSpec((d, GATE_LANES), lambda i, j: (0, 0)),
            pl.BlockSpec((d, GATE_LANES), lambda i, j: (0, 0)),
        ],
        out_specs=[
            pl.BlockSpec((tm, tn), lambda i, j: (i, j)),
            pl.BlockSpec((tm, GATE_LANES), lambda i, j: (i, 0)),
        ],
        out_shape=[
            jax.ShapeDtypeStruct((m, n), F32),
            jax.ShapeDtypeStruct((m, GATE_LANES), F32),
        ],
        scratch_shapes=[pltpu.VMEM((tm, d), BF16)],
        compiler_params=pltpu.CompilerParams(
            dimension_semantics=("parallel", "arbitrary"),
            vmem_limit_bytes=_vmem_limit(blocks, tm * d * 2, tm * d * 8)),
        name="inproj",
    )(x2, g, w_main, wg_hi, wg_lo)


def _inv_unit_lower_minus_eye(a_list, rowi, coli):
    diag16 = (rowi >> 4) == (coli >> 4)
    x = [jnp.where(diag16, a, 0.0) for a in a_list]
    e = [-t for t in x]
    for _ in range(3):
        xb = [t.astype(BF16) for t in x]
        x = [_dot(t, t) for t in xb]
        ex = [_dot(ei.astype(BF16), xi.astype(BF16)) for ei, xi in zip(e, x)]
        e = [ei + xi + exi for ei, xi, exi in zip(e, x, ex)]
    level = 5
    while (1 << (level - 1)) < GDN_CHUNK:
        same_hi = (rowi >> level) == (coli >> level)
        same_lo = (rowi >> (level - 1)) == (coli >> (level - 1))
        off = jnp.logical_and(same_hi, jnp.logical_not(same_lo))
        y = [jnp.where(off, a, 0.0) for a in a_list]
        z = [yi + _dot(yi.astype(BF16), ei.astype(BF16)) for yi, ei in zip(y, e)]
        ez = [_dot(ei.astype(BF16), zi.astype(BF16)) for ei, zi in zip(e, z)]
        e = [ei - zi - ezi for ei, zi, ezi in zip(e, z, ez)]
        level += 1
    return e


def _gdn_kernel(q_ref, k_ref, v_ref, z_ref, gates_ref, cwq_ref, cwk_ref, cwv_ref, gp_ref, ng_ref,
                o_ref, s_ref, halo_ref, gct_ref, *, n_heads, head_dim):
    c_len = GDN_CHUNK
    hb = GDN_HEADS_PER_STEP
    tb = q_ref.shape[1]
    n_chunks = tb // c_len
    hg = pl.program_id(1)

    @pl.when(pl.program_id(2) == 0)
    def _():
        s_ref[...] = jnp.zeros_like(s_ref)
        halo_ref[...] = jnp.zeros_like(halo_ref)

    gates = gates_ref[0]
    gp = gp_ref[...]
    beta_all = _sigmoid(gates)
    xa = gates + gp[1:2, :]
    softplus = jnp.maximum(xa, 0.0) + jnp.log(1.0 + jnp.exp(-jnp.abs(xa)))
    gcum = -jnp.exp(gp[0:1, :]) * softplus
    row_in_chunk = lax.broadcasted_iota(jnp.int32, (tb, GATE_LANES), 0) & (c_len - 1)
    shift = 1
    while shift < c_len:
        gcum = gcum + jnp.where(row_in_chunk >= shift, pltpu.roll(gcum, shift, 0), 0.0)
        shift *= 2
    for c in range(n_chunks):
        gct_ref[c] = gcum[c * c_len:(c + 1) * c_len, :].T
    gate_lane = lax.broadcasted_iota(jnp.int32, (tb, GATE_LANES), 1)

    def conv_silu(x_ref, cw_ref, slot):
        x = x_ref[0]
        xc = jnp.concatenate([halo_ref[slot], x], axis=0)
        cw = cw_ref[...]
        taps = cw.shape[0]
        y = xc * cw[taps - 1:taps, :]
        for s in range(1, taps):
            y = y + pltpu.roll(xc, s, 0) * cw[taps - 1 - s:taps - s, :]
        halo_ref[slot] = x[tb - 8:tb, :]
        return _silu(y[8:, :])

    yq = conv_silu(q_ref, cwq_ref, 0)
    yk = conv_silu(k_ref, cwk_ref, 1)
    yv = conv_silu(v_ref, cwv_ref, 2)

    rowi = lax.broadcasted_iota(jnp.int32, (c_len, c_len), 0)
    coli = lax.broadcasted_iota(jnp.int32, (c_len, c_len), 1)
    incl = rowi >= coli
    strict = rowi > coli

    qh, kh, vh, beta, gcol = [], [], [], [], []
    for j in range(hb):
        lanes = slice(j * head_dim, (j + 1) * head_dim)
        head = hg * hb + j
        qj = yq[:, lanes]
        kj = yk[:, lanes]
        qh.append(qj * (lax.rsqrt(jnp.sum(qj * qj, axis=1, keepdims=True) + EPS) * head_dim ** -0.5))
        kh.append(kj * lax.rsqrt(jnp.sum(kj * kj, axis=1, keepdims=True) + EPS))
        vh.append(yv[:, lanes])
        beta.append(jnp.sum(jnp.where(gate_lane == head, beta_all, 0.0), axis=1, keepdims=True))
        gcol.append(jnp.sum(jnp.where(gate_lane == head + n_heads, gcum, 0.0), axis=1, keepdims=True))

    probs = [(j, c) for j in range(hb) for c in range(n_chunks)]
    rows = {c: slice(c * c_len, (c + 1) * c_len) for c in range(n_chunks)}
    g_c = [gcol[j][rows[c]] for j, c in probs]
    g_r = [gct_ref[c, pl.ds(hg * hb + j + n_heads, 1), :] for j, c in probs]
    g_last = [g[c_len - 1:c_len, :] for g in g_c]
    decay = [jnp.where(incl, jnp.exp(jnp.where(incl, gc - gr, 0.0)), 0.0) for gc, gr in zip(g_c, g_r)]
    e_g = [jnp.exp(g) for g in g_c]
    kc = [kh[j][rows[c]] for j, c in probs]
    qc = [qh[j][rows[c]] for j, c in probs]
    bb = [beta[j][rows[c]] for j, c in probs]
    kb = [k * b_ for k, b_ in zip(kc, bb)]
    kbf = [k.astype(BF16) for k in kc]
    kk = [_dot_nt(a.astype(BF16), b_) for a, b_ in zip(kb, kbf)]
    qk = [_dot_nt(a.astype(BF16), b_) for a, b_ in zip(qc, kbf)]
    a_low = [jnp.where(strict, t * d_, 0.0) for t, d_ in zip(kk, decay)]
    attn = [(t * d_).astype(BF16) for t, d_ in zip(qk, decay)]
    e_inv = _inv_unit_lower_minus_eye(a_low, rowi, coli)
    rhs = [jnp.concatenate([vh[j][rows[c]] * b_, kb_ * eg], axis=1)
           for (j, c), b_, kb_, eg in zip(probs, bb, kb, e_g)]
    sol = [r + _dot(e.astype(BF16), r.astype(BF16)) for r, e in zip(rhs, e_inv)]
    u = [t[:, :head_dim] for t in sol]
    wq = [jnp.concatenate([t[:, head_dim:], q * eg], axis=0).astype(BF16)
          for t, q, eg in zip(sol, qc, e_g)]
    k_dec = [(k * jnp.exp(gl - g)).astype(BF16) for k, gl, g in zip(kc, g_last, g_c)]
    e_last = [jnp.exp(gl) for gl in g_last]

    z = z_ref[0]
    ng = ng_ref[...]
    s = [s_ref[j] for j in range(hb)]
    for c in range(n_chunks):
        idx = [j * n_chunks + c for j in range(hb)]
        ws = [_dot(wq[i], s[j].astype(BF16)) for j, i in enumerate(idx)]
        v_bf = [(u[i] - ws[j][:c_len]).astype(BF16) for j, i in enumerate(idx)]
        o = [ws[j][c_len:] + _dot(attn[i], v_bf[j]) for j, i in enumerate(idx)]
        s = [s[j] * e_last[i] + _dot_tn(k_dec[i], v_bf[j]) for j, i in enumerate(idx)]
        for j in range(hb):
            lanes = slice(j * head_dim, (j + 1) * head_dim)
            zc = z[rows[c], lanes]
            o_ref[0, rows[c], lanes] = (_rms(o[j], ng) * _silu(zc)).astype(o_ref.dtype)
    for j in range(hb):
        s_ref[j] = s[j]


def _gdn(proj3, gates3, conv_w, gate_params, norm_g, *, n_heads, head_dim, col_q, col_k, col_v,
         col_z, v_width):
    b, t, _ = proj3.shape
    hb = GDN_HEADS_PER_STEP
    gw = hb * head_dim
    tb = _pick_tile(t, 2 * GDN_CHUNK, GDN_CHUNK)
    taps = conv_w.shape[0]
    qb, kb_, vb, zb = col_q // gw, col_k // gw, col_v // gw, col_z // gw
    blocks = 4 * tb * gw * 4 + tb * GATE_LANES * 4 + 3 * 8 * gw * 4 + tb * gw * 2
    scratch = hb * head_dim * head_dim * 4 + 3 * 8 * gw * 4 + (tb // GDN_CHUNK) * GDN_CHUNK * GATE_LANES * 4
    kern = functools.partial(_gdn_kernel, n_heads=n_heads, head_dim=head_dim)
    return pl.pallas_call(
        kern,
        grid=(b, n_heads // hb, t // tb),
        in_specs=[
            pl.BlockSpec((1, tb, gw), lambda i, h, s: (i, s, qb + h)),
            pl.BlockSpec((1, tb, gw), lambda i, h, s: (i, s, kb_ + h)),
            pl.BlockSpec((1, tb, gw), lambda i, h, s: (i, s, vb + h)),
            pl.BlockSpec((1, tb, gw), lambda i, h, s: (i, s, zb + h)),
            pl.BlockSpec((1, tb, GATE_LANES), lambda i, h, s: (i, s, 0)),
            pl.BlockSpec((taps, gw), lambda i, h, s: (0, qb + h)),
            pl.BlockSpec((taps, gw), lambda i, h, s: (0, kb_ + h)),
            pl.BlockSpec((taps, gw), lambda i, h, s: (0, vb + h)),
            pl.BlockSpec((8, GATE_LANES), lambda i, h, s: (0, 0)),
            pl.BlockSpec((1, head_dim), lambda i, h, s: (0, 0)),
        ],
        out_specs=pl.BlockSpec((1, tb, gw), lambda i, h, s: (i, s, h)),
        out_shape=jax.ShapeDtypeStruct((b, t, v_width), BF16),
        scratch_shapes=[
            pltpu.VMEM((hb, head_dim, head_dim), F32),
            pltpu.VMEM((3, 8, gw), F32),
            pltpu.VMEM((tb // GDN_CHUNK, GATE_LANES, GDN_CHUNK), F32),
        ],
        compiler_params=pltpu.CompilerParams(
            dimension_semantics=("parallel", "parallel", "arbitrary"),
            vmem_limit_bytes=_vmem_limit(blocks, scratch, 16 << 20)),
        name="gdn",
    )(proj3, proj3, proj3, proj3, gates3, conv_w, conv_w, conv_w, gate_params, norm_g)


def _shortconv_kernel(b_ref, c_ref, h_ref, cw_ref, o_ref):
    ch = c_ref[0] * h_ref[0]
    cw = cw_ref[...]
    taps = cw.shape[0]
    row = lax.broadcasted_iota(jnp.int32, ch.shape, 0)
    y = ch * cw[taps - 1:taps, :]
    for s in range(1, taps):
        y = y + jnp.where(row >= s, pltpu.roll(ch, s, 0), 0.0) * cw[taps - 1 - s:taps - s, :]
    o_ref[0] = (b_ref[0] * y).astype(o_ref.dtype)


def _shortconv(proj3, conv_w, *, col_b, col_c, col_h, width):
    b, t, _ = proj3.shape
    wb = _pick_tile(width, 2 * LANES, LANES)
    taps = conv_w.shape[0]
    bb, cb, hb = col_b // wb, col_c // wb, col_h // wb
    blocks = 3 * t * wb * 4 + t * wb * 2 + 8 * wb * 4
    return pl.pallas_call(
        _shortconv_kernel,
        grid=(b, width // wb),
        in_specs=[
            pl.BlockSpec((1, t, wb), lambda i, j: (i, 0, bb + j)),
            pl.BlockSpec((1, t, wb), lambda i, j: (i, 0, cb + j)),
            pl.BlockSpec((1, t, wb), lambda i, j: (i, 0, hb + j)),
            pl.BlockSpec((taps, wb), lambda i, j: (0, j)),
        ],
        out_specs=pl.BlockSpec((1, t, wb), lambda i, j: (i, 0, j)),
        out_shape=jax.ShapeDtypeStruct((b, t, width), BF16),
        compiler_params=pltpu.CompilerParams(
            dimension_semantics=("parallel", "parallel"),
            vmem_limit_bytes=_vmem_limit(blocks, 0, 4 * t * wb * 4)),
        name="shortconv",
    )(proj3, proj3, proj3, conv_w)


def _merge_kernel(a_ref, b_ref, wa_ref, wb_ref, ga_ref, gb_ref, o_ref):
    pa = _dot(a_ref[...], wa_ref[...])
    pb = _dot(b_ref[...], wb_ref[...])
    o_ref[...] = (_sigmoid(ga_ref[...]) * pa + _sigmoid(gb_ref[...]) * pb).astype(o_ref.dtype)


def _merge(o_a, o_b, wa, wb, proj2, *, col_ga, col_gb):
    m, ka = o_a.shape
    kb_ = o_b.shape[1]
    d = wa.shape[1]
    tm = _pick_tile(m, 1024, 16)
    tn = _pick_tile(d, 512, LANES)
    ga0, gb0 = col_ga // tn, col_gb // tn
    blocks = tm * (ka + kb_) * 2 + (ka + kb_) * tn * 2 + 2 * tm * tn * 4 + tm * tn * 2
    return pl.pallas_call(
        _merge_kernel,
        grid=(m // tm, d // tn),
        in_specs=[
            pl.BlockSpec((tm, ka), lambda i, j: (i, 0)),
            pl.BlockSpec((tm, kb_), lambda i, j: (i, 0)),
            pl.BlockSpec((ka, tn), lambda i, j: (0, j)),
            pl.BlockSpec((kb_, tn), lambda i, j: (0, j)),
            pl.BlockSpec((tm, tn), lambda i, j: (i, ga0 + j)),
            pl.BlockSpec((tm, tn), lambda i, j: (i, gb0 + j)),
        ],
        out_specs=pl.BlockSpec((tm, tn), lambda i, j: (i, j)),
        out_shape=jax.ShapeDtypeStruct((m, d), BF16),
        compiler_params=pltpu.CompilerParams(
            dimension_semantics=("parallel", "parallel"),
            vmem_limit_bytes=_vmem_limit(blocks, 0, 4 * tm * tn * 4)),
        name="merge",
    )(o_a, o_b, wa, wb, proj2, proj2)


def _outproj_kernel(m_ref, w_ref, x_ref, g_ref, h_ref, hn_ref):
    h = x_ref[...] + _dot(m_ref[...], w_ref[...])
    h_ref[...] = h
    hn_ref[...] = _rms(h, g_ref[...]).astype(hn_ref.dtype)


def _outproj(merged, w_out, x2, g):
    m, d = x2.shape
    k = merged.shape[1]
    tm = _pick_tile(m, 512, 16)
    blocks = tm * k * 2 + k * d * 2 + 2 * tm * d * 4 + tm * d * 2 + d * 4
    return pl.pallas_call(
        _outproj_kernel,
        grid=(m // tm,),
        in_specs=[
            pl.BlockSpec((tm, k), lambda i: (i, 0)),
            pl.BlockSpec((k, d), lambda i: (0, 0)),
            pl.BlockSpec((tm, d), lambda i: (i, 0)),
            pl.BlockSpec((1, d), lambda i: (0, 0)),
        ],
        out_specs=[
            pl.BlockSpec((tm, d), lambda i: (i, 0)),
            pl.BlockSpec((tm, d), lambda i: (i, 0)),
        ],
        out_shape=[
            jax.ShapeDtypeStruct((m, d), F32),
            jax.ShapeDtypeStruct((m, d), BF16),
        ],
        compiler_params=pltpu.CompilerParams(
            dimension_semantics=("parallel",),
            vmem_limit_bytes=_vmem_limit(blocks, 0, 3 * tm * d * 4)),
        name="outproj",
    )(merged, w_out, x2, g)


def _ffn_kernel(hn_ref, wg_ref, wu_ref, wd_ref, h_ref, g_ref, o_ref, acc_ref, *, final_norm):
    f = pl.program_id(1)

    @pl.when(f == 0)
    def _():
        acc_ref[...] = jnp.zeros_like(acc_ref)

    hn = hn_ref[...]
    ff = _silu(_dot(hn, wg_ref[...])) * _dot(hn, wu_ref[...])
    acc_ref[...] += _dot(ff.astype(BF16), wd_ref[...])

    @pl.when(f == pl.num_programs(1) - 1)
    def _():
        h = h_ref[...] + acc_ref[...]
        o_ref[...] = _rms(h, g_ref[...]) if final_norm else h


def _ffn(hn, wg, wu, wd, h, g, *, final_norm):
    m, d = h.shape
    dff = wg.shape[1]
    tm = _pick_tile(m, 512, 16)
    tf = _pick_tile(dff, 512, LANES)
    blocks = tm * d * 2 + 3 * d * tf * 2 + 2 * tm * d * 4 + d * 4
    kern = functools.partial(_ffn_kernel, final_norm=final_norm)
    return pl.pallas_call(
        kern,
        grid=(m // tm, dff // tf),
        in_specs=[
            pl.BlockSpec((tm, d), lambda i, f: (i, 0)),
            pl.BlockSpec((d, tf), lambda i, f: (0, f)),
            pl.BlockSpec((d, tf), lambda i, f: (0, f)),
            pl.BlockSpec((tf, d), lambda i, f: (f, 0)),
            pl.BlockSpec((tm, d), lambda i, f: (i, 0)),
            pl.BlockSpec((1, d), lambda i, f: (0, 0)),
        ],
        out_specs=pl.BlockSpec((tm, d), lambda i, f: (i, 0)),
        out_shape=jax.ShapeDtypeStruct((m, d), F32),
        scratch_shapes=[pltpu.VMEM((tm, d), F32)],
        compiler_params=pltpu.CompilerParams(
            dimension_semantics=("parallel", "arbitrary"),
            vmem_limit_bytes=_vmem_limit(blocks, tm * d * 4, 4 * tm * tf * 4 + tm * d * 4)),
        name="ffn",
    )(hn, wg, wu, wd, h, g)


def kernel(x, ln_mix_g, w_in, conv_qkv_w, A_log, dt_bias, gdn_norm_g, w_proj_a, conv_sc_w, w_proj_b,
           w_out, ln_ffn_g, w_gate, w_up, w_down, ln_final_g):
    b, t, d = x.shape
    depth = w_in.shape[0]
    n_heads = A_log.shape[1]
    v_width = w_proj_a.shape[1]
    qk_width = (conv_qkv_w.shape[2] - v_width) // 2
    sc_width = w_proj_b.shape[1]
    head_dim = gdn_norm_g.shape[1]
    assert qk_width == n_heads * head_dim and v_width == n_heads * head_dim
    assert head_dim == LANES and n_heads % GDN_HEADS_PER_STEP == 0 and 2 * n_heads <= GATE_LANES
    assert t % GDN_CHUNK == 0

    col_q, col_k, col_v = 0, qk_width, 2 * qk_width
    col_z = col_v + v_width
    col_gates = col_z + v_width
    rest = col_gates + 2 * n_heads
    col_b = col_gates
    col_c = col_b + sc_width
    col_h = col_c + sc_width
    col_ga = col_h + sc_width
    col_gb = col_ga + d

    h2 = x.reshape(b * t, d)
    for l in range(depth):
        w_l = w_in[l]
        w_main = jnp.concatenate([w_l[:, :col_gates], w_l[:, rest:]], axis=1).astype(BF16)
        w_g = jnp.pad(w_l[:, col_gates:rest], ((0, 0), (0, GATE_LANES - 2 * n_heads)))
        w_g_hi = w_g.astype(BF16)
        w_g_lo = (w_g - w_g_hi.astype(F32)).astype(BF16)
        gate_params = jnp.zeros((8, GATE_LANES), F32)
        gate_params = gate_params.at[0, n_heads:2 * n_heads].set(A_log[l])
        gate_params = gate_params.at[1, n_heads:2 * n_heads].set(dt_bias[l])

        proj2, gates2 = _inproj(h2, ln_mix_g[l][None, :], w_main, w_g_hi, w_g_lo)
        proj3 = proj2.reshape(b, t, -1)
        o_a = _gdn(proj3, gates2.reshape(b, t, GATE_LANES), conv_qkv_w[l], gate_params,
                   gdn_norm_g[l][None, :], n_heads=n_heads, head_dim=head_dim,
                   col_q=col_q, col_k=col_k, col_v=col_v, col_z=col_z, v_width=v_width)
        o_b = _shortconv(proj3, conv_sc_w[l], col_b=col_b, col_c=col_c, col_h=col_h, width=sc_width)
        merged = _merge(o_a.reshape(b * t, v_width), o_b.reshape(b * t, sc_width),
                        w_proj_a[l].astype(BF16), w_proj_b[l].astype(BF16), proj2,
                        col_ga=col_ga, col_gb=col_gb)
        h_mid, hn = _outproj(merged, w_out[l].astype(BF16), h2, ln_ffn_g[l][None, :])
        last = l == depth - 1
        g_last = ln_final_g[None, :] if last else jnp.ones((1, d), F32)
        h2 = _ffn(hn, w_gate[l].astype(BF16), w_up[l].astype(BF16), w_down[l].astype(BF16),
                  h_mid, g_last, final_norm=last)
    return h2.reshape(b, t, d)
```

```python
import functools

import jax
import jax.numpy as jnp
from jax import lax
from jax.experimental import pallas as pl
from jax.experimental.pallas import tpu as pltpu

EPS = 1e-6
F32 = jnp.float32
BF16 = jnp.bfloat16

LANES = 128
V7X_VMEM_CAP_BYTES = 56 * 1024 * 1024
GDN_CHUNK = 128
GDN_HEADS_PER_STEP = 4
GATE_LANES = LANES


def _vmem_limit(block_bytes, scratch_bytes=0, temp_bytes=0):
    need = 2 * block_bytes + scratch_bytes + temp_bytes + (4 << 20)
    return int(min(V7X_VMEM_CAP_BYTES, max(need, 16 << 20)))


def _pick_tile(n, target, align):
    t = min(n, target)
    t -= t % align
    while t >= align:
        if n % t == 0:
            return t
        t -= align
    return n


def _sigmoid(x):
    return 1.0 / (1.0 + jnp.exp(-x))


def _silu(x):
    return x * _sigmoid(x)


def _dot(a, b):
    return jnp.dot(a, b, preferred_element_type=F32)


def _dot_nt(a, b):
    return lax.dot_general(a, b, (((1,), (1,)), ((), ())), preferred_element_type=F32)


def _dot_tn(a, b):
    return lax.dot_general(a, b, (((0,), (0,)), ((), ())), preferred_element_type=F32)


def _rms(x, g):
    return x * lax.rsqrt(jnp.mean(x * x, axis=-1, keepdims=True) + EPS) * g


def _wprep_kernel(wt_ref, o_ref):
    o_ref[...] = wt_ref[...].T.astype(o_ref.dtype)


def _wprep(w_t, *, col_gates, gate_cols):
    n, d = w_t.shape
    n_out = n - gate_cols
    tn = _pick_tile(n_out, 512, LANES)
    assert col_gates % tn == 0 and gate_cols % 8 == 0
    n_plain = col_gates // tn
    blocks = d * tn * 4 + d * tn * 2
    return pl.pallas_call(
        _wprep_kernel,
        grid=(n_out // tn,),
        in_specs=[pl.BlockSpec((pl.Element(tn), pl.Element(d)),
                               lambda j: (pl.multiple_of(j * tn + jnp.where(j >= n_plain, gate_cols, 0), 8), 0))],
        out_specs=pl.BlockSpec((d, tn), lambda j: (0, j)),
        out_shape=jax.ShapeDtypeStruct((d, n_out), BF16),
        compiler_params=pltpu.CompilerParams(
            dimension_semantics=("parallel",),
            vmem_limit_bytes=_vmem_limit(blocks, 0, 2 * d * tn * 4)),
        name="wprep",
    )(w_t)


def _gprep_kernel(g_ref, hi_ref, lo_ref):
    g = g_ref[...]
    pad = jnp.zeros((GATE_LANES - g.shape[0], g.shape[1]), F32)
    w = jnp.concatenate([g, pad], axis=0).T
    hi = w.astype(BF16)
    hi_ref[...] = hi
    lo_ref[...] = (w - hi.astype(F32)).astype(BF16)


def _gprep(w_t, *, col_gates, gate_cols):
    d = w_t.shape[1]
    out = jax.ShapeDtypeStruct((d, GATE_LANES), BF16)
    return pl.pallas_call(
        _gprep_kernel,
        grid=(1,),
        in_specs=[pl.BlockSpec((pl.Element(gate_cols), pl.Element(d)), lambda i: (col_gates, 0))],
        out_specs=[pl.BlockSpec((d, GATE_LANES), lambda i: (0, 0))] * 2,
        out_shape=[out, out],
        name="gprep",
    )(w_t)


def _inproj_kernel(x_ref, g_ref, w_ref, wg_hi_ref, wg_lo_ref, out_ref, gates_ref, xn_ref):
    @pl.when(pl.program_id(1) == 0)
    def _():
        y = _rms(x_ref[...], g_ref[...])
        hi = y.astype(BF16)
        xn_ref[...] = hi
        lo = (y - hi.astype(F32)).astype(BF16)
        gates_ref[...] = (_dot(hi, wg_hi_ref[...]) + _dot(hi, wg_lo_ref[...])
                          + _dot(lo, wg_hi_ref[...]))

    out_ref[...] = _dot(xn_ref[...], w_ref[...])


def _inproj(x2, g, w_main, wg_hi, wg_lo):
    m, d = x2.shape
    n = w_main.shape[1]
    tm = _pick_tile(m, 1024, 8)
    tn = _pick_tile(n, 512, LANES)
    blocks = tm * d * 4 + d * tn * 2 + tm * tn * 4 + tm * GATE_LANES * 4 + 2 * d * GATE_LANES * 2
    return pl.pallas_call(
        _inproj_kernel,
        grid=(m // tm, n // tn),
        in_specs=[
            pl.BlockSpec((tm, d), lambda i, j: (i, 0)),
            pl.BlockSpec((1, d), lambda i, j: (0, 0)),
            pl.BlockSpec((d, tn), lambda i, j: (0, j)),
            pl.BlockSpec((d, GATE_LANES), lambda i, j: (0, 0)),
            pl.BlockSpec((d, GATE_LANES), lambda i, j: (0, 0)),
        ],
        out_specs=[
            pl.BlockSpec((tm, tn), lambda i, j: (i, j)),
            pl.BlockSpec((tm, GATE_LANES), lambda i, j: (i, 0)),
        ],
        out_shape=[
            jax.ShapeDtypeStruct((m, n), F32),
            jax.ShapeDtypeStruct((m, GATE_LANES), F32),
        ],
        scratch_shapes=[pltpu.VMEM((tm, d), BF16)],
        compiler_params=pltpu.CompilerParams(
            dimension_semantics=("parallel", "arbitrary"),
            vmem_limit_bytes=_vmem_limit(blocks, tm * d * 2, tm * d * 8)),
        name="inproj",
    )(x2, g, w_main, wg_hi, wg_lo)


def _inv_unit_lower_minus_eye(a_list, rowi, coli):
    diag16 = (rowi >> 4) == (coli >> 4)
    x = [jnp.where(diag16, a, 0.0) for a in a_list]
    e = [-t for t in x]
    for _ in range(3):
        xb = [t.astype(BF16) for t in x]
        x = [_dot(t, t) for t in xb]
        ex = [_dot(ei.astype(BF16), xi.astype(BF16)) for ei, xi in zip(e, x)]
        e = [ei + xi + exi for ei, xi, exi in zip(e, x, ex)]
    level = 5
    while (1 << (level - 1)) < GDN_CHUNK:
        same_hi = (rowi >> level) == (coli >> level)
        same_lo = (rowi >> (level - 1)) == (coli >> (level - 1))
        off = jnp.logical_and(same_hi, jnp.logical_not(same_lo))
        y = [jnp.where(off, a, 0.0) for a in a_list]
        z = [yi + _dot(yi.astype(BF16), ei.astype(BF16)) for yi, ei in zip(y, e)]
        ez = [_dot(ei.astype(BF16), zi.astype(BF16)) for ei, zi in zip(e, z)]
        e = [ei - zi - ezi for ei, zi, ezi in zip(e, z, ez)]
        level += 1
    return e


def _gdn_kernel(q_ref, k_ref, v_ref, z_ref, gates_ref, cwq_ref, cwk_ref, cwv_ref, gp_ref, ng_ref,
                o_ref, s_ref, halo_ref, gct_ref, *, n_heads, head_dim):
    c_len = GDN_CHUNK
    hb = GDN_HEADS_PER_STEP
    tb = q_ref.shape[1]
    n_chunks = tb // c_len
    hg = pl.program_id(1)

    @pl.when(pl.program_id(2) == 0)
    def _():
        s_ref[...] = jnp.zeros_like(s_ref)
        halo_ref[:, :, 0:8, :] = jnp.zeros((3, hb, 8, head_dim), F32)

    gates = gates_ref[0]
    gp = gp_ref[...]
    beta_all = _sigmoid(gates)
    xa = gates + gp[1:2, :]
    softplus = jnp.maximum(xa, 0.0) + jnp.log(1.0 + jnp.exp(-jnp.abs(xa)))
    gcum = -jnp.exp(gp[0:1, :]) * softplus
    row_in_chunk = lax.broadcasted_iota(jnp.int32, (tb, GATE_LANES), 0) & (c_len - 1)
    shift = 1
    while shift < c_len:
        gcum = gcum + jnp.where(row_in_chunk >= shift, pltpu.roll(gcum, shift, 0), 0.0)
        shift *= 2
    for c in range(n_chunks):
        gct_ref[c] = gcum[c * c_len:(c + 1) * c_len, :].T
    gate_lane = lax.broadcasted_iota(jnp.int32, (tb, GATE_LANES), 1)

    def conv_silu(x_ref, cw_ref, slot):
        cw = cw_ref[...]
        taps = cw.shape[0]
        out = []
        for j in range(hb):
            lanes = slice(j * head_dim, (j + 1) * head_dim)
            x = x_ref[0, :, lanes]
            halo_ref[slot, j, pl.ds(8, tb), :] = x
            y = x * cw[taps - 1:taps, lanes]
            for s in range(1, taps):
                y = y + halo_ref[slot, j, pl.ds(8 - s, tb), :] * cw[taps - 1 - s:taps - s, lanes]
            halo_ref[slot, j, pl.ds(0, 8), :] = x[tb - 8:tb, :]
            out.append(_silu(y))
        return out

    yq = conv_silu(q_ref, cwq_ref, 0)
    yk = conv_silu(k_ref, cwk_ref, 1)
    yv = conv_silu(v_ref, cwv_ref, 2)

    rowi = lax.broadcasted_iota(jnp.int32, (c_len, c_len), 0)
    coli = lax.broadcasted_iota(jnp.int32, (c_len, c_len), 1)
    incl = rowi >= coli
    strict = rowi > coli

    qh, kh, vh, beta, gcol = [], [], [], [], []
    for j in range(hb):
        head = hg * hb + j
        qj = yq[j]
        kj = yk[j]
        qh.append(qj * (lax.rsqrt(jnp.sum(qj * qj, axis=1, keepdims=True) + EPS) * head_dim ** -0.5))
        kh.append(kj * lax.rsqrt(jnp.sum(kj * kj, axis=1, keepdims=True) + EPS))
        vh.append(yv[j])
        beta.append(jnp.sum(jnp.where(gate_lane == head, beta_all, 0.0), axis=1, keepdims=True))
        gcol.append(jnp.sum(jnp.where(gate_lane == head + n_heads, gcum, 0.0), axis=1, keepdims=True))

    probs = [(j, c) for j in range(hb) for c in range(n_chunks)]
    rows = {c: slice(c * c_len, (c + 1) * c_len) for c in range(n_chunks)}
    g_c = [gcol[j][rows[c]] for j, c in probs]
    g_r = [gct_ref[c, pl.ds(hg * hb + j + n_heads, 1), :] for j, c in probs]
    g_last = [g[c_len - 1:c_len, :] for g in g_c]
    decay = [jnp.where(incl, jnp.exp(jnp.where(incl, gc - gr, 0.0)), 0.0) for gc, gr in zip(g_c, g_r)]
    e_g = [jnp.exp(g) for g in g_c]
    kc = [kh[j][rows[c]] for j, c in probs]
    qc = [qh[j][rows[c]] for j, c in probs]
    bb = [beta[j][rows[c]] for j, c in probs]
    kb = [k * b_ for k, b_ in zip(kc, bb)]
    kbf = [k.astype(BF16) for k in kc]
    kk = [_dot_nt(a.astype(BF16), b_) for a, b_ in zip(kb, kbf)]
    qk = [_dot_nt(a.astype(BF16), b_) for a, b_ in zip(qc, kbf)]
    a_low = [jnp.where(strict, t * d_, 0.0) for t, d_ in zip(kk, decay)]
    attn = [(t * d_).astype(BF16) for t, d_ in zip(qk, decay)]
    e_inv = _inv_unit_lower_minus_eye(a_low, rowi, coli)
    rhs = [jnp.concatenate([vh[j][rows[c]] * b_, kb_ * eg], axis=1)
           for (j, c), b_, kb_, eg in zip(probs, bb, kb, e_g)]
    sol = [r + _dot(e.astype(BF16), r.astype(BF16)) for r, e in zip(rhs, e_inv)]
    u = [t[:, :head_dim] for t in sol]
    wq = [jnp.concatenate([t[:, head_dim:], q * eg], axis=0).astype(BF16)
          for t, q, eg in zip(sol, qc, e_g)]
    k_dec = [(k * jnp.exp(gl - g)).astype(BF16) for k, gl, g in zip(kc, g_last, g_c)]
    e_last = [jnp.exp(gl) for gl in g_last]

    z = z_ref[0]
    ng = ng_ref[...]
    s = [s_ref[j] for j in range(hb)]
    for c in range(n_chunks):
        idx = [j * n_chunks + c for j in range(hb)]
        ws = [_dot(wq[i], s[j].astype(BF16)) for j, i in enumerate(idx)]
        v_bf = [(u[i] - ws[j][:c_len]).astype(BF16) for j, i in enumerate(idx)]
        o = [ws[j][c_len:] + _dot(attn[i], v_bf[j]) for j, i in enumerate(idx)]
        s = [s[j] * e_last[i] + _dot_tn(k_dec[i], v_bf[j]) for j, i in enumerate(idx)]
        for j in range(hb):
            lanes = slice(j * head_dim, (j + 1) * head_dim)
            zc = z[rows[c], lanes]
            o_ref[0, rows[c], lanes] = (_rms(o[j], ng) * _silu(zc)).astype(o_ref.dtype)
    for j in range(hb):
        s_ref[j] = s[j]


def _gdn(proj3, gates3, conv_w, gate_params, norm_g, *, n_heads, head_dim, col_q, col_k, col_v,
         col_z, v_width):
    b, t, _ = proj3.shape
    hb = GDN_HEADS_PER_STEP
    gw = hb * head_dim
    tb = _pick_tile(t, 4 * GDN_CHUNK, GDN_CHUNK)
    taps = conv_w.shape[0]
    qb, kb_, vb, zb = col_q // gw, col_k // gw, col_v // gw, col_z // gw
    blocks = 4 * tb * gw * 4 + tb * GATE_LANES * 4 + 3 * 8 * gw * 4 + tb * gw * 2
    scratch = hb * head_dim * head_dim * 4 + 3 * (tb + 8) * gw * 4 + (tb // GDN_CHUNK) * GDN_CHUNK * GATE_LANES * 4
    kern = functools.partial(_gdn_kernel, n_heads=n_heads, head_dim=head_dim)
    return pl.pallas_call(
        kern,
        grid=(b, n_heads // hb, t // tb),
        in_specs=[
            pl.BlockSpec((1, tb, gw), lambda i, h, s: (i, s, qb + h)),
            pl.BlockSpec((1, tb, gw), lambda i, h, s: (i, s, kb_ + h)),
            pl.BlockSpec((1, tb, gw), lambda i, h, s: (i, s, vb + h)),
            pl.BlockSpec((1, tb, gw), lambda i, h, s: (i, s, zb + h)),
            pl.BlockSpec((1, tb, GATE_LANES), lambda i, h, s: (i, s, 0)),
            pl.BlockSpec((taps, gw), lambda i, h, s: (0, qb + h)),
            pl.BlockSpec((taps, gw), lambda i, h, s: (0, kb_ + h)),
            pl.BlockSpec((taps, gw), lambda i, h, s: (0, vb + h)),
            pl.BlockSpec((8, GATE_LANES), lambda i, h, s: (0, 0)),
            pl.BlockSpec((1, head_dim), lambda i, h, s: (0, 0)),
        ],
        out_specs=pl.BlockSpec((1, tb, gw), lambda i, h, s: (i, s, h)),
        out_shape=jax.ShapeDtypeStruct((b, t, v_width), BF16),
        scratch_shapes=[
            pltpu.VMEM((hb, head_dim, head_dim), F32),
            pltpu.VMEM((3, hb, tb + 8, head_dim), F32),
            pltpu.VMEM((tb // GDN_CHUNK, GATE_LANES, GDN_CHUNK), F32),
        ],
        compiler_params=pltpu.CompilerParams(
            dimension_semantics=("parallel", "parallel", "arbitrary"),
            vmem_limit_bytes=_vmem_limit(blocks, scratch, 16 << 20)),
        name="gdn",
    )(proj3, proj3, proj3, proj3, gates3, conv_w, conv_w, conv_w, gate_params, norm_g)


def _shortconv_kernel(b_ref, c_ref, h_ref, cw_ref, o_ref):
    ch = c_ref[0] * h_ref[0]
    cw = cw_ref[...]
    taps = cw.shape[0]
    row = lax.broadcasted_iota(jnp.int32, ch.shape, 0)
    y = ch * cw[taps - 1:taps, :]
    for s in range(1, taps):
        y = y + jnp.where(row >= s, pltpu.roll(ch, s, 0), 0.0) * cw[taps - 1 - s:taps - s, :]
    o_ref[0] = (b_ref[0] * y).astype(o_ref.dtype)


def _shortconv(proj3, conv_w, *, col_b, col_c, col_h, width):
    b, t, _ = proj3.shape
    wb = _pick_tile(width, 2 * LANES, LANES)
    taps = conv_w.shape[0]
    bb, cb, hb = col_b // wb, col_c // wb, col_h // wb
    blocks = 3 * t * wb * 4 + t * wb * 2 + 8 * wb * 4
    return pl.pallas_call(
        _shortconv_kernel,
        grid=(b, width // wb),
        in_specs=[
            pl.BlockSpec((1, t, wb), lambda i, j: (i, 0, bb + j)),
            pl.BlockSpec((1, t, wb), lambda i, j: (i, 0, cb + j)),
            pl.BlockSpec((1, t, wb), lambda i, j: (i, 0, hb + j)),
            pl.BlockSpec((taps, wb), lambda i, j: (0, j)),
        ],
        out_specs=pl.BlockSpec((1, t, wb), lambda i, j: (i, 0, j)),
        out_shape=jax.ShapeDtypeStruct((b, t, width), BF16),
        compiler_params=pltpu.CompilerParams(
            dimension_semantics=("parallel", "parallel"),
            vmem_limit_bytes=_vmem_limit(blocks, 0, 4 * t * wb * 4)),
        name="shortconv",
    )(proj3, proj3, proj3, conv_w)


def _merge_kernel(a_ref, b_ref, wa_ref, wb_ref, ga_ref, gb_ref, o_ref):
    pa = _dot(a_ref[...], wa_ref[...])
    pb = _dot(b_ref[...], wb_ref[...])
    o_ref[...] = (_sigmoid(ga_ref[...]) * pa + _sigmoid(gb_ref[...]) * pb).astype(o_ref.dtype)


def _merge(o_a, o_b, wa, wb, proj2, *, col_ga, col_gb):
    m, ka = o_a.shape
    kb_ = o_b.shape[1]
    d = wa.shape[1]
    tm = _pick_tile(m, 1024, 16)
    tn = _pick_tile(d, 512, LANES)
    ga0, gb0 = col_ga // tn, col_gb // tn
    blocks = tm * (ka + kb_) * 2 + (ka + kb_) * tn * 2 + 2 * tm * tn * 4 + tm * tn * 2
    return pl.pallas_call(
        _merge_kernel,
        grid=(m // tm, d // tn),
        in_specs=[
            pl.BlockSpec((tm, ka), lambda i, j: (i, 0)),
            pl.BlockSpec((tm, kb_), lambda i, j: (i, 0)),
            pl.BlockSpec((ka, tn), lambda i, j: (0, j)),
            pl.BlockSpec((kb_, tn), lambda i, j: (0, j)),
            pl.BlockSpec((tm, tn), lambda i, j: (i, ga0 + j)),
            pl.BlockSpec((tm, tn), lambda i, j: (i, gb0 + j)),
        ],
        out_specs=pl.BlockSpec((tm, tn), lambda i, j: (i, j)),
        out_shape=jax.ShapeDtypeStruct((m, d), BF16),
        compiler_params=pltpu.CompilerParams(
            dimension_semantics=("parallel", "parallel"),
            vmem_limit_bytes=_vmem_limit(blocks, 0, 4 * tm * tn * 4)),
        name="merge",
    )(o_a, o_b, wa, wb, proj2, proj2)


def _outproj_kernel(m_ref, w_ref, x_ref, g_ref, h_ref, hn_ref):
    h = x_ref[...] + _dot(m_ref[...], w_ref[...])
    h_ref[...] = h
    hn_ref[...] = _rms(h, g_ref[...]).astype(hn_ref.dtype)


def _outproj(merged, w_out, x2, g):
    m, d = x2.shape
    k = merged.shape[1]
    tm = _pick_tile(m, 512, 16)
    blocks = tm * k * 2 + k * d * 2 + 2 * tm * d * 4 + tm * d * 2 + d * 4
    return pl.pallas_call(
        _outproj_kernel,
        grid=(m // tm,),
        in_specs=[
            pl.BlockSpec((tm, k), lambda i: (i, 0)),
            pl.BlockSpec((k, d), lambda i: (0, 0)),
            pl.BlockSpec((tm, d), lambda i: (i, 0)),
            pl.BlockSpec((1, d), lambda i: (0, 0)),
        ],
        out_specs=[
            pl.BlockSpec((tm, d), lambda i: (i, 0)),
            pl.BlockSpec((tm, d), lambda i: (i, 0)),
        ],
        out_shape=[
            jax.ShapeDtypeStruct((m, d), F32),
            jax.ShapeDtypeStruct((m, d), BF16),
        ],
        compiler_params=pltpu.CompilerParams(
            dimension_semantics=("parallel",),
            vmem_limit_bytes=_vmem_limit(blocks, 0, 3 * tm * d * 4)),
        name="outproj",
    )(merged, w_out, x2, g)


def _ffn_kernel(hn_ref, wg_ref, wu_ref, wd_ref, h_ref, g_ref, o_ref, acc_ref, *, final_norm):
    f = pl.program_id(1)

    @pl.when(f == 0)
    def _():
        acc_ref[...] = jnp.zeros_like(acc_ref)

    hn = hn_ref[...]
    ff = _silu(_dot(hn, wg_ref[...])) * _dot(hn, wu_ref[...])
    acc_ref[...] += _dot(ff.astype(BF16), wd_ref[...])

    @pl.when(f == pl.num_programs(1) - 1)
    def _():
        h = h_ref[...] + acc_ref[...]
        o_ref[...] = _rms(h, g_ref[...]) if final_norm else h


def _ffn(hn, wg, wu, wd, h, g, *, final_norm):
    m, d = h.shape
    dff = wg.shape[1]
    tm = _pick_tile(m, 512, 16)
    tf = _pick_tile(dff, 512, LANES)
    blocks = tm * d * 2 + 3 * d * tf * 2 + 2 * tm * d * 4 + d * 4
    kern = functools.partial(_ffn_kernel, final_norm=final_norm)
    return pl.pallas_call(
        kern,
        grid=(m // tm, dff // tf),
        in_specs=[
            pl.BlockSpec((tm, d), lambda i, f: (i, 0)),
            pl.BlockSpec((d, tf), lambda i, f: (0, f)),
            pl.BlockSpec((d, tf), lambda i, f: (0, f)),
            pl.BlockSpec((tf, d), lambda i, f: (f, 0)),
            pl.BlockSpec((tm, d), lambda i, f: (i, 0)),
            pl.BlockSpec((1, d), lambda i, f: (0, 0)),
        ],
        out_specs=pl.BlockSpec((tm, d), lambda i, f: (i, 0)),
        out_shape=jax.ShapeDtypeStruct((m, d), F32),
        scratch_shapes=[pltpu.VMEM((tm, d), F32)],
        compiler_params=pltpu.CompilerParams(
            dimension_semantics=("parallel", "arbitrary"),
            vmem_limit_bytes=_vmem_limit(blocks, tm * d * 4, 4 * tm * tf * 4 + tm * d * 4)),
        name="ffn",
    )(hn, wg, wu, wd, h, g)


def kernel(x, ln_mix_g, w_in, conv_qkv_w, A_log, dt_bias, gdn_norm_g, w_proj_a, conv_sc_w, w_proj_b,
           w_out, ln_ffn_g, w_gate, w_up, w_down, ln_final_g):
    b, t, d = x.shape
    depth = w_in.shape[0]
    n_heads = A_log.shape[1]
    v_width = w_proj_a.shape[1]
    qk_width = (conv_qkv_w.shape[2] - v_width) // 2
    sc_width = w_proj_b.shape[1]
    head_dim = gdn_norm_g.shape[1]
    assert qk_width == n_heads * head_dim and v_width == n_heads * head_dim
    assert head_dim == LANES and n_heads % GDN_HEADS_PER_STEP == 0 and 2 * n_heads <= GATE_LANES
    assert t % GDN_CHUNK == 0

    col_q, col_k, col_v = 0, qk_width, 2 * qk_width
    col_z = col_v + v_width
    col_gates = col_z + v_width
    rest = col_gates + 2 * n_heads
    col_b = col_gates
    col_c = col_b + sc_width
    col_h = col_c + sc_width
    col_ga = col_h + sc_width
    col_gb = col_ga + d

    h2 = x.reshape(b * t, d)
    for l in range(depth):
        w_t = jnp.swapaxes(w_in[l], 0, 1)
        w_main = _wprep(w_t, col_gates=col_gates, gate_cols=rest - col_gates)
        w_g_hi, w_g_lo = _gprep(w_t, col_gates=col_gates, gate_cols=rest - col_gates)
        gate_params = jnp.zeros((8, GATE_LANES), F32)
        gate_params = gate_params.at[0, n_heads:2 * n_heads].set(A_log[l])
        gate_params = gate_params.at[1, n_heads:2 * n_heads].set(dt_bias[l])

        proj2, gates2 = _inproj(h2, ln_mix_g[l][None, :], w_main, w_g_hi, w_g_lo)
        proj3 = proj2.reshape(b, t, -1)
        o_a = _gdn(proj3, gates2.reshape(b, t, GATE_LANES), conv_qkv_w[l], gate_params,
                   gdn_norm_g[l][None, :], n_heads=n_heads, head_dim=head_dim,
                   col_q=col_q, col_k=col_k, col_v=col_v, col_z=col_z, v_width=v_width)
        o_b = _shortconv(proj3, conv_sc_w[l], col_b=col_b, col_c=col_c, col_h=col_h, width=sc_width)
        merged = _merge(o_a.reshape(b * t, v_width), o_b.reshape(b * t, sc_width),
                        w_proj_a[l].astype(BF16), w_proj_b[l].astype(BF16), proj2,
                        col_ga=col_ga, col_gb=col_gb)
        h_mid, hn = _outproj(merged, w_out[l].astype(BF16), h2, ln_ffn_g[l][None, :])
        last = l == depth - 1
        g_last = ln_final_g[None, :] if last else jnp.ones((1, d), F32)
        h2 = _ffn(hn, w_gate[l].astype(BF16), w_up[l].astype(BF16), w_down[l].astype(BF16),
                  h_mid, g_last, final_norm=last)
    return h2.reshape(b, t, d)
```

```python
import functools

import jax
import jax.numpy as jnp
from jax import lax
from jax.experimental import pallas as pl
from jax.experimental.pallas import tpu as pltpu

EPS = 1e-6
F32 = jnp.float32
BF16 = jnp.bfloat16

LANES = 128
V7X_VMEM_CAP_BYTES = 56 * 1024 * 1024
GDN_CHUNK = 128
GDN_HEADS_PER_STEP = 4
GATE_LANES = LANES


def _vmem_limit(block_bytes, scratch_bytes=0, temp_bytes=0):
    need = 2 * block_bytes + scratch_bytes + temp_bytes + (4 << 20)
    return int(min(V7X_VMEM_CAP_BYTES, max(need, 16 << 20)))


def _pick_tile(n, target, align):
    t = min(n, target)
    t -= t % align
    while t >= align:
        if n % t == 0:
            return t
        t -= align
    return n


def _sigmoid(x):
    return 0.5 * jnp.tanh(0.5 * x) + 0.5


def _silu(x):
    h = 0.5 * x
    return h + h * jnp.tanh(h)


def _dot(a, b):
    return jnp.dot(a, b, preferred_element_type=F32)


def _dot_nt(a, b):
    return lax.dot_general(a, b, (((1,), (1,)), ((), ())), preferred_element_type=F32)


def _dot_tn(a, b):
    return lax.dot_general(a, b, (((0,), (0,)), ((), ())), preferred_element_type=F32)


def _rms(x, g):
    return x * lax.rsqrt(jnp.mean(x * x, axis=-1, keepdims=True) + EPS) * g


def _wprep_kernel(wt_ref, o_ref):
    o_ref[...] = wt_ref[...].T.astype(o_ref.dtype)


def _wprep(w_t, *, col_gates, gate_cols):
    n, d = w_t.shape
    n_out = n - gate_cols
    tn = _pick_tile(n_out, 512, LANES)
    assert col_gates % tn == 0 and gate_cols % 8 == 0
    n_plain = col_gates // tn
    blocks = d * tn * 4 + d * tn * 2
    return pl.pallas_call(
        _wprep_kernel,
        grid=(n_out // tn,),
        in_specs=[pl.BlockSpec((pl.Element(tn), pl.Element(d)),
                               lambda j: (pl.multiple_of(j * tn + jnp.where(j >= n_plain, gate_cols, 0), 8), 0))],
        out_specs=pl.BlockSpec((d, tn), lambda j: (0, j)),
        out_shape=jax.ShapeDtypeStruct((d, n_out), BF16),
        compiler_params=pltpu.CompilerParams(
            dimension_semantics=("parallel",),
            vmem_limit_bytes=_vmem_limit(blocks, 0, 2 * d * tn * 4)),
        name="wprep",
    )(w_t)


def _gprep_kernel(g_ref, hi_ref, lo_ref):
    g = g_ref[...]
    pad = jnp.zeros((GATE_LANES - g.shape[0], g.shape[1]), F32)
    w = jnp.concatenate([g, pad], axis=0).T
    hi = w.astype(BF16)
    hi_ref[...] = hi
    lo_ref[...] = (w - hi.astype(F32)).astype(BF16)


def _gprep(w_t, *, col_gates, gate_cols):
    d = w_t.shape[1]
    out = jax.ShapeDtypeStruct((d, GATE_LANES), BF16)
    return pl.pallas_call(
        _gprep_kernel,
        grid=(1,),
        in_specs=[pl.BlockSpec((pl.Element(gate_cols), pl.Element(d)), lambda i: (col_gates, 0))],
        out_specs=[pl.BlockSpec((d, GATE_LANES), lambda i: (0, 0))] * 2,
        out_shape=[out, out],
        name="gprep",
    )(w_t)


def _inproj_kernel(x_ref, g_ref, w_ref, wg_hi_ref, wg_lo_ref, gp_ref, out_ref, gates_ref, gct_ref,
                   xn_ref, *, n_heads):
    @pl.when(pl.program_id(1) == 0)
    def _():
        y = _rms(x_ref[...], g_ref[...])
        hi = y.astype(BF16)
        xn_ref[...] = hi
        lo = (y - hi.astype(F32)).astype(BF16)
        gates = (_dot(hi, wg_hi_ref[...]) + _dot(hi, wg_lo_ref[...]) + _dot(lo, wg_hi_ref[...]))
        gp = gp_ref[...]
        xa = gates + gp[1:2, :]
        softplus = jnp.maximum(xa, 0.0) + jnp.log(1.0 + jnp.exp(-jnp.abs(xa)))
        gcum = -jnp.exp(gp[0:1, :]) * softplus
        row_in_chunk = lax.broadcasted_iota(jnp.int32, gates.shape, 0) & (GDN_CHUNK - 1)
        shift = 1
        while shift < GDN_CHUNK:
            gcum = gcum + jnp.where(row_in_chunk >= shift, pltpu.roll(gcum, shift, 0), 0.0)
            shift *= 2
        lane = lax.broadcasted_iota(jnp.int32, gates.shape, 1)
        gates_ref[...] = jnp.where(lane < n_heads, _sigmoid(gates), gcum)
        for c in range(gct_ref.shape[0]):
            gct_ref[c] = gcum[c * GDN_CHUNK:(c + 1) * GDN_CHUNK, :].T

    out_ref[...] = _dot(xn_ref[...], w_ref[...])


def _inproj(x2, g, w_main, wg_hi, wg_lo, gate_params, *, n_heads):
    m, d = x2.shape
    n = w_main.shape[1]
    tm = _pick_tile(m, 1024, GDN_CHUNK)
    tn = _pick_tile(n, 1024, LANES)
    n_ch = tm // GDN_CHUNK
    blocks = (tm * d * 4 + d * tn * 2 + tm * tn * 4 + 2 * tm * GATE_LANES * 4
              + 2 * d * GATE_LANES * 2)
    kern = functools.partial(_inproj_kernel, n_heads=n_heads)
    return pl.pallas_call(
        kern,
        grid=(m // tm, n // tn),
        in_specs=[
            pl.BlockSpec((tm, d), lambda i, j: (i, 0)),
            pl.BlockSpec((1, d), lambda i, j: (0, 0)),
            pl.BlockSpec((d, tn), lambda i, j: (0, j)),
            pl.BlockSpec((d, GATE_LANES), lambda i, j: (0, 0)),
            pl.BlockSpec((d, GATE_LANES), lambda i, j: (0, 0)),
            pl.BlockSpec((8, GATE_LANES), lambda i, j: (0, 0)),
        ],
        out_specs=[
            pl.BlockSpec((tm, tn), lambda i, j: (i, j)),
            pl.BlockSpec((tm, GATE_LANES), lambda i, j: (i, 0)),
            pl.BlockSpec((n_ch, GATE_LANES, GDN_CHUNK), lambda i, j: (i, 0, 0)),
        ],
        out_shape=[
            jax.ShapeDtypeStruct((m, n), F32),
            jax.ShapeDtypeStruct((m, GATE_LANES), F32),
            jax.ShapeDtypeStruct((m // GDN_CHUNK, GATE_LANES, GDN_CHUNK), F32),
        ],
        scratch_shapes=[pltpu.VMEM((tm, d), BF16)],
        compiler_params=pltpu.CompilerParams(
            dimension_semantics=("parallel", "arbitrary"),
            vmem_limit_bytes=_vmem_limit(blocks, tm * d * 2, tm * d * 8)),
        name="inproj",
    )(x2, g, w_main, wg_hi, wg_lo, gate_params)


def _inv_unit_lower_minus_eye(a_list, rowi, coli):
    diag16 = (rowi >> 4) == (coli >> 4)
    x = [jnp.where(diag16, a, 0.0) for a in a_list]
    e = [-t for t in x]
    for _ in range(3):
        xb = [t.astype(BF16) for t in x]
        x = [_dot(t, t) for t in xb]
        ex = [_dot(ei.astype(BF16), xi.astype(BF16)) for ei, xi in zip(e, x)]
        e = [ei + xi + exi for ei, xi, exi in zip(e, x, ex)]
    level = 5
    while (1 << (level - 1)) < GDN_CHUNK:
        same_hi = (rowi >> level) == (coli >> level)
        same_lo = (rowi >> (level - 1)) == (coli >> (level - 1))
        off = jnp.logical_and(same_hi, jnp.logical_not(same_lo))
        y = [jnp.where(off, a, 0.0) for a in a_list]
        z = [yi + _dot(yi.astype(BF16), ei.astype(BF16)) for yi, ei in zip(y, e)]
        ez = [_dot(ei.astype(BF16), zi.astype(BF16)) for ei, zi in zip(e, z)]
        e = [ei - zi - ezi for ei, zi, ezi in zip(e, z, ez)]
        level += 1
    return e


def _gdn_kernel(q_ref, k_ref, v_ref, z_ref, gates_ref, gct_ref, cwq_ref, cwk_ref, cwv_ref, ng_ref,
                o_ref, s_ref, halo_ref, *, n_heads, head_dim):
    c_len = GDN_CHUNK
    hb = GDN_HEADS_PER_STEP
    tb = q_ref.shape[1]
    n_chunks = tb // c_len
    hg = pl.program_id(1)

    @pl.when(pl.program_id(2) == 0)
    def _():
        s_ref[...] = jnp.zeros_like(s_ref)
        halo_ref[:, :, 0:8, :] = jnp.zeros((3, hb, 8, head_dim), F32)

    gates = gates_ref[0]
    gate_lane = lax.broadcasted_iota(jnp.int32, (tb, GATE_LANES), 1)

    def conv_silu(x_ref, cw_ref, slot):
        cw = cw_ref[...]
        taps = cw.shape[0]
        out = []
        for j in range(hb):
            lanes = slice(j * head_dim, (j + 1) * head_dim)
            x = x_ref[0, :, lanes]
            halo_ref[slot, j, pl.ds(8, tb), :] = x
            y = x * cw[taps - 1:taps, lanes]
            for s in range(1, taps):
                y = y + halo_ref[slot, j, pl.ds(8 - s, tb), :] * cw[taps - 1 - s:taps - s, lanes]
            halo_ref[slot, j, pl.ds(0, 8), :] = x[tb - 8:tb, :]
            out.append(_silu(y))
        return out

    yq = conv_silu(q_ref, cwq_ref, 0)
    yk = conv_silu(k_ref, cwk_ref, 1)
    yv = conv_silu(v_ref, cwv_ref, 2)

    rowi = lax.broadcasted_iota(jnp.int32, (c_len, c_len), 0)
    coli = lax.broadcasted_iota(jnp.int32, (c_len, c_len), 1)
    incl = rowi >= coli
    strict = rowi > coli

    qh, kh, vh, beta, gcol = [], [], [], [], []
    for j in range(hb):
        head = hg * hb + j
        qj = yq[j]
        kj = yk[j]
        qh.append(qj * (lax.rsqrt(jnp.sum(qj * qj, axis=1, keepdims=True) + EPS) * head_dim ** -0.5))
        kh.append(kj * lax.rsqrt(jnp.sum(kj * kj, axis=1, keepdims=True) + EPS))
        vh.append(yv[j])
        beta.append(jnp.sum(jnp.where(gate_lane == head, gates, 0.0), axis=1, keepdims=True))
        gcol.append(jnp.sum(jnp.where(gate_lane == head + n_heads, gates, 0.0), axis=1, keepdims=True))

    probs = [(j, c) for j in range(hb) for c in range(n_chunks)]
    rows = {c: slice(c * c_len, (c + 1) * c_len) for c in range(n_chunks)}
    g_c = [gcol[j][rows[c]] for j, c in probs]
    g_r = [gct_ref[c, pl.ds(hg * hb + j + n_heads, 1), :] for j, c in probs]
    g_last = [g[c_len - 1:c_len, :] for g in g_c]
    decay = [jnp.where(incl, jnp.exp(jnp.where(incl, gc - gr, 0.0)), 0.0) for gc, gr in zip(g_c, g_r)]
    e_g = [jnp.exp(g) for g in g_c]
    kc = [kh[j][rows[c]] for j, c in probs]
    qc = [qh[j][rows[c]] for j, c in probs]
    bb = [beta[j][rows[c]] for j, c in probs]
    kb = [k * b_ for k, b_ in zip(kc, bb)]
    kbf = [k.astype(BF16) for k in kc]
    kk = [_dot_nt(a.astype(BF16), b_) for a, b_ in zip(kb, kbf)]
    qk = [_dot_nt(a.astype(BF16), b_) for a, b_ in zip(qc, kbf)]
    a_low = [jnp.where(strict, t * d_, 0.0) for t, d_ in zip(kk, decay)]
    attn = [(t * d_).astype(BF16) for t, d_ in zip(qk, decay)]
    e_inv = _inv_unit_lower_minus_eye(a_low, rowi, coli)
    rhs = [jnp.concatenate([vh[j][rows[c]] * b_, kb_ * eg], axis=1)
           for (j, c), b_, kb_, eg in zip(probs, bb, kb, e_g)]
    sol = [r + _dot(e.astype(BF16), r.astype(BF16)) for r, e in zip(rhs, e_inv)]
    u = [t[:, :head_dim] for t in sol]
    wq = [jnp.concatenate([t[:, head_dim:], q * eg], axis=0).astype(BF16)
          for t, q, eg in zip(sol, qc, e_g)]
    k_dec = [(k * jnp.exp(gl - g)).astype(BF16) for k, gl, g in zip(kc, g_last, g_c)]
    e_last = [jnp.exp(gl) for gl in g_last]

    z = z_ref[0]
    ng = ng_ref[...]
    s = [s_ref[j] for j in range(hb)]
    for c in range(n_chunks):
        idx = [j * n_chunks + c for j in range(hb)]
        ws = [_dot(wq[i], s[j].astype(BF16)) for j, i in enumerate(idx)]
        v_bf = [(u[i] - ws[j][:c_len]).astype(BF16) for j, i in enumerate(idx)]
        o = [ws[j][c_len:] + _dot(attn[i], v_bf[j]) for j, i in enumerate(idx)]
        s = [s[j] * e_last[i] + _dot_tn(k_dec[i], v_bf[j]) for j, i in enumerate(idx)]
        for j in range(hb):
            lanes = slice(j * head_dim, (j + 1) * head_dim)
            zc = z[rows[c], lanes]
            o_ref[0, rows[c], lanes] = (_rms(o[j], ng) * _silu(zc)).astype(o_ref.dtype)
    for j in range(hb):
        s_ref[j] = s[j]


def _gdn(proj3, gates3, gct, conv_w, norm_g, *, n_heads, head_dim, col_q, col_k, col_v, col_z,
         v_width):
    b, t, _ = proj3.shape
    hb = GDN_HEADS_PER_STEP
    gw = hb * head_dim
    tb = _pick_tile(t, 4 * GDN_CHUNK, GDN_CHUNK)
    n_ch = tb // GDN_CHUNK
    steps = t // tb
    taps = conv_w.shape[0]
    qb, kb_, vb, zb = col_q // gw, col_k // gw, col_v // gw, col_z // gw
    blocks = 4 * tb * gw * 4 + 2 * tb * GATE_LANES * 4 + 3 * 8 * gw * 4 + tb * gw * 2
    scratch = hb * head_dim * head_dim * 4 + 3 * (tb + 8) * gw * 4
    kern = functools.partial(_gdn_kernel, n_heads=n_heads, head_dim=head_dim)
    return pl.pallas_call(
        kern,
        grid=(b, n_heads // hb, steps),
        in_specs=[
            pl.BlockSpec((1, tb, gw), lambda i, h, s: (i, s, qb + h)),
            pl.BlockSpec((1, tb, gw), lambda i, h, s: (i, s, kb_ + h)),
            pl.BlockSpec((1, tb, gw), lambda i, h, s: (i, s, vb + h)),
            pl.BlockSpec((1, tb, gw), lambda i, h, s: (i, s, zb + h)),
            pl.BlockSpec((1, tb, GATE_LANES), lambda i, h, s: (i, s, 0)),
            pl.BlockSpec((n_ch, GATE_LANES, GDN_CHUNK), lambda i, h, s: (i * steps + s, 0, 0)),
            pl.BlockSpec((taps, gw), lambda i, h, s: (0, qb + h)),
            pl.BlockSpec((taps, gw), lambda i, h, s: (0, kb_ + h)),
            pl.BlockSpec((taps, gw), lambda i, h, s: (0, vb + h)),
            pl.BlockSpec((1, head_dim), lambda i, h, s: (0, 0)),
        ],
        out_specs=pl.BlockSpec((1, tb, gw), lambda i, h, s: (i, s, h)),
        out_shape=jax.ShapeDtypeStruct((b, t, v_width), BF16),
        scratch_shapes=[
            pltpu.VMEM((hb, head_dim, head_dim), F32),
            pltpu.VMEM((3, hb, tb + 8, head_dim), F32),
        ],
        compiler_params=pltpu.CompilerParams(
            dimension_semantics=("parallel", "parallel", "arbitrary"),
            vmem_limit_bytes=_vmem_limit(blocks, scratch, 16 << 20)),
        name="gdn",
    )(proj3, proj3, proj3, proj3, gates3, gct, conv_w, conv_w, conv_w, norm_g)


def _shortconv_kernel(b_ref, c_ref, h_ref, cw_ref, o_ref):
    ch = c_ref[0] * h_ref[0]
    cw = cw_ref[...]
    taps = cw.shape[0]
    row = lax.broadcasted_iota(jnp.int32, ch.shape, 0)
    y = ch * cw[taps - 1:taps, :]
    for s in range(1, taps):
        y = y + jnp.where(row >= s, pltpu.roll(ch, s, 0), 0.0) * cw[taps - 1 - s:taps - s, :]
    o_ref[0] = (b_ref[0] * y).astype(o_ref.dtype)


def _shortconv(proj3, conv_w, *, col_b, col_c, col_h, width):
    b, t, _ = proj3.shape
    wb = _pick_tile(width, 2 * LANES, LANES)
    taps = conv_w.shape[0]
    bb, cb, hb = col_b // wb, col_c // wb, col_h // wb
    blocks = 3 * t * wb * 4 + t * wb * 2 + 8 * wb * 4
    return pl.pallas_call(
        _shortconv_kernel,
        grid=(b, width // wb),
        in_specs=[
            pl.BlockSpec((1, t, wb), lambda i, j: (i, 0, bb + j)),
            pl.BlockSpec((1, t, wb), lambda i, j: (i, 0, cb + j)),
            pl.BlockSpec((1, t, wb), lambda i, j: (i, 0, hb + j)),
            pl.BlockSpec((taps, wb), lambda i, j: (0, j)),
        ],
        out_specs=pl.BlockSpec((1, t, wb), lambda i, j: (i, 0, j)),
        out_shape=jax.ShapeDtypeStruct((b, t, width), BF16),
        compiler_params=pltpu.CompilerParams(
            dimension_semantics=("parallel", "parallel"),
            vmem_limit_bytes=_vmem_limit(blocks, 0, 4 * t * wb * 4)),
        name="shortconv",
    )(proj3, proj3, proj3, conv_w)


def _merge_kernel(a_ref, b_ref, wa_ref, wb_ref, ga_ref, gb_ref, o_ref):
    pa = _dot(a_ref[...], wa_ref[...])
    pb = _dot(b_ref[...], wb_ref[...])
    o_ref[...] = (_sigmoid(ga_ref[...]) * pa + _sigmoid(gb_ref[...]) * pb).astype(o_ref.dtype)


def _merge(o_a, o_b, wa, wb, proj2, *, col_ga, col_gb):
    m, ka = o_a.shape
    kb_ = o_b.shape[1]
    d = wa.shape[1]
    tm = _pick_tile(m, 1024, 16)
    tn = _pick_tile(d, 512, LANES)
    ga0, gb0 = col_ga // tn, col_gb // tn
    blocks = tm * (ka + kb_) * 2 + (ka + kb_) * tn * 2 + 2 * tm * tn * 4 + tm * tn * 2
    return pl.pallas_call(
        _merge_kernel,
        grid=(m // tm, d // tn),
        in_specs=[
            pl.BlockSpec((tm, ka), lambda i, j: (i, 0)),
            pl.BlockSpec((tm, kb_), lambda i, j: (i, 0)),
            pl.BlockSpec((ka, tn), lambda i, j: (0, j)),
            pl.BlockSpec((kb_, tn), lambda i, j: (0, j)),
            pl.BlockSpec((tm, tn), lambda i, j: (i, ga0 + j)),
            pl.BlockSpec((tm, tn), lambda i, j: (i, gb0 + j)),
        ],
        out_specs=pl.BlockSpec((tm, tn), lambda i, j: (i, j)),
        out_shape=jax.ShapeDtypeStruct((m, d), BF16),
        compiler_params=pltpu.CompilerParams(
            dimension_semantics=("parallel", "parallel"),
            vmem_limit_bytes=_vmem_limit(blocks, 0, 4 * tm * tn * 4)),
        name="merge",
    )(o_a, o_b, wa, wb, proj2, proj2)


def _outproj_kernel(m_ref, w_ref, x_ref, g_ref, h_ref, hn_ref):
    h = x_ref[...] + _dot(m_ref[...], w_ref[...])
    h_ref[...] = h
    hn_ref[...] = _rms(h, g_ref[...]).astype(hn_ref.dtype)


def _outproj(merged, w_out, x2, g):
    m, d = x2.shape
    k = merged.shape[1]
    tm = _pick_tile(m, 512, 16)
    blocks = tm * k * 2 + k * d * 2 + 2 * tm * d * 4 + tm * d * 2 + d * 4
    return pl.pallas_call(
        _outproj_kernel,
        grid=(m // tm,),
        in_specs=[
            pl.BlockSpec((tm, k), lambda i: (i, 0)),
            pl.BlockSpec((k, d), lambda i: (0, 0)),
            pl.BlockSpec((tm, d), lambda i: (i, 0)),
            pl.BlockSpec((1, d), lambda i: (0, 0)),
        ],
        out_specs=[
            pl.BlockSpec((tm, d), lambda i: (i, 0)),
            pl.BlockSpec((tm, d), lambda i: (i, 0)),
        ],
        out_shape=[
            jax.ShapeDtypeStruct((m, d), F32),
            jax.ShapeDtypeStruct((m, d), BF16),
        ],
        compiler_params=pltpu.CompilerParams(
            dimension_semantics=("parallel",),
            vmem_limit_bytes=_vmem_limit(blocks, 0, 3 * tm * d * 4)),
        name="outproj",
    )(merged, w_out, x2, g)


def _ffn_kernel(hn_ref, wg_ref, wu_ref, wd_ref, h_ref, g_ref, o_ref, acc_ref, *, final_norm):
    f = pl.program_id(1)

    @pl.when(f == 0)
    def _():
        acc_ref[...] = jnp.zeros_like(acc_ref)

    hn = hn_ref[...]
    ff = _silu(_dot(hn, wg_ref[...])) * _dot(hn, wu_ref[...])
    acc_ref[...] += _dot(ff.astype(BF16), wd_ref[...])

    @pl.when(f == pl.num_programs(1) - 1)
    def _():
        h = h_ref[...] + acc_ref[...]
        o_ref[...] = _rms(h, g_ref[...]) if final_norm else h


def _ffn(hn, wg, wu, wd, h, g, *, final_norm):
    m, d = h.shape
    dff = wg.shape[1]
    tm = _pick_tile(m, 512, 16)
    tf = _pick_tile(dff, 512, LANES)
    blocks = tm * d * 2 + 3 * d * tf * 2 + 2 * tm * d * 4 + d * 4
    kern = functools.partial(_ffn_kernel, final_norm=final_norm)
    return pl.pallas_call(
        kern,
        grid=(m // tm, dff // tf),
        in_specs=[
            pl.BlockSpec((tm, d), lambda i, f: (i, 0)),
            pl.BlockSpec((d, tf), lambda i, f: (0, f)),
            pl.BlockSpec((d, tf), lambda i, f: (0, f)),
            pl.BlockSpec((tf, d), lambda i, f: (f, 0)),
            pl.BlockSpec((tm, d), lambda i, f: (i, 0)),
            pl.BlockSpec((1, d), lambda i, f: (0, 0)),
        ],
        out_specs=pl.BlockSpec((tm, d), lambda i, f: (i, 0)),
        out_shape=jax.ShapeDtypeStruct((m, d), F32),
        scratch_shapes=[pltpu.VMEM((tm, d), F32)],
        compiler_params=pltpu.CompilerParams(
            dimension_semantics=("parallel", "arbitrary"),
            vmem_limit_bytes=_vmem_limit(blocks, tm * d * 4, 4 * tm * tf * 4 + tm * d * 4)),
        name="ffn",
    )(hn, wg, wu, wd, h, g)


def kernel(x, ln_mix_g, w_in, conv_qkv_w, A_log, dt_bias, gdn_norm_g, w_proj_a, conv_sc_w, w_proj_b,
           w_out, ln_ffn_g, w_gate, w_up, w_down, ln_final_g):
    b, t, d = x.shape
    depth = w_in.shape[0]
    n_heads = A_log.shape[1]
    v_width = w_proj_a.shape[1]
    qk_width = (conv_qkv_w.shape[2] - v_width) // 2
    sc_width = w_proj_b.shape[1]
    head_dim = gdn_norm_g.shape[1]
    assert qk_width == n_heads * head_dim and v_width == n_heads * head_dim
    assert head_dim == LANES and n_heads % GDN_HEADS_PER_STEP == 0 and 2 * n_heads <= GATE_LANES
    assert t % GDN_CHUNK == 0

    col_q, col_k, col_v = 0, qk_width, 2 * qk_width
    col_z = col_v + v_width
    col_gates = col_z + v_width
    rest = col_gates + 2 * n_heads
    col_b = col_gates
    col_c = col_b + sc_width
    col_h = col_c + sc_width
    col_ga = col_h + sc_width
    col_gb = col_ga + d

    h2 = x.reshape(b * t, d)
    for l in range(depth):
        w_t = jnp.swapaxes(w_in[l], 0, 1)
        w_main = _wprep(w_t, col_gates=col_gates, gate_cols=rest - col_gates)
        w_g_hi, w_g_lo = _gprep(w_t, col_gates=col_gates, gate_cols=rest - col_gates)
        gate_params = jnp.zeros((8, GATE_LANES), F32)
        gate_params = gate_params.at[0, n_heads:2 * n_heads].set(A_log[l])
        gate_params = gate_params.at[1, n_heads:2 * n_heads].set(dt_bias[l])

        proj2, gates2, gct = _inproj(h2, ln_mix_g[l][None, :], w_main, w_g_hi, w_g_lo, gate_params,
                                     n_heads=n_heads)
        proj3 = proj2.reshape(b, t, -1)
        o_a = _gdn(proj3, gates2.reshape(b, t, GATE_LANES), gct, conv_qkv_w[l],
                   gdn_norm_g[l][None, :], n_heads=n_heads, head_dim=head_dim,
                   col_q=col_q, col_k=col_k, col_v=col_v, col_z=col_z, v_width=v_width)
        o_b = _shortconv(proj3, conv_sc_w[l], col_b=col_b, col_c=col_c, col_h=col_h, width=sc_width)
        merged = _merge(o_a.reshape(b * t, v_width), o_b.reshape(b * t, sc_width),
                        w_proj_a[l].astype(BF16), w_proj_b[l].astype(BF16), proj2,
                        col_ga=col_ga, col_gb=col_gb)
        h_mid, hn = _outproj(merged, w_out[l].astype(BF16), h2, ln_ffn_g[l][None, :])
        last = l == depth - 1
        g_last = ln_final_g[None, :] if last else jnp.ones((1, d), F32)
        h2 = _ffn(hn, w_gate[l].astype(BF16), w_up[l].astype(BF16), w_down[l].astype(BF16),
                  h_mid, g_last, final_norm=last)
    return h2.reshape(b, t, d)
```

```python
import functools
import math

import jax
import jax.numpy as jnp
from jax import lax
from jax.experimental import pallas as pl
from jax.experimental.pallas import tpu as pltpu

EPS = 1e-6
F32 = jnp.float32
BF16 = jnp.bfloat16

LANES = 128
V7X_VMEM_CAP_BYTES = 56 * 1024 * 1024
GDN_CHUNK = 128
GDN_HEADS_PER_STEP = 4
GATE_LANES = LANES


def _vmem_limit(block_bytes, scratch_bytes=0, temp_bytes=0):
    need = 2 * block_bytes + scratch_bytes + temp_bytes + (4 << 20)
    return int(min(V7X_VMEM_CAP_BYTES, max(need, 16 << 20)))


def _pick_tile(n, target, align):
    t = min(n, target)
    t -= t % align
    while t >= align:
        if n % t == 0:
            return t
        t -= align
    return n


def _sigmoid(x):
    return 0.5 * jnp.tanh(0.5 * x) + 0.5


def _silu(x):
    h = 0.5 * x
    return h + h * jnp.tanh(h)


def _dot(a, b):
    return jnp.dot(a, b, preferred_element_type=F32)


def _dot_nt(a, b):
    return lax.dot_general(a, b, (((1,), (1,)), ((), ())), preferred_element_type=F32)


def _dot_tn(a, b):
    return lax.dot_general(a, b, (((0,), (0,)), ((), ())), preferred_element_type=F32)


def _rms(x, g):
    return x * lax.rsqrt(jnp.mean(x * x, axis=-1, keepdims=True) + EPS) * g


def _wprep_kernel(wt_ref, o_ref):
    o_ref[...] = wt_ref[...].T.astype(o_ref.dtype)


def _wprep(w_t, *, col_gates, gate_cols):
    n, d = w_t.shape
    n_out = n - gate_cols
    tn = _pick_tile(math.gcd(n_out, col_gates), 1024, LANES)
    assert n_out % tn == 0 and col_gates % tn == 0 and gate_cols % 8 == 0
    n_plain = col_gates // tn
    blocks = d * tn * 4 + d * tn * 2
    return pl.pallas_call(
        _wprep_kernel,
        grid=(n_out // tn,),
        in_specs=[pl.BlockSpec((pl.Element(tn), pl.Element(d)),
                               lambda j: (pl.multiple_of(j * tn + jnp.where(j >= n_plain, gate_cols, 0), 8), 0))],
        out_specs=pl.BlockSpec((d, tn), lambda j: (0, j)),
        out_shape=jax.ShapeDtypeStruct((d, n_out), BF16),
        compiler_params=pltpu.CompilerParams(
            dimension_semantics=("parallel",),
            vmem_limit_bytes=_vmem_limit(blocks, 0, 2 * d * tn * 4)),
        name="wprep",
    )(w_t)


def _gprep_kernel(g_ref, hi_ref, lo_ref):
    g = g_ref[...]
    pad = jnp.zeros((GATE_LANES - g.shape[0], g.shape[1]), F32)
    w = jnp.concatenate([g, pad], axis=0).T
    hi = w.astype(BF16)
    hi_ref[...] = hi
    lo_ref[...] = (w - hi.astype(F32)).astype(BF16)


def _gprep(w_t, *, col_gates, gate_cols):
    d = w_t.shape[1]
    out = jax.ShapeDtypeStruct((d, GATE_LANES), BF16)
    return pl.pallas_call(
        _gprep_kernel,
        grid=(1,),
        in_specs=[pl.BlockSpec((pl.Element(gate_cols), pl.Element(d)), lambda i: (col_gates, 0))],
        out_specs=[pl.BlockSpec((d, GATE_LANES), lambda i: (0, 0))] * 2,
        out_shape=[out, out],
        name="gprep",
    )(w_t)


def _inproj_kernel(x_ref, g_ref, w_ref, wg_hi_ref, wg_lo_ref, gp_ref, out_ref, gates_ref, gct_ref,
                   xn_ref, *, n_heads):
    @pl.when(pl.program_id(1) == 0)
    def _():
        y = _rms(x_ref[...], g_ref[...])
        hi = y.astype(BF16)
        xn_ref[...] = hi
        lo = (y - hi.astype(F32)).astype(BF16)
        gates = (_dot(hi, wg_hi_ref[...]) + _dot(hi, wg_lo_ref[...]) + _dot(lo, wg_hi_ref[...]))
        gp = gp_ref[...]
        xa = gates + gp[1:2, :]
        softplus = jnp.maximum(xa, 0.0) + jnp.log(1.0 + jnp.exp(-jnp.abs(xa)))
        gcum = -jnp.exp(gp[0:1, :]) * softplus
        row_in_chunk = lax.broadcasted_iota(jnp.int32, gates.shape, 0) & (GDN_CHUNK - 1)
        shift = 1
        while shift < GDN_CHUNK:
            gcum = gcum + jnp.where(row_in_chunk >= shift, pltpu.roll(gcum, shift, 0), 0.0)
            shift *= 2
        lane = lax.broadcasted_iota(jnp.int32, gates.shape, 1)
        gates_ref[...] = jnp.where(lane < n_heads, _sigmoid(gates), gcum)
        for c in range(gct_ref.shape[0]):
            gct_ref[c] = gcum[c * GDN_CHUNK:(c + 1) * GDN_CHUNK, :].T

    out_ref[...] = _dot(xn_ref[...], w_ref[...])


def _inproj(x2, g, w_main, wg_hi, wg_lo, gate_params, *, n_heads):
    m, d = x2.shape
    n = w_main.shape[1]
    tm = _pick_tile(m, 1024, GDN_CHUNK)
    tn = _pick_tile(n, 1024, LANES)
    n_ch = tm // GDN_CHUNK
    blocks = (tm * d * 4 + d * tn * 2 + tm * tn * 4 + 2 * tm * GATE_LANES * 4
              + 2 * d * GATE_LANES * 2)
    kern = functools.partial(_inproj_kernel, n_heads=n_heads)
    return pl.pallas_call(
        kern,
        grid=(m // tm, n // tn),
        in_specs=[
            pl.BlockSpec((tm, d), lambda i, j: (i, 0)),
            pl.BlockSpec((1, d), lambda i, j: (0, 0)),
            pl.BlockSpec((d, tn), lambda i, j: (0, j)),
            pl.BlockSpec((d, GATE_LANES), lambda i, j: (0, 0)),
            pl.BlockSpec((d, GATE_LANES), lambda i, j: (0, 0)),
            pl.BlockSpec((8, GATE_LANES), lambda i, j: (0, 0)),
        ],
        out_specs=[
            pl.BlockSpec((tm, tn), lambda i, j: (i, j)),
            pl.BlockSpec((tm, GATE_LANES), lambda i, j: (i, 0)),
            pl.BlockSpec((n_ch, GATE_LANES, GDN_CHUNK), lambda i, j: (i, 0, 0)),
        ],
        out_shape=[
            jax.ShapeDtypeStruct((m, n), F32),
            jax.ShapeDtypeStruct((m, GATE_LANES), F32),
            jax.ShapeDtypeStruct((m // GDN_CHUNK, GATE_LANES, GDN_CHUNK), F32),
        ],
        scratch_shapes=[pltpu.VMEM((tm, d), BF16)],
        compiler_params=pltpu.CompilerParams(
            dimension_semantics=("parallel", "arbitrary"),
            vmem_limit_bytes=_vmem_limit(blocks, tm * d * 2, tm * d * 8)),
        name="inproj",
    )(x2, g, w_main, wg_hi, wg_lo, gate_params)


def _inv_unit_lower_minus_eye(a_list, rowi, coli):
    diag16 = (rowi >> 4) == (coli >> 4)
    x = [jnp.where(diag16, a, 0.0) for a in a_list]
    e = [-t for t in x]
    for _ in range(3):
        xb = [t.astype(BF16) for t in x]
        x = [_dot(t, t) for t in xb]
        ex = [_dot(ei.astype(BF16), xi.astype(BF16)) for ei, xi in zip(e, x)]
        e = [ei + xi + exi for ei, xi, exi in zip(e, x, ex)]
    level = 5
    while (1 << (level - 1)) < GDN_CHUNK:
        same_hi = (rowi >> level) == (coli >> level)
        same_lo = (rowi >> (level - 1)) == (coli >> (level - 1))
        off = jnp.logical_and(same_hi, jnp.logical_not(same_lo))
        y = [jnp.where(off, a, 0.0) for a in a_list]
        z = [yi + _dot(yi.astype(BF16), ei.astype(BF16)) for yi, ei in zip(y, e)]
        ez = [_dot(ei.astype(BF16), zi.astype(BF16)) for ei, zi in zip(e, z)]
        e = [ei - zi - ezi for ei, zi, ezi in zip(e, z, ez)]
        level += 1
    return e


def _gdn_kernel(q_ref, k_ref, v_ref, z_ref, gates_ref, gct_ref, cwq_ref, cwk_ref, cwv_ref, ng_ref,
                scb_ref, scc_ref, sch_ref, cws_ref, o_ref, ob_ref, s_ref, halo_ref, *, n_heads, head_dim):
    c_len = GDN_CHUNK
    hb = GDN_HEADS_PER_STEP
    tb = q_ref.shape[1]
    n_chunks = tb // c_len
    hg = pl.program_id(1)

    @pl.when(pl.program_id(2) == 0)
    def _():
        s_ref[...] = jnp.zeros_like(s_ref)
        halo_ref[:, :, 0:8, :] = jnp.zeros((halo_ref.shape[0], hb, 8, head_dim), F32)

    gates = gates_ref[0]
    gate_lane = lax.broadcasted_iota(jnp.int32, (tb, GATE_LANES), 1)

    def causal_conv(load_x, cw_ref, slot):
        cw = cw_ref[...]
        taps = cw.shape[0]
        out = []
        for j in range(hb):
            lanes = slice(j * head_dim, (j + 1) * head_dim)
            x = load_x(lanes)
            halo_ref[slot, j, pl.ds(8, tb), :] = x
            y = x * cw[taps - 1:taps, lanes]
            for s in range(1, taps):
                y = y + halo_ref[slot, j, pl.ds(8 - s, tb), :] * cw[taps - 1 - s:taps - s, lanes]
            halo_ref[slot, j, pl.ds(0, 8), :] = x[tb - 8:tb, :]
            out.append(y)
        return out

    yq = [_silu(y) for y in causal_conv(lambda ln: q_ref[0, :, ln], cwq_ref, 0)]
    yk = [_silu(y) for y in causal_conv(lambda ln: k_ref[0, :, ln], cwk_ref, 1)]
    yv = [_silu(y) for y in causal_conv(lambda ln: v_ref[0, :, ln], cwv_ref, 2)]


    rowi = lax.broadcasted_iota(jnp.int32, (c_len, c_len), 0)
    coli = lax.broadcasted_iota(jnp.int32, (c_len, c_len), 1)
    incl = rowi >= coli
    strict = rowi > coli

    qh, kh, vh, beta, gcol = [], [], [], [], []
    for j in range(hb):
        head = hg * hb + j
        qj = yq[j]
        kj = yk[j]
        qh.append(qj * (lax.rsqrt(jnp.sum(qj * qj, axis=1, keepdims=True) + EPS) * head_dim ** -0.5))
        kh.append(kj * lax.rsqrt(jnp.sum(kj * kj, axis=1, keepdims=True) + EPS))
        vh.append(yv[j])
        beta.append(jnp.sum(jnp.where(gate_lane == head, gates, 0.0), axis=1, keepdims=True))
        gcol.append(jnp.sum(jnp.where(gate_lane == head + n_heads, gates, 0.0), axis=1, keepdims=True))

    probs = [(j, c) for j in range(hb) for c in range(n_chunks)]
    rows = {c: slice(c * c_len, (c + 1) * c_len) for c in range(n_chunks)}
    g_c = [gcol[j][rows[c]] for j, c in probs]
    g_r = [gct_ref[c, pl.ds(hg * hb + j + n_heads, 1), :] for j, c in probs]
    g_last = [g[c_len - 1:c_len, :] for g in g_c]
    decay = [jnp.where(incl, jnp.exp(jnp.where(incl, gc - gr, 0.0)), 0.0) for gc, gr in zip(g_c, g_r)]
    e_g = [jnp.exp(g) for g in g_c]
    kc = [kh[j][rows[c]] for j, c in probs]
    qc = [qh[j][rows[c]] for j, c in probs]
    bb = [beta[j][rows[c]] for j, c in probs]
    kb = [k * b_ for k, b_ in zip(kc, bb)]
    kbf = [k.astype(BF16) for k in kc]
    kk = [_dot_nt(a.astype(BF16), b_) for a, b_ in zip(kb, kbf)]
    qk = [_dot_nt(a.astype(BF16), b_) for a, b_ in zip(qc, kbf)]
    a_low = [jnp.where(strict, t * d_, 0.0) for t, d_ in zip(kk, decay)]
    attn = [(t * d_).astype(BF16) for t, d_ in zip(qk, decay)]
    e_inv = _inv_unit_lower_minus_eye(a_low, rowi, coli)

    ysc = causal_conv(lambda ln: scc_ref[0, :, ln] * sch_ref[0, :, ln], cws_ref, 3)
    for j in range(hb):
        lanes = slice(j * head_dim, (j + 1) * head_dim)
        ob_ref[0, :, lanes] = (scb_ref[0, :, lanes] * ysc[j]).astype(ob_ref.dtype)

    rhs = [jnp.concatenate([vh[j][rows[c]] * b_, kb_ * eg], axis=1)
           for (j, c), b_, kb_, eg in zip(probs, bb, kb, e_g)]
    sol = [r + _dot(e.astype(BF16), r.astype(BF16)) for r, e in zip(rhs, e_inv)]
    u = [t[:, :head_dim] for t in sol]
    wq = [jnp.concatenate([t[:, head_dim:], q * eg], axis=0).astype(BF16)
          for t, q, eg in zip(sol, qc, e_g)]
    k_dec = [(k * jnp.exp(gl - g)).astype(BF16) for k, gl, g in zip(kc, g_last, g_c)]
    e_last = [jnp.exp(gl) for gl in g_last]

    z = z_ref[0]
    ng = ng_ref[...]
    s = [s_ref[j] for j in range(hb)]
    for c in range(n_chunks):
        idx = [j * n_chunks + c for j in range(hb)]
        ws = [_dot(wq[i], s[j].astype(BF16)) for j, i in enumerate(idx)]
        v_bf = [(u[i] - ws[j][:c_len]).astype(BF16) for j, i in enumerate(idx)]
        o = [ws[j][c_len:] + _dot(attn[i], v_bf[j]) for j, i in enumerate(idx)]
        s = [s[j] * e_last[i] + _dot_tn(k_dec[i], v_bf[j]) for j, i in enumerate(idx)]
        for j in range(hb):
            lanes = slice(j * head_dim, (j + 1) * head_dim)
            zc = z[rows[c], lanes]
            o_ref[0, rows[c], lanes] = (_rms(o[j], ng) * _silu(zc)).astype(o_ref.dtype)
    for j in range(hb):
        s_ref[j] = s[j]


def _gdn(proj3, gates3, gct, conv_w, norm_g, conv_sc_w, *, n_heads, head_dim, col_q, col_k, col_v,
         col_z, col_b, col_c, col_h, v_width):
    b, t, _ = proj3.shape
    hb = GDN_HEADS_PER_STEP
    gw = hb * head_dim
    tb = _pick_tile(t, 4 * GDN_CHUNK, GDN_CHUNK)
    n_ch = tb // GDN_CHUNK
    steps = t // tb
    taps = conv_w.shape[0]
    taps_sc = conv_sc_w.shape[0]
    qb, kb_, vb, zb = col_q // gw, col_k // gw, col_v // gw, col_z // gw
    sb, sc, sh = col_b // gw, col_c // gw, col_h // gw
    blocks = 7 * tb * gw * 4 + 2 * tb * GATE_LANES * 4 + 4 * 8 * gw * 4 + 2 * tb * gw * 2
    scratch = hb * head_dim * head_dim * 4 + 4 * (tb + 8) * gw * 4
    kern = functools.partial(_gdn_kernel, n_heads=n_heads, head_dim=head_dim)
    out = jax.ShapeDtypeStruct((b, t, v_width), BF16)
    return pl.pallas_call(
        kern,
        grid=(b, n_heads // hb, steps),
        in_specs=[
            pl.BlockSpec((1, tb, gw), lambda i, h, s: (i, s, qb + h)),
            pl.BlockSpec((1, tb, gw), lambda i, h, s: (i, s, kb_ + h)),
            pl.BlockSpec((1, tb, gw), lambda i, h, s: (i, s, vb + h)),
            pl.BlockSpec((1, tb, gw), lambda i, h, s: (i, s, zb + h)),
            pl.BlockSpec((1, tb, GATE_LANES), lambda i, h, s: (i, s, 0)),
            pl.BlockSpec((n_ch, GATE_LANES, GDN_CHUNK), lambda i, h, s: (i * steps + s, 0, 0)),
            pl.BlockSpec((taps, gw), lambda i, h, s: (0, qb + h)),
            pl.BlockSpec((taps, gw), lambda i, h, s: (0, kb_ + h)),
            pl.BlockSpec((taps, gw), lambda i, h, s: (0, vb + h)),
            pl.BlockSpec((1, head_dim), lambda i, h, s: (0, 0)),
            pl.BlockSpec((1, tb, gw), lambda i, h, s: (i, s, sb + h)),
            pl.BlockSpec((1, tb, gw), lambda i, h, s: (i, s, sc + h)),
            pl.BlockSpec((1, tb, gw), lambda i, h, s: (i, s, sh + h)),
            pl.BlockSpec((taps_sc, gw), lambda i, h, s: (0, h)),
        ],
        out_specs=[pl.BlockSpec((1, tb, gw), lambda i, h, s: (i, s, h))] * 2,
        out_shape=[out, out],
        scratch_shapes=[
            pltpu.VMEM((hb, head_dim, head_dim), F32),
            pltpu.VMEM((4, hb, tb + 8, head_dim), F32),
        ],
        compiler_params=pltpu.CompilerParams(
            dimension_semantics=("parallel", "parallel", "arbitrary"),
            vmem_limit_bytes=_vmem_limit(blocks, scratch, 16 << 20)),
        name="gdn",
    )(proj3, proj3, proj3, proj3, gates3, gct, conv_w, conv_w, conv_w, norm_g,
      proj3, proj3, proj3, conv_sc_w)


def _merge_kernel(a_ref, b_ref, wa_ref, wb_ref, ga_ref, gb_ref, o_ref):
    pa = _dot(a_ref[...], wa_ref[...])
    pb = _dot(b_ref[...], wb_ref[...])
    o_ref[...] = (_sigmoid(ga_ref[...]) * pa + _sigmoid(gb_ref[...]) * pb).astype(o_ref.dtype)


def _merge(o_a, o_b, wa, wb, proj2, *, col_ga, col_gb):
    m, ka = o_a.shape
    kb_ = o_b.shape[1]
    d = wa.shape[1]
    tm = _pick_tile(m, 1024, 16)
    tn = _pick_tile(d, 512, LANES)
    ga0, gb0 = col_ga // tn, col_gb // tn
    blocks = tm * (ka + kb_) * 2 + (ka + kb_) * tn * 2 + 2 * tm * tn * 4 + tm * tn * 2
    return pl.pallas_call(
        _merge_kernel,
        grid=(m // tm, d // tn),
        in_specs=[
            pl.BlockSpec((tm, ka), lambda i, j: (i, 0)),
            pl.BlockSpec((tm, kb_), lambda i, j: (i, 0)),
            pl.BlockSpec((ka, tn), lambda i, j: (0, j)),
            pl.BlockSpec((kb_, tn), lambda i, j: (0, j)),
            pl.BlockSpec((tm, tn), lambda i, j: (i, ga0 + j)),
            pl.BlockSpec((tm, tn), lambda i, j: (i, gb0 + j)),
        ],
        out_specs=pl.BlockSpec((tm, tn), lambda i, j: (i, j)),
        out_shape=jax.ShapeDtypeStruct((m, d), BF16),
        compiler_params=pltpu.CompilerParams(
            dimension_semantics=("parallel", "parallel"),
            vmem_limit_bytes=_vmem_limit(blocks, 0, 4 * tm * tn * 4)),
        name="merge",
    )(o_a, o_b, wa, wb, proj2, proj2)


def _outproj_kernel(m_ref, w_ref, x_ref, g_ref, h_ref, hn_ref):
    h = x_ref[...] + _dot(m_ref[...], w_ref[...])
    h_ref[...] = h
    hn_ref[...] = _rms(h, g_ref[...]).astype(hn_ref.dtype)


def _outproj(merged, w_out, x2, g):
    m, d = x2.shape
    k = merged.shape[1]
    tm = _pick_tile(m, 512, 16)
    blocks = tm * k * 2 + k * d * 2 + 2 * tm * d * 4 + tm * d * 2 + d * 4
    return pl.pallas_call(
        _outproj_kernel,
        grid=(m // tm,),
        in_specs=[
            pl.BlockSpec((tm, k), lambda i: (i, 0)),
            pl.BlockSpec((k, d), lambda i: (0, 0)),
            pl.BlockSpec((tm, d), lambda i: (i, 0)),
            pl.BlockSpec((1, d), lambda i: (0, 0)),
        ],
        out_specs=[
            pl.BlockSpec((tm, d), lambda i: (i, 0)),
            pl.BlockSpec((tm, d), lambda i: (i, 0)),
        ],
        out_shape=[
            jax.ShapeDtypeStruct((m, d), F32),
            jax.ShapeDtypeStruct((m, d), BF16),
        ],
        compiler_params=pltpu.CompilerParams(
            dimension_semantics=("parallel",),
            vmem_limit_bytes=_vmem_limit(blocks, 0, 3 * tm * d * 4)),
        name="outproj",
    )(merged, w_out, x2, g)


def _ffn_kernel(hn_ref, wg_ref, wu_ref, wd_ref, h_hbm, g_ref, o_ref, sem, *, final_norm):
    i = pl.program_id(0)
    f = pl.program_id(1)
    tm = o_ref.shape[0]

    def residual_copy():
        return pltpu.make_async_copy(h_hbm.at[pl.ds(pl.multiple_of(i * tm, tm), tm), :], o_ref, sem)

    @pl.when(f == 0)
    def _():
        residual_copy().start()

    hn = hn_ref[...]
    ff = (_silu(_dot(hn, wg_ref[...])) * _dot(hn, wu_ref[...])).astype(BF16)

    @pl.when(f == 0)
    def _():
        residual_copy().wait()

    o_ref[...] += _dot(ff, wd_ref[...])

    if final_norm:
        @pl.when(f == pl.num_programs(1) - 1)
        def _():
            o_ref[...] = _rms(o_ref[...], g_ref[...])


def _ffn(hn, wg, wu, wd, h, g, *, final_norm):
    m, d = h.shape
    dff = wg.shape[1]
    tm = _pick_tile(m, 1024, 16)
    tf = _pick_tile(dff, 512, LANES)
    blocks = tm * d * 2 + 3 * d * tf * 2 + tm * d * 4 + d * 4
    kern = functools.partial(_ffn_kernel, final_norm=final_norm)
    return pl.pallas_call(
        kern,
        grid=(m // tm, dff // tf),
        in_specs=[
            pl.BlockSpec((tm, d), lambda i, f: (i, 0)),
            pl.BlockSpec((d, tf), lambda i, f: (0, f)),
            pl.BlockSpec((d, tf), lambda i, f: (0, f)),
            pl.BlockSpec((tf, d), lambda i, f: (f, 0)),
            pl.BlockSpec(memory_space=pl.ANY),
            pl.BlockSpec((1, d), lambda i, f: (0, 0)),
        ],
        out_specs=pl.BlockSpec((tm, d), lambda i, f: (i, 0)),
        out_shape=jax.ShapeDtypeStruct((m, d), F32),
        scratch_shapes=[pltpu.SemaphoreType.DMA(())],
        compiler_params=pltpu.CompilerParams(
            dimension_semantics=("parallel", "arbitrary"),
            vmem_limit_bytes=_vmem_limit(blocks, 0, 4 * tm * tf * 4 + tm * d * 4)),
        name="ffn",
    )(hn, wg, wu, wd, h, g)


def kernel(x, ln_mix_g, w_in, conv_qkv_w, A_log, dt_bias, gdn_norm_g, w_proj_a, conv_sc_w, w_proj_b,
           w_out, ln_ffn_g, w_gate, w_up, w_down, ln_final_g):
    b, t, d = x.shape
    depth = w_in.shape[0]
    n_heads = A_log.shape[1]
    v_width = w_proj_a.shape[1]
    qk_width = (conv_qkv_w.shape[2] - v_width) // 2
    sc_width = w_proj_b.shape[1]
    head_dim = gdn_norm_g.shape[1]
    assert qk_width == n_heads * head_dim and v_width == n_heads * head_dim
    assert head_dim == LANES and n_heads % GDN_HEADS_PER_STEP == 0 and 2 * n_heads <= GATE_LANES
    assert t % GDN_CHUNK == 0 and sc_width == v_width

    col_q, col_k, col_v = 0, qk_width, 2 * qk_width
    col_z = col_v + v_width
    col_gates = col_z + v_width
    rest = col_gates + 2 * n_heads
    col_b = col_gates
    col_c = col_b + sc_width
    col_h = col_c + sc_width
    col_ga = col_h + sc_width
    col_gb = col_ga + d

    h2 = x.reshape(b * t, d)
    for l in range(depth):
        w_t = jnp.swapaxes(w_in[l], 0, 1)
        w_main = _wprep(w_t, col_gates=col_gates, gate_cols=rest - col_gates)
        w_g_hi, w_g_lo = _gprep(w_t, col_gates=col_gates, gate_cols=rest - col_gates)
        gate_params = jnp.zeros((8, GATE_LANES), F32)
        gate_params = gate_params.at[0, n_heads:2 * n_heads].set(A_log[l])
        gate_params = gate_params.at[1, n_heads:2 * n_heads].set(dt_bias[l])

        proj2, gates2, gct = _inproj(h2, ln_mix_g[l][None, :], w_main, w_g_hi, w_g_lo, gate_params,
                                     n_heads=n_heads)
        proj3 = proj2.reshape(b, t, -1)
        o_a, o_b = _gdn(proj3, gates2.reshape(b, t, GATE_LANES), gct, conv_qkv_w[l],
                        gdn_norm_g[l][None, :], conv_sc_w[l], n_heads=n_heads, head_dim=head_dim,
                        col_q=col_q, col_k=col_k, col_v=col_v, col_z=col_z,
                        col_b=col_b, col_c=col_c, col_h=col_h, v_width=v_width)
        merged = _merge(o_a.reshape(b * t, v_width), o_b.reshape(b * t, sc_width),
                        w_proj_a[l].astype(BF16), w_proj_b[l].astype(BF16), proj2,
                        col_ga=col_ga, col_gb=col_gb)
        h_mid, hn = _outproj(merged, w_out[l].astype(BF16), h2, ln_ffn_g[l][None, :])
        last = l == depth - 1
        g_last = ln_final_g[None, :] if last else jnp.ones((1, d), F32)
        h2 = _ffn(hn, w_gate[l].astype(BF16), w_up[l].astype(BF16), w_down[l].astype(BF16),
                  h_mid, g_last, final_norm=last)
    return h2.reshape(b, t, d)
```

```python
import functools
import math

import jax
import jax.numpy as jnp
from jax import lax
from jax.experimental import pallas as pl
from jax.experimental.pallas import tpu as pltpu

EPS = 1e-6
F32 = jnp.float32
BF16 = jnp.bfloat16

LANES = 128
V7X_VMEM_CAP_BYTES = 56 * 1024 * 1024
GDN_CHUNK = 128
GDN_HEADS_PER_STEP = 4
GATE_LANES = LANES


def _vmem_limit(block_bytes, scratch_bytes=0, temp_bytes=0):
    need = 2 * block_bytes + scratch_bytes + temp_bytes + (4 << 20)
    return int(min(V7X_VMEM_CAP_BYTES, max(need, 16 << 20)))


def _pick_tile(n, target, align):
    t = min(n, target)
    t -= t % align
    while t >= align:
        if n % t == 0:
            return t
        t -= align
    return n


def _sigmoid(x):
    return 0.5 * jnp.tanh(0.5 * x) + 0.5


def _silu(x):
    h = 0.5 * x
    return h + h * jnp.tanh(h)


def _dot(a, b):
    return jnp.dot(a, b, preferred_element_type=F32)


def _dot_nt(a, b):
    return lax.dot_general(a, b, (((1,), (1,)), ((), ())), preferred_element_type=F32)


def _dot_tn(a, b):
    return lax.dot_general(a, b, (((0,), (0,)), ((), ())), preferred_element_type=F32)


def _rms(x, g):
    return x * lax.rsqrt(jnp.mean(x * x, axis=-1, keepdims=True) + EPS) * g


def _wprep_kernel(wt_ref, o_ref):
    o_ref[...] = wt_ref[...].T.astype(o_ref.dtype)


def _wprep(w_t, *, col_gates, gate_cols):
    n, d = w_t.shape
    n_out = n - gate_cols
    tn = _pick_tile(math.gcd(n_out, col_gates), 1024, LANES)
    assert n_out % tn == 0 and col_gates % tn == 0 and gate_cols % 8 == 0
    n_plain = col_gates // tn
    blocks = d * tn * 4 + d * tn * 2
    return pl.pallas_call(
        _wprep_kernel,
        grid=(n_out // tn,),
        in_specs=[pl.BlockSpec((pl.Element(tn), pl.Element(d)),
                               lambda j: (pl.multiple_of(j * tn + jnp.where(j >= n_plain, gate_cols, 0), 8), 0))],
        out_specs=pl.BlockSpec((d, tn), lambda j: (0, j)),
        out_shape=jax.ShapeDtypeStruct((d, n_out), BF16),
        compiler_params=pltpu.CompilerParams(
            dimension_semantics=("parallel",),
            vmem_limit_bytes=_vmem_limit(blocks, 0, 2 * d * tn * 4)),
        name="wprep",
    )(w_t)


def _gprep_kernel(g_ref, hi_ref, lo_ref):
    g = g_ref[...]
    pad = jnp.zeros((GATE_LANES - g.shape[0], g.shape[1]), F32)
    w = jnp.concatenate([g, pad], axis=0).T
    hi = w.astype(BF16)
    hi_ref[...] = hi
    lo_ref[...] = (w - hi.astype(F32)).astype(BF16)


def _gprep(w_t, *, col_gates, gate_cols):
    d = w_t.shape[1]
    out = jax.ShapeDtypeStruct((d, GATE_LANES), BF16)
    return pl.pallas_call(
        _gprep_kernel,
        grid=(1,),
        in_specs=[pl.BlockSpec((pl.Element(gate_cols), pl.Element(d)), lambda i: (col_gates, 0))],
        out_specs=[pl.BlockSpec((d, GATE_LANES), lambda i: (0, 0))] * 2,
        out_shape=[out, out],
        name="gprep",
    )(w_t)


INPROJ_NORM_ROWS = 2 * GDN_CHUNK


def _inproj_kernel(*refs, n_heads, n_side):
    x_ref, g_ref, w_ref, wg_hi_ref, wg_lo_ref, gp_ref = refs[:6]
    side_in = refs[6:6 + n_side]
    out_ref, gates_ref, gct_ref = refs[6 + n_side:9 + n_side]
    side_out = refs[9 + n_side:9 + 2 * n_side]
    xn_ref = refs[9 + 2 * n_side]
    rows = INPROJ_NORM_ROWS
    ch_per_pass = rows // GDN_CHUNK

    @pl.when(pl.program_id(1) == 0)
    def _():
        gp = gp_ref[...]
        row_in_chunk = lax.broadcasted_iota(jnp.int32, (rows, GATE_LANES), 0) & (GDN_CHUNK - 1)
        lane = lax.broadcasted_iota(jnp.int32, (rows, GATE_LANES), 1)

        def norm_rows(r, carry):
            r0 = pl.multiple_of(r * rows, rows)
            y = _rms(x_ref[pl.ds(r0, rows), :], g_ref[...])
            hi = y.astype(BF16)
            xn_ref[pl.ds(r0, rows), :] = hi
            lo = (y - hi.astype(F32)).astype(BF16)
            gates = (_dot(hi, wg_hi_ref[...]) + _dot(hi, wg_lo_ref[...]) + _dot(lo, wg_hi_ref[...]))
            xa = gates + gp[1:2, :]
            softplus = jnp.maximum(xa, 0.0) + jnp.log(1.0 + jnp.exp(-jnp.abs(xa)))
            gcum = -jnp.exp(gp[0:1, :]) * softplus
            shift = 1
            while shift < GDN_CHUNK:
                gcum = gcum + jnp.where(row_in_chunk >= shift, pltpu.roll(gcum, shift, 0), 0.0)
                shift *= 2
            gates_ref[pl.ds(r0, rows), :] = jnp.where(lane < n_heads, _sigmoid(gates), gcum)
            for c in range(ch_per_pass):
                gct_ref[r * ch_per_pass + c] = gcum[c * GDN_CHUNK:(c + 1) * GDN_CHUNK, :].T
            return carry

        lax.fori_loop(0, x_ref.shape[0] // rows, norm_rows, 0)

    out_ref[...] = _dot(xn_ref[...], w_ref[...])

    for s_in, s_out in zip(side_in, side_out):
        s_out[...] = s_in[...].astype(s_out.dtype)


def _slab_rows(n_rows, steps):
    for r in range(16, n_rows + 1, 16):
        if n_rows % r == 0 and n_rows // r <= steps:
            return r
    return n_rows


def _inproj(x2, g, w_main, wg_hi, wg_lo, gate_params, side, *, n_heads):
    m, d = x2.shape
    n = w_main.shape[1]
    tm = _pick_tile(m, 1024, INPROJ_NORM_ROWS)
    tn = _pick_tile(n, 1024, LANES)
    n_ch = tm // GDN_CHUNK
    nj = n // tn
    steps = (m // tm) * nj
    side_specs_in, side_specs_out, side_shapes, side_bytes = [], [], [], 0
    for w in side:
        r = _slab_rows(w.shape[0], steps)
        last = w.shape[0] // r - 1
        spec = pl.BlockSpec((r, w.shape[1]), lambda i, j, last=last: (jnp.minimum(i * nj + j, last), 0))
        side_specs_in.append(spec)
        side_specs_out.append(spec)
        side_shapes.append(jax.ShapeDtypeStruct(w.shape, BF16))
        side_bytes += r * w.shape[1] * 6
    blocks = (tm * d * 4 + d * tn * 2 + tm * tn * 4 + 2 * tm * GATE_LANES * 4
              + 2 * d * GATE_LANES * 2 + side_bytes)
    kern = functools.partial(_inproj_kernel, n_heads=n_heads, n_side=len(side))
    outs = pl.pallas_call(
        kern,
        grid=(m // tm, nj),
        in_specs=[
            pl.BlockSpec((tm, d), lambda i, j: (i, 0)),
            pl.BlockSpec((1, d), lambda i, j: (0, 0)),
            pl.BlockSpec((d, tn), lambda i, j: (0, j)),
            pl.BlockSpec((d, GATE_LANES), lambda i, j: (0, 0)),
            pl.BlockSpec((d, GATE_LANES), lambda i, j: (0, 0)),
            pl.BlockSpec((8, GATE_LANES), lambda i, j: (0, 0)),
        ] + side_specs_in,
        out_specs=[
            pl.BlockSpec((tm, tn), lambda i, j: (i, j)),
            pl.BlockSpec((tm, GATE_LANES), lambda i, j: (i, 0)),
            pl.BlockSpec((n_ch, GATE_LANES, GDN_CHUNK), lambda i, j: (i, 0, 0)),
        ] + side_specs_out,
        out_shape=[
            jax.ShapeDtypeStruct((m, n), F32),
            jax.ShapeDtypeStruct((m, GATE_LANES), F32),
            jax.ShapeDtypeStruct((m // GDN_CHUNK, GATE_LANES, GDN_CHUNK), F32),
        ] + side_shapes,
        scratch_shapes=[pltpu.VMEM((tm, d), BF16)],
        compiler_params=pltpu.CompilerParams(
            dimension_semantics=("arbitrary", "arbitrary"),
            vmem_limit_bytes=_vmem_limit(blocks, tm * d * 2, 2 * tm * tn * 4)),
        name="inproj",
    )(x2, g, w_main, wg_hi, wg_lo, gate_params, *side)
    return outs[0], outs[1], outs[2], outs[3:]


def _inv_unit_lower_minus_eye(a_list, rowi, coli):
    diag16 = (rowi >> 4) == (coli >> 4)
    x = [jnp.where(diag16, a, 0.0) for a in a_list]
    e = [-t for t in x]
    for _ in range(3):
        xb = [t.astype(BF16) for t in x]
        x = [_dot(t, t) for t in xb]
        ex = [_dot(ei.astype(BF16), xi.astype(BF16)) for ei, xi in zip(e, x)]
        e = [ei + xi + exi for ei, xi, exi in zip(e, x, ex)]
    level = 5
    while (1 << (level - 1)) < GDN_CHUNK:
        same_hi = (rowi >> level) == (coli >> level)
        same_lo = (rowi >> (level - 1)) == (coli >> (level - 1))
        off = jnp.logical_and(same_hi, jnp.logical_not(same_lo))
        y = [jnp.where(off, a, 0.0) for a in a_list]
        z = [yi + _dot(yi.astype(BF16), ei.astype(BF16)) for yi, ei in zip(y, e)]
        ez = [_dot(ei.astype(BF16), zi.astype(BF16)) for ei, zi in zip(e, z)]
        e = [ei - zi - ezi for ei, zi, ezi in zip(e, z, ez)]
        level += 1
    return e


def _gdn_kernel(q_ref, k_ref, v_ref, z_ref, gates_ref, gct_ref, cwq_ref, cwk_ref, cwv_ref, ng_ref,
                scb_ref, scc_ref, sch_ref, cws_ref, o_ref, ob_ref, s_ref, halo_ref, *, n_heads, head_dim):
    c_len = GDN_CHUNK
    hb = GDN_HEADS_PER_STEP
    tb = q_ref.shape[1]
    n_chunks = tb // c_len
    hg = pl.program_id(1)

    @pl.when(pl.program_id(2) == 0)
    def _():
        s_ref[...] = jnp.zeros_like(s_ref)
        halo_ref[:, :, 0:8, :] = jnp.zeros((halo_ref.shape[0], hb, 8, head_dim), F32)

    gates = gates_ref[0]
    gate_lane = lax.broadcasted_iota(jnp.int32, (tb, GATE_LANES), 1)

    def causal_conv(load_x, cw_ref, slot):
        cw = cw_ref[...]
        taps = cw.shape[0]
        out = []
        for j in range(hb):
            lanes = slice(j * head_dim, (j + 1) * head_dim)
            x = load_x(lanes)
            halo_ref[slot, j, pl.ds(8, tb), :] = x
            y = x * cw[taps - 1:taps, lanes]
            for s in range(1, taps):
                y = y + halo_ref[slot, j, pl.ds(8 - s, tb), :] * cw[taps - 1 - s:taps - s, lanes]
            halo_ref[slot, j, pl.ds(0, 8), :] = x[tb - 8:tb, :]
            out.append(y)
        return out

    yq = [_silu(y) for y in causal_conv(lambda ln: q_ref[0, :, ln], cwq_ref, 0)]
    yk = [_silu(y) for y in causal_conv(lambda ln: k_ref[0, :, ln], cwk_ref, 1)]
    yv = [_silu(y) for y in causal_conv(lambda ln: v_ref[0, :, ln], cwv_ref, 2)]


    rowi = lax.broadcasted_iota(jnp.int32, (c_len, c_len), 0)
    coli = lax.broadcasted_iota(jnp.int32, (c_len, c_len), 1)
    incl = rowi >= coli
    strict = rowi > coli

    qh, kh, vh, beta, gcol = [], [], [], [], []
    for j in range(hb):
        head = hg * hb + j
        qj = yq[j]
        kj = yk[j]
        qh.append(qj * (lax.rsqrt(jnp.sum(qj * qj, axis=1, keepdims=True) + EPS) * head_dim ** -0.5))
        kh.append(kj * lax.rsqrt(jnp.sum(kj * kj, axis=1, keepdims=True) + EPS))
        vh.append(yv[j])
        beta.append(jnp.sum(jnp.where(gate_lane == head, gates, 0.0), axis=1, keepdims=True))
        gcol.append(jnp.sum(jnp.where(gate_lane == head + n_heads, gates, 0.0), axis=1, keepdims=True))

    probs = [(j, c) for j in range(hb) for c in range(n_chunks)]
    rows = {c: slice(c * c_len, (c + 1) * c_len) for c in range(n_chunks)}
    g_c = [gcol[j][rows[c]] for j, c in probs]
    g_r = [gct_ref[c, pl.ds(hg * hb + j + n_heads, 1), :] for j, c in probs]
    g_last = [g[c_len - 1:c_len, :] for g in g_c]
    decay = [jnp.where(incl, jnp.exp(jnp.where(incl, gc - gr, 0.0)), 0.0) for gc, gr in zip(g_c, g_r)]
    e_g = [jnp.exp(g) for g in g_c]
    kc = [kh[j][rows[c]] for j, c in probs]
    qc = [qh[j][rows[c]] for j, c in probs]
    bb = [beta[j][rows[c]] for j, c in probs]
    kb = [k * b_ for k, b_ in zip(kc, bb)]
    kbf = [k.astype(BF16) for k in kc]
    kk = [_dot_nt(a.astype(BF16), b_) for a, b_ in zip(kb, kbf)]
    qk = [_dot_nt(a.astype(BF16), b_) for a, b_ in zip(qc, kbf)]
    a_low = [jnp.where(strict, t * d_, 0.0) for t, d_ in zip(kk, decay)]
    attn = [(t * d_).astype(BF16) for t, d_ in zip(qk, decay)]
    e_inv = _inv_unit_lower_minus_eye(a_low, rowi, coli)

    ysc = causal_conv(lambda ln: scc_ref[0, :, ln] * sch_ref[0, :, ln], cws_ref, 3)
    for j in range(hb):
        lanes = slice(j * head_dim, (j + 1) * head_dim)
        ob_ref[0, :, lanes] = (scb_ref[0, :, lanes] * ysc[j]).astype(ob_ref.dtype)

    rhs = [jnp.concatenate([vh[j][rows[c]] * b_, kb_ * eg], axis=1)
           for (j, c), b_, kb_, eg in zip(probs, bb, kb, e_g)]
    sol = [r + _dot(e.astype(BF16), r.astype(BF16)) for r, e in zip(rhs, e_inv)]
    u = [t[:, :head_dim] for t in sol]
    wq = [jnp.concatenate([t[:, head_dim:], q * eg], axis=0).astype(BF16)
          for t, q, eg in zip(sol, qc, e_g)]
    k_dec = [(k * jnp.exp(gl - g)).astype(BF16) for k, gl, g in zip(kc, g_last, g_c)]
    e_last = [jnp.exp(gl) for gl in g_last]

    z = z_ref[0]
    ng = ng_ref[...]
    s = [s_ref[j] for j in range(hb)]
    for c in range(n_chunks):
        idx = [j * n_chunks + c for j in range(hb)]
        ws = [_dot(wq[i], s[j].astype(BF16)) for j, i in enumerate(idx)]
        v_bf = [(u[i] - ws[j][:c_len]).astype(BF16) for j, i in enumerate(idx)]
        o = [ws[j][c_len:] + _dot(attn[i], v_bf[j]) for j, i in enumerate(idx)]
        s = [s[j] * e_last[i] + _dot_tn(k_dec[i], v_bf[j]) for j, i in enumerate(idx)]
        for j in range(hb):
            lanes = slice(j * head_dim, (j + 1) * head_dim)
            zc = z[rows[c], lanes]
            o_ref[0, rows[c], lanes] = (_rms(o[j], ng) * _silu(zc)).astype(o_ref.dtype)
    for j in range(hb):
        s_ref[j] = s[j]


def _gdn(proj3, gates3, gct, conv_w, norm_g, conv_sc_w, *, n_heads, head_dim, col_q, col_k, col_v,
         col_z, col_b, col_c, col_h, v_width):
    b, t, _ = proj3.shape
    hb = GDN_HEADS_PER_STEP
    gw = hb * head_dim
    tb = _pick_tile(t, 4 * GDN_CHUNK, GDN_CHUNK)
    n_ch = tb // GDN_CHUNK
    steps = t // tb
    taps = conv_w.shape[0]
    taps_sc = conv_sc_w.shape[0]
    qb, kb_, vb, zb = col_q // gw, col_k // gw, col_v // gw, col_z // gw
    sb, sc, sh = col_b // gw, col_c // gw, col_h // gw
    blocks = 7 * tb * gw * 4 + 2 * tb * GATE_LANES * 4 + 4 * 8 * gw * 4 + 2 * tb * gw * 2
    scratch = hb * head_dim * head_dim * 4 + 4 * (tb + 8) * gw * 4
    kern = functools.partial(_gdn_kernel, n_heads=n_heads, head_dim=head_dim)
    out = jax.ShapeDtypeStruct((b, t, v_width), BF16)
    return pl.pallas_call(
        kern,
        grid=(b, n_heads // hb, steps),
        in_specs=[
            pl.BlockSpec((1, tb, gw), lambda i, h, s: (i, s, qb + h)),
            pl.BlockSpec((1, tb, gw), lambda i, h, s: (i, s, kb_ + h)),
            pl.BlockSpec((1, tb, gw), lambda i, h, s: (i, s, vb + h)),
            pl.BlockSpec((1, tb, gw), lambda i, h, s: (i, s, zb + h)),
            pl.BlockSpec((1, tb, GATE_LANES), lambda i, h, s: (i, s, 0)),
            pl.BlockSpec((n_ch, GATE_LANES, GDN_CHUNK), lambda i, h, s: (i * steps + s, 0, 0)),
            pl.BlockSpec((taps, gw), lambda i, h, s: (0, qb + h)),
            pl.BlockSpec((taps, gw), lambda i, h, s: (0, kb_ + h)),
            pl.BlockSpec((taps, gw), lambda i, h, s: (0, vb + h)),
            pl.BlockSpec((1, head_dim), lambda i, h, s: (0, 0)),
            pl.BlockSpec((1, tb, gw), lambda i, h, s: (i, s, sb + h)),
            pl.BlockSpec((1, tb, gw), lambda i, h, s: (i, s, sc + h)),
            pl.BlockSpec((1, tb, gw), lambda i, h, s: (i, s, sh + h)),
            pl.BlockSpec((taps_sc, gw), lambda i, h, s: (0, h)),
        ],
        out_specs=[pl.BlockSpec((1, tb, gw), lambda i, h, s: (i, s, h))] * 2,
        out_shape=[out, out],
        scratch_shapes=[
            pltpu.VMEM((hb, head_dim, head_dim), F32),
            pltpu.VMEM((4, hb, tb + 8, head_dim), F32),
        ],
        compiler_params=pltpu.CompilerParams(
            dimension_semantics=("parallel", "parallel", "arbitrary"),
            vmem_limit_bytes=_vmem_limit(blocks, scratch, 16 << 20)),
        name="gdn",
    )(proj3, proj3, proj3, proj3, gates3, gct, conv_w, conv_w, conv_w, norm_g,
      proj3, proj3, proj3, conv_sc_w)


def _merge_kernel(a_ref, b_ref, wa_ref, wb_ref, ga_ref, gb_ref, o_ref):
    pa = _dot(a_ref[...], wa_ref[...])
    pb = _dot(b_ref[...], wb_ref[...])
    o_ref[...] = (_sigmoid(ga_ref[...]) * pa + _sigmoid(gb_ref[...]) * pb).astype(o_ref.dtype)


def _merge(o_a, o_b, wa, wb, proj2, *, col_ga, col_gb):
    m, ka = o_a.shape
    kb_ = o_b.shape[1]
    d = wa.shape[1]
    tm = _pick_tile(m, 1024, 16)
    tn = _pick_tile(d, 512, LANES)
    ga0, gb0 = col_ga // tn, col_gb // tn
    blocks = tm * (ka + kb_) * 2 + (ka + kb_) * tn * 2 + 2 * tm * tn * 4 + tm * tn * 2
    return pl.pallas_call(
        _merge_kernel,
        grid=(m // tm, d // tn),
        in_specs=[
            pl.BlockSpec((tm, ka), lambda i, j: (i, 0)),
            pl.BlockSpec((tm, kb_), lambda i, j: (i, 0)),
            pl.BlockSpec((ka, tn), lambda i, j: (0, j)),
            pl.BlockSpec((kb_, tn), lambda i, j: (0, j)),
            pl.BlockSpec((tm, tn), lambda i, j: (i, ga0 + j)),
            pl.BlockSpec((tm, tn), lambda i, j: (i, gb0 + j)),
        ],
        out_specs=pl.BlockSpec((tm, tn), lambda i, j: (i, j)),
        out_shape=jax.ShapeDtypeStruct((m, d), BF16),
        compiler_params=pltpu.CompilerParams(
            dimension_semantics=("parallel", "parallel"),
            vmem_limit_bytes=_vmem_limit(blocks, 0, 4 * tm * tn * 4)),
        name="merge",
    )(o_a, o_b, wa, wb, proj2, proj2)


def _outproj_kernel(m_ref, w_ref, x_ref, g_ref, h_ref, hn_ref):
    h = x_ref[...] + _dot(m_ref[...], w_ref[...])
    h_ref[...] = h
    hn_ref[...] = _rms(h, g_ref[...]).astype(hn_ref.dtype)


def _outproj(merged, w_out, x2, g):
    m, d = x2.shape
    k = merged.shape[1]
    tm = _pick_tile(m, 512, 16)
    blocks = tm * k * 2 + k * d * 2 + 2 * tm * d * 4 + tm * d * 2 + d * 4
    return pl.pallas_call(
        _outproj_kernel,
        grid=(m // tm,),
        in_specs=[
            pl.BlockSpec((tm, k), lambda i: (i, 0)),
            pl.BlockSpec((k, d), lambda i: (0, 0)),
            pl.BlockSpec((tm, d), lambda i: (i, 0)),
            pl.BlockSpec((1, d), lambda i: (0, 0)),
        ],
        out_specs=[
            pl.BlockSpec((tm, d), lambda i: (i, 0)),
            pl.BlockSpec((tm, d), lambda i: (i, 0)),
        ],
        out_shape=[
            jax.ShapeDtypeStruct((m, d), F32),
            jax.ShapeDtypeStruct((m, d), BF16),
        ],
        compiler_params=pltpu.CompilerParams(
            dimension_semantics=("parallel",),
            vmem_limit_bytes=_vmem_limit(blocks, 0, 3 * tm * d * 4)),
        name="outproj",
    )(merged, w_out, x2, g)


def _ffn_kernel(hn_ref, wg_ref, wu_ref, wd_ref, h_hbm, g_ref, o_ref, sem, *, final_norm):
    i = pl.program_id(0)
    f = pl.program_id(1)
    tm = o_ref.shape[0]

    def residual_copy():
        return pltpu.make_async_copy(h_hbm.at[pl.ds(pl.multiple_of(i * tm, tm), tm), :], o_ref, sem)

    @pl.when(f == 0)
    def _():
        residual_copy().start()

    hn = hn_ref[...]
    ff = (_silu(_dot(hn, wg_ref[...])) * _dot(hn, wu_ref[...])).astype(BF16)

    @pl.when(f == 0)
    def _():
        residual_copy().wait()

    o_ref[...] += _dot(ff, wd_ref[...])

    if final_norm:
        @pl.when(f == pl.num_programs(1) - 1)
        def _():
            o_ref[...] = _rms(o_ref[...], g_ref[...])


def _ffn(hn, wg, wu, wd, h, g, *, final_norm):
    m, d = h.shape
    dff = wg.shape[1]
    tm = _pick_tile(m, 1024, 16)
    tf = _pick_tile(dff, 512, LANES)
    blocks = tm * d * 2 + 3 * d * tf * 2 + tm * d * 4 + d * 4
    kern = functools.partial(_ffn_kernel, final_norm=final_norm)
    return pl.pallas_call(
        kern,
        grid=(m // tm, dff // tf),
        in_specs=[
            pl.BlockSpec((tm, d), lambda i, f: (i, 0)),
            pl.BlockSpec((d, tf), lambda i, f: (0, f)),
            pl.BlockSpec((d, tf), lambda i, f: (0, f)),
            pl.BlockSpec((tf, d), lambda i, f: (f, 0)),
            pl.BlockSpec(memory_space=pl.ANY),
            pl.BlockSpec((1, d), lambda i, f: (0, 0)),
        ],
        out_specs=pl.BlockSpec((tm, d), lambda i, f: (i, 0)),
        out_shape=jax.ShapeDtypeStruct((m, d), F32),
        scratch_shapes=[pltpu.SemaphoreType.DMA(())],
        compiler_params=pltpu.CompilerParams(
            dimension_semantics=("parallel", "arbitrary"),
            vmem_limit_bytes=_vmem_limit(blocks, 0, 4 * tm * tf * 4 + tm * d * 4)),
        name="ffn",
    )(hn, wg, wu, wd, h, g)


def kernel(x, ln_mix_g, w_in, conv_qkv_w, A_log, dt_bias, gdn_norm_g, w_proj_a, conv_sc_w, w_proj_b,
           w_out, ln_ffn_g, w_gate, w_up, w_down, ln_final_g):
    b, t, d = x.shape
    depth = w_in.shape[0]
    n_heads = A_log.shape[1]
    v_width = w_proj_a.shape[1]
    qk_width = (conv_qkv_w.shape[2] - v_width) // 2
    sc_width = w_proj_b.shape[1]
    head_dim = gdn_norm_g.shape[1]
    assert qk_width == n_heads * head_dim and v_width == n_heads * head_dim
    assert head_dim == LANES and n_heads % GDN_HEADS_PER_STEP == 0 and 2 * n_heads <= GATE_LANES
    assert t % GDN_CHUNK == 0 and sc_width == v_width

    col_q, col_k, col_v = 0, qk_width, 2 * qk_width
    col_z = col_v + v_width
    col_gates = col_z + v_width
    rest = col_gates + 2 * n_heads
    col_b = col_gates
    col_c = col_b + sc_width
    col_h = col_c + sc_width
    col_ga = col_h + sc_width
    col_gb = col_ga + d

    h2 = x.reshape(b * t, d)
    for l in range(depth):
        w_t = jnp.swapaxes(w_in[l], 0, 1)
        w_main = _wprep(w_t, col_gates=col_gates, gate_cols=rest - col_gates)
        w_g_hi, w_g_lo = _gprep(w_t, col_gates=col_gates, gate_cols=rest - col_gates)
        gate_params = jnp.zeros((8, GATE_LANES), F32)
        gate_params = gate_params.at[0, n_heads:2 * n_heads].set(A_log[l])
        gate_params = gate_params.at[1, n_heads:2 * n_heads].set(dt_bias[l])

        later = [w_proj_a[l], w_proj_b[l], w_out[l], w_gate[l], w_up[l], w_down[l]]
        proj2, gates2, gct, later = _inproj(h2, ln_mix_g[l][None, :], w_main, w_g_hi, w_g_lo,
                                            gate_params, later, n_heads=n_heads)
        wa_bf, wb_bf, wout_bf, wgate_bf, wup_bf, wdown_bf = later
        proj3 = proj2.reshape(b, t, -1)
        o_a, o_b = _gdn(proj3, gates2.reshape(b, t, GATE_LANES), gct, conv_qkv_w[l],
                        gdn_norm_g[l][None, :], conv_sc_w[l], n_heads=n_heads, head_dim=head_dim,
                        col_q=col_q, col_k=col_k, col_v=col_v, col_z=col_z,
                        col_b=col_b, col_c=col_c, col_h=col_h, v_width=v_width)
        merged = _merge(o_a.reshape(b * t, v_width), o_b.reshape(b * t, sc_width),
                        wa_bf, wb_bf, proj2, col_ga=col_ga, col_gb=col_gb)
        h_mid, hn = _outproj(merged, wout_bf, h2, ln_ffn_g[l][None, :])
        last = l == depth - 1
        g_last = ln_final_g[None, :] if last else jnp.ones((1, d), F32)
        h2 = _ffn(hn, wgate_bf, wup_bf, wdown_bf, h_mid, g_last, final_norm=last)
    return h2.reshape(b, t, d)
```

```python
import functools
import math

import jax
import jax.numpy as jnp
from jax import lax
from jax.experimental import pallas as pl
from jax.experimental.pallas import tpu as pltpu

EPS = 1e-6
F32 = jnp.float32
BF16 = jnp.bfloat16

LANES = 128
MXU_COLS = 256
V7X_VMEM_CAP_BYTES = 56 * 1024 * 1024
GDN_CHUNK = 128
GDN_HEADS_PER_STEP = 4
GATE_LANES = LANES


def _vmem_limit(block_bytes, scratch_bytes=0, temp_bytes=0):
    need = 2 * block_bytes + scratch_bytes + temp_bytes + (4 << 20)
    return int(min(V7X_VMEM_CAP_BYTES, max(need, 16 << 20)))


def _pick_tile(n, target, align):
    t = min(n, target)
    t -= t % align
    while t >= align:
        if n % t == 0:
            return t
        t -= align
    return n


def _sigmoid(x):
    return 0.5 * jnp.tanh(0.5 * x) + 0.5


def _silu(x):
    h = 0.5 * x
    return h + h * jnp.tanh(h)


def _dot(a, b):
    return jnp.dot(a, b, preferred_element_type=F32)


def _dot_nt(a, b):
    return lax.dot_general(a, b, (((1,), (1,)), ((), ())), preferred_element_type=F32)


def _dot_tn(a, b):
    return lax.dot_general(a, b, (((0,), (0,)), ((), ())), preferred_element_type=F32)


def _rms(x, g):
    return x * lax.rsqrt(jnp.mean(x * x, axis=-1, keepdims=True) + EPS) * g


def _wprep_kernel(wt_ref, o_ref):
    o_ref[...] = wt_ref[...].T.astype(o_ref.dtype)


def _wprep(w_t, *, col_gates, gate_cols):
    n, d = w_t.shape
    n_out = n - gate_cols
    tn = _pick_tile(math.gcd(n_out, col_gates), 1024, LANES)
    assert n_out % tn == 0 and col_gates % tn == 0 and gate_cols % 8 == 0
    n_plain = col_gates // tn
    blocks = d * tn * 4 + d * tn * 2
    return pl.pallas_call(
        _wprep_kernel,
        grid=(n_out // tn,),
        in_specs=[pl.BlockSpec((pl.Element(tn), pl.Element(d)),
                               lambda j: (pl.multiple_of(j * tn + jnp.where(j >= n_plain, gate_cols, 0), 8), 0))],
        out_specs=pl.BlockSpec((d, tn), lambda j: (0, j)),
        out_shape=jax.ShapeDtypeStruct((d, n_out), BF16),
        compiler_params=pltpu.CompilerParams(
            dimension_semantics=("parallel",),
            vmem_limit_bytes=_vmem_limit(blocks, 0, 2 * d * tn * 4)),
        name="wprep",
    )(w_t)


def _gprep_kernel(g_ref, hi_ref, lo_ref):
    g = g_ref[...]
    pad = jnp.zeros((GATE_LANES - g.shape[0], g.shape[1]), F32)
    w = jnp.concatenate([g, pad], axis=0).T
    hi = w.astype(BF16)
    hi_ref[...] = hi
    lo_ref[...] = (w - hi.astype(F32)).astype(BF16)


def _gprep(w_t, *, col_gates, gate_cols):
    d = w_t.shape[1]
    out = jax.ShapeDtypeStruct((d, GATE_LANES), BF16)
    return pl.pallas_call(
        _gprep_kernel,
        grid=(1,),
        in_specs=[pl.BlockSpec((pl.Element(gate_cols), pl.Element(d)), lambda i: (col_gates, 0))],
        out_specs=[pl.BlockSpec((d, GATE_LANES), lambda i: (0, 0))] * 2,
        out_shape=[out, out],
        name="gprep",
    )(w_t)


INPROJ_NORM_ROWS = 2 * GDN_CHUNK


def _inproj_kernel(*refs, n_heads, n_side, n_conv, seq_len):
    x_ref, g_ref, w_ref, wg_hi_ref, wg_lo_ref, gp_ref, cw_ref = refs[:7]
    side_in = refs[7:7 + n_side]
    out_ref, gates_ref, gct_ref = refs[7 + n_side:10 + n_side]
    side_out = refs[10 + n_side:10 + 2 * n_side]
    xn_ref, cbuf_ref, halo_ref = refs[10 + 2 * n_side:]
    rows = INPROJ_NORM_ROWS
    ch_per_pass = rows // GDN_CHUNK
    i = pl.program_id(0)
    j = pl.program_id(1)
    tm, tn = out_ref.shape

    @pl.when(jnp.logical_and(i == 0, j == 0))
    def _():
        halo_ref[...] = jnp.zeros_like(halo_ref)

    @pl.when(j == 0)
    def _():
        gp = gp_ref[...]
        row_in_chunk = lax.broadcasted_iota(jnp.int32, (rows, GATE_LANES), 0) & (GDN_CHUNK - 1)
        lane = lax.broadcasted_iota(jnp.int32, (rows, GATE_LANES), 1)

        def norm_rows(r, carry):
            r0 = pl.multiple_of(r * rows, rows)
            y = _rms(x_ref[pl.ds(r0, rows), :], g_ref[...])
            hi = y.astype(BF16)
            xn_ref[pl.ds(r0, rows), :] = hi
            lo = (y - hi.astype(F32)).astype(BF16)
            w_hi = wg_hi_ref[...]
            hi_terms = _dot(hi, jnp.concatenate([w_hi, wg_lo_ref[...]], axis=1))
            gates = hi_terms[:, :GATE_LANES] + hi_terms[:, GATE_LANES:] + _dot(lo, w_hi)
            xa = gates + gp[1:2, :]
            softplus = jnp.maximum(xa, 0.0) + jnp.log(1.0 + jnp.exp(-jnp.abs(xa)))
            gcum = -jnp.exp(gp[0:1, :]) * softplus
            shift = 1
            while shift < GDN_CHUNK:
                gcum = gcum + jnp.where(row_in_chunk >= shift, pltpu.roll(gcum, shift, 0), 0.0)
                shift *= 2
            gates_ref[pl.ds(r0, rows), :] = jnp.where(lane < n_heads, _sigmoid(gates), gcum)
            for c in range(ch_per_pass):
                gct_ref[r * ch_per_pass + c] = gcum[c * GDN_CHUNK:(c + 1) * GDN_CHUNK, :].T
            return carry

        lax.fori_loop(0, x_ref.shape[0] // rows, norm_rows, 0)

    @pl.when(j >= n_conv)
    def _():
        out_ref[...] = _dot(xn_ref[...], w_ref[...])

    @pl.when(j < n_conv)
    def _():
        cw = cw_ref[...]
        taps = cw.shape[0]
        starts_seq = (i * tm) % seq_len == 0
        xn = xn_ref[...]
        for c0 in range(0, tn, MXU_COLS):
            acc = _dot(xn, w_ref[:, c0:c0 + MXU_COLS])
            for lt in range(c0 // LANES, (c0 + MXU_COLS) // LANES):
                lanes = slice(lt * LANES, (lt + 1) * LANES)
                x = acc[:, lt * LANES - c0:(lt + 1) * LANES - c0]
                cbuf_ref[lt, pl.ds(0, 8), :] = jnp.where(starts_seq, 0.0, halo_ref[j, lt])
                cbuf_ref[lt, pl.ds(8, tm), :] = x
                y = x * cw[taps - 1:taps, lanes]
                for s in range(1, taps):
                    y = y + cbuf_ref[lt, pl.ds(8 - s, tm), :] * cw[taps - 1 - s:taps - s, lanes]
                halo_ref[j, lt] = x[tm - 8:tm, :]
                out_ref[:, lanes] = _silu(y)

    for s_in, s_out in zip(side_in, side_out):
        s_out[...] = s_in[...].astype(s_out.dtype)


def _slab_rows(n_rows, steps):
    for r in range(16, n_rows + 1, 16):
        if n_rows % r == 0 and n_rows // r <= steps:
            return r
    return n_rows


def _inproj(x2, g, w_main, wg_hi, wg_lo, gate_params, conv_w, side, *, n_heads, seq_len):
    m, d = x2.shape
    n = w_main.shape[1]
    taps, conv_cols = conv_w.shape
    tm = _pick_tile(seq_len, 1024, INPROJ_NORM_ROWS)
    tn = _pick_tile(math.gcd(n, conv_cols), 1024, LANES)
    assert m % tm == 0 and n % tn == 0 and conv_cols % tn == 0 and taps <= 8
    n_conv = conv_cols // tn
    n_ch = tm // GDN_CHUNK
    nj = n // tn
    steps = (m // tm) * nj
    side_specs_in, side_specs_out, side_shapes, side_bytes = [], [], [], 0
    for w in side:
        r = _slab_rows(w.shape[0], steps)
        last = w.shape[0] // r - 1
        spec = pl.BlockSpec((r, w.shape[1]), lambda i, j, last=last: (jnp.minimum(i * nj + j, last), 0))
        side_specs_in.append(spec)
        side_specs_out.append(spec)
        side_shapes.append(jax.ShapeDtypeStruct(w.shape, BF16))
        side_bytes += r * w.shape[1] * 6
    blocks = (tm * d * 4 + d * tn * 2 + tm * tn * 4 + 2 * tm * GATE_LANES * 4
              + 2 * d * GATE_LANES * 2 + side_bytes)
    kern = functools.partial(_inproj_kernel, n_heads=n_heads, n_side=len(side), n_conv=n_conv,
                             seq_len=seq_len)
    conv_scratch = (tn // LANES) * (tm + 8) * LANES * 4 + n_conv * (tn // LANES) * 8 * LANES * 4
    outs = pl.pallas_call(
        kern,
        grid=(m // tm, nj),
        in_specs=[
            pl.BlockSpec((tm, d), lambda i, j: (i, 0)),
            pl.BlockSpec((1, d), lambda i, j: (0, 0)),
            pl.BlockSpec((d, tn), lambda i, j: (0, j)),
            pl.BlockSpec((d, GATE_LANES), lambda i, j: (0, 0)),
            pl.BlockSpec((d, GATE_LANES), lambda i, j: (0, 0)),
            pl.BlockSpec((8, GATE_LANES), lambda i, j: (0, 0)),
            pl.BlockSpec((taps, tn), lambda i, j: (0, jnp.minimum(j, n_conv - 1))),
        ] + side_specs_in,
        out_specs=[
            pl.BlockSpec((tm, tn), lambda i, j: (i, j)),
            pl.BlockSpec((tm, GATE_LANES), lambda i, j: (i, 0)),
            pl.BlockSpec((n_ch, GATE_LANES, GDN_CHUNK), lambda i, j: (i, 0, 0)),
        ] + side_specs_out,
        out_shape=[
            jax.ShapeDtypeStruct((m, n), F32),
            jax.ShapeDtypeStruct((m, GATE_LANES), F32),
            jax.ShapeDtypeStruct((m // GDN_CHUNK, GATE_LANES, GDN_CHUNK), F32),
        ] + side_shapes,
        scratch_shapes=[
            pltpu.VMEM((tm, d), BF16),
            pltpu.VMEM((tn // LANES, tm + 8, LANES), F32),
            pltpu.VMEM((n_conv, tn // LANES, 8, LANES), F32),
        ],
        compiler_params=pltpu.CompilerParams(
            dimension_semantics=("arbitrary", "arbitrary"),
            vmem_limit_bytes=_vmem_limit(blocks, tm * d * 2 + conv_scratch, 2 * tm * tn * 4)),
        name="inproj",
    )(x2, g, w_main, wg_hi, wg_lo, gate_params, conv_w, *side)
    return outs[0], outs[1], outs[2], outs[3:]


def _inv_unit_lower_minus_eye(a_list, rowi, coli):
    diag16 = (rowi >> 4) == (coli >> 4)
    x = [jnp.where(diag16, a, 0.0) for a in a_list]
    e = [-t for t in x]
    for _ in range(3):
        xb = [t.astype(BF16) for t in x]
        x = [_dot(t, t) for t in xb]
        ex = [_dot(ei.astype(BF16), xi.astype(BF16)) for ei, xi in zip(e, x)]
        e = [ei + xi + exi for ei, xi, exi in zip(e, x, ex)]
    level = 5
    while (1 << (level - 1)) < GDN_CHUNK:
        same_hi = (rowi >> level) == (coli >> level)
        same_lo = (rowi >> (level - 1)) == (coli >> (level - 1))
        off = jnp.logical_and(same_hi, jnp.logical_not(same_lo))
        y = [jnp.where(off, a, 0.0) for a in a_list]
        z = [yi + _dot(yi.astype(BF16), ei.astype(BF16)) for yi, ei in zip(y, e)]
        ez = [_dot(ei.astype(BF16), zi.astype(BF16)) for ei, zi in zip(e, z)]
        e = [ei - zi - ezi for ei, zi, ezi in zip(e, z, ez)]
        level += 1
    return e


def _gdn_kernel(q_ref, k_ref, v_ref, z_ref, gates_ref, gct_ref, ng_ref, scb_ref, scc_ref, sch_ref,
                cws_ref, o_ref, ob_ref, s_ref, halo_ref, *, n_heads, head_dim):
    c_len = GDN_CHUNK
    hb = GDN_HEADS_PER_STEP
    tb = q_ref.shape[1]
    n_chunks = tb // c_len
    hg = pl.program_id(1)

    @pl.when(pl.program_id(2) == 0)
    def _():
        s_ref[...] = jnp.zeros_like(s_ref)
        halo_ref[:, 0:8, :] = jnp.zeros((hb, 8, head_dim), F32)

    gates = gates_ref[0]
    gate_lane = lax.broadcasted_iota(jnp.int32, (tb, GATE_LANES), 1)

    def causal_conv(load_x, cw_ref):
        cw = cw_ref[...]
        taps = cw.shape[0]
        out = []
        for j in range(hb):
            lanes = slice(j * head_dim, (j + 1) * head_dim)
            x = load_x(lanes)
            halo_ref[j, pl.ds(8, tb), :] = x
            y = x * cw[taps - 1:taps, lanes]
            for s in range(1, taps):
                y = y + halo_ref[j, pl.ds(8 - s, tb), :] * cw[taps - 1 - s:taps - s, lanes]
            halo_ref[j, pl.ds(0, 8), :] = x[tb - 8:tb, :]
            out.append(y)
        return out

    head_lanes = [slice(j * head_dim, (j + 1) * head_dim) for j in range(hb)]
    yq = [q_ref[0, :, ln] for ln in head_lanes]
    yk = [k_ref[0, :, ln] for ln in head_lanes]
    yv = [v_ref[0, :, ln] for ln in head_lanes]

    rowi = lax.broadcasted_iota(jnp.int32, (c_len, c_len), 0)
    coli = lax.broadcasted_iota(jnp.int32, (c_len, c_len), 1)
    incl = rowi >= coli
    strict = rowi > coli

    qh, kh, vh, beta, gcol = [], [], [], [], []
    for j in range(hb):
        head = hg * hb + j
        qj = yq[j]
        kj = yk[j]
        qh.append(qj * (lax.rsqrt(jnp.sum(qj * qj, axis=1, keepdims=True) + EPS) * head_dim ** -0.5))
        kh.append(kj * lax.rsqrt(jnp.sum(kj * kj, axis=1, keepdims=True) + EPS))
        vh.append(yv[j])
        beta.append(jnp.sum(jnp.where(gate_lane == head, gates, 0.0), axis=1, keepdims=True))
        gcol.append(jnp.sum(jnp.where(gate_lane == head + n_heads, gates, 0.0), axis=1, keepdims=True))

    probs = [(j, c) for j in range(hb) for c in range(n_chunks)]
    rows = {c: slice(c * c_len, (c + 1) * c_len) for c in range(n_chunks)}
    g_c = [gcol[j][rows[c]] for j, c in probs]
    g_r = [gct_ref[c, pl.ds(hg * hb + j + n_heads, 1), :] for j, c in probs]
    g_last = [g[c_len - 1:c_len, :] for g in g_c]
    decay = [jnp.where(incl, jnp.exp(jnp.where(incl, gc - gr, 0.0)), 0.0) for gc, gr in zip(g_c, g_r)]
    e_g = [jnp.exp(g) for g in g_c]
    kc = [kh[j][rows[c]] for j, c in probs]
    qc = [qh[j][rows[c]] for j, c in probs]
    bb = [beta[j][rows[c]] for j, c in probs]
    kb = [k * b_ for k, b_ in zip(kc, bb)]
    kbf = [k.astype(BF16) for k in kc]
    kk = [_dot_nt(a.astype(BF16), b_) for a, b_ in zip(kb, kbf)]
    qk = [_dot_nt(a.astype(BF16), b_) for a, b_ in zip(qc, kbf)]
    a_low = [jnp.where(strict, t * d_, 0.0) for t, d_ in zip(kk, decay)]
    attn = [(t * d_).astype(BF16) for t, d_ in zip(qk, decay)]
    e_inv = _inv_unit_lower_minus_eye(a_low, rowi, coli)

    ysc = causal_conv(lambda ln: scc_ref[0, :, ln] * sch_ref[0, :, ln], cws_ref)
    for j in range(hb):
        lanes = slice(j * head_dim, (j + 1) * head_dim)
        ob_ref[0, :, lanes] = (scb_ref[0, :, lanes] * ysc[j]).astype(ob_ref.dtype)

    rhs = [jnp.concatenate([vh[j][rows[c]] * b_, kb_ * eg], axis=1)
           for (j, c), b_, kb_, eg in zip(probs, bb, kb, e_g)]
    sol = [r + _dot(e.astype(BF16), r.astype(BF16)) for r, e in zip(rhs, e_inv)]
    u = [t[:, :head_dim] for t in sol]
    wq = [jnp.concatenate([t[:, head_dim:], q * eg], axis=0).astype(BF16)
          for t, q, eg in zip(sol, qc, e_g)]
    k_dec = [(k * jnp.exp(gl - g)).astype(BF16) for k, gl, g in zip(kc, g_last, g_c)]
    e_last = [jnp.exp(gl) for gl in g_last]

    z = z_ref[0]
    ng = ng_ref[...]
    s = [s_ref[j] for j in range(hb)]
    for c in range(n_chunks):
        idx = [j * n_chunks + c for j in range(hb)]
        ws = [_dot(wq[i], s[j].astype(BF16)) for j, i in enumerate(idx)]
        v_bf = [(u[i] - ws[j][:c_len]).astype(BF16) for j, i in enumerate(idx)]
        o = [ws[j][c_len:] + _dot(attn[i], v_bf[j]) for j, i in enumerate(idx)]
        s = [s[j] * e_last[i] + _dot_tn(k_dec[i], v_bf[j]) for j, i in enumerate(idx)]
        for j in range(hb):
            lanes = slice(j * head_dim, (j + 1) * head_dim)
            zc = z[rows[c], lanes]
            o_ref[0, rows[c], lanes] = (_rms(o[j], ng) * _silu(zc)).astype(o_ref.dtype)
    for j in range(hb):
        s_ref[j] = s[j]


def _gdn(proj3, gates3, gct, norm_g, conv_sc_w, *, n_heads, head_dim, col_q, col_k, col_v, col_z,
         col_b, col_c, col_h, v_width):
    b, t, _ = proj3.shape
    hb = GDN_HEADS_PER_STEP
    gw = hb * head_dim
    tb = _pick_tile(t, 4 * GDN_CHUNK, GDN_CHUNK)
    n_ch = tb // GDN_CHUNK
    steps = t // tb
    taps_sc = conv_sc_w.shape[0]
    qb, kb_, vb, zb = col_q // gw, col_k // gw, col_v // gw, col_z // gw
    sb, sc, sh = col_b // gw, col_c // gw, col_h // gw
    blocks = 7 * tb * gw * 4 + 2 * tb * GATE_LANES * 4 + 8 * gw * 4 + 2 * tb * gw * 2
    scratch = hb * head_dim * head_dim * 4 + (tb + 8) * gw * 4
    kern = functools.partial(_gdn_kernel, n_heads=n_heads, head_dim=head_dim)
    out = jax.ShapeDtypeStruct((b, t, v_width), BF16)
    return pl.pallas_call(
        kern,
        grid=(b, n_heads // hb, steps),
        in_specs=[
            pl.BlockSpec((1, tb, gw), lambda i, h, s: (i, s, qb + h)),
            pl.BlockSpec((1, tb, gw), lambda i, h, s: (i, s, kb_ + h)),
            pl.BlockSpec((1, tb, gw), lambda i, h, s: (i, s, vb + h)),
            pl.BlockSpec((1, tb, gw), lambda i, h, s: (i, s, zb + h)),
            pl.BlockSpec((1, tb, GATE_LANES), lambda i, h, s: (i, s, 0)),
            pl.BlockSpec((n_ch, GATE_LANES, GDN_CHUNK), lambda i, h, s: (i * steps + s, 0, 0)),
            pl.BlockSpec((1, head_dim), lambda i, h, s: (0, 0)),
            pl.BlockSpec((1, tb, gw), lambda i, h, s: (i, s, sb + h)),
            pl.BlockSpec((1, tb, gw), lambda i, h, s: (i, s, sc + h)),
            pl.BlockSpec((1, tb, gw), lambda i, h, s: (i, s, sh + h)),
            pl.BlockSpec((taps_sc, gw), lambda i, h, s: (0, h)),
        ],
        out_specs=[pl.BlockSpec((1, tb, gw), lambda i, h, s: (i, s, h))] * 2,
        out_shape=[out, out],
        scratch_shapes=[
            pltpu.VMEM((hb, head_dim, head_dim), F32),
            pltpu.VMEM((hb, tb + 8, head_dim), F32),
        ],
        compiler_params=pltpu.CompilerParams(
            dimension_semantics=("parallel", "parallel", "arbitrary"),
            vmem_limit_bytes=_vmem_limit(blocks, scratch, 16 << 20)),
        name="gdn",
    )(proj3, proj3, proj3, proj3, gates3, gct, norm_g, proj3, proj3, proj3, conv_sc_w)


def _merge_kernel(a_ref, b_ref, wa_ref, wb_ref, ga_ref, gb_ref, o_ref):
    pa = _dot(a_ref[...], wa_ref[...])
    pb = _dot(b_ref[...], wb_ref[...])
    o_ref[...] = (_sigmoid(ga_ref[...]) * pa + _sigmoid(gb_ref[...]) * pb).astype(o_ref.dtype)


def _merge(o_a, o_b, wa, wb, proj2, *, col_ga, col_gb):
    m, ka = o_a.shape
    kb_ = o_b.shape[1]
    d = wa.shape[1]
    tm = _pick_tile(m, 1024, 16)
    tn = _pick_tile(d, 512, LANES)
    ga0, gb0 = col_ga // tn, col_gb // tn
    blocks = tm * (ka + kb_) * 2 + (ka + kb_) * tn * 2 + 2 * tm * tn * 4 + tm * tn * 2
    return pl.pallas_call(
        _merge_kernel,
        grid=(m // tm, d // tn),
        in_specs=[
            pl.BlockSpec((tm, ka), lambda i, j: (i, 0)),
            pl.BlockSpec((tm, kb_), lambda i, j: (i, 0)),
            pl.BlockSpec((ka, tn), lambda i, j: (0, j)),
            pl.BlockSpec((kb_, tn), lambda i, j: (0, j)),
            pl.BlockSpec((tm, tn), lambda i, j: (i, ga0 + j)),
            pl.BlockSpec((tm, tn), lambda i, j: (i, gb0 + j)),
        ],
        out_specs=pl.BlockSpec((tm, tn), lambda i, j: (i, j)),
        out_shape=jax.ShapeDtypeStruct((m, d), BF16),
        compiler_params=pltpu.CompilerParams(
            dimension_semantics=("parallel", "parallel"),
            vmem_limit_bytes=_vmem_limit(blocks, 0, 4 * tm * tn * 4)),
        name="merge",
    )(o_a, o_b, wa, wb, proj2, proj2)


def _outproj_kernel(m_ref, w_ref, x_ref, g_ref, h_ref, hn_ref):
    h = x_ref[...] + _dot(m_ref[...], w_ref[...])
    h_ref[...] = h
    hn_ref[...] = _rms(h, g_ref[...]).astype(hn_ref.dtype)


def _outproj(merged, w_out, x2, g):
    m, d = x2.shape
    k = merged.shape[1]
    tm = _pick_tile(m, 512, 16)
    blocks = tm * k * 2 + k * d * 2 + 2 * tm * d * 4 + tm * d * 2 + d * 4
    return pl.pallas_call(
        _outproj_kernel,
        grid=(m // tm,),
        in_specs=[
            pl.BlockSpec((tm, k), lambda i: (i, 0)),
            pl.BlockSpec((k, d), lambda i: (0, 0)),
            pl.BlockSpec((tm, d), lambda i: (i, 0)),
            pl.BlockSpec((1, d), lambda i: (0, 0)),
        ],
        out_specs=[
            pl.BlockSpec((tm, d), lambda i: (i, 0)),
            pl.BlockSpec((tm, d), lambda i: (i, 0)),
        ],
        out_shape=[
            jax.ShapeDtypeStruct((m, d), F32),
            jax.ShapeDtypeStruct((m, d), BF16),
        ],
        compiler_params=pltpu.CompilerParams(
            dimension_semantics=("parallel",),
            vmem_limit_bytes=_vmem_limit(blocks, 0, 3 * tm * d * 4)),
        name="outproj",
    )(merged, w_out, x2, g)


def _ffn_kernel(hn_ref, wg_ref, wu_ref, wd_ref, h_hbm, g_ref, o_ref, sem, *, final_norm):
    i = pl.program_id(0)
    f = pl.program_id(1)
    tm = o_ref.shape[0]

    def residual_copy():
        return pltpu.make_async_copy(h_hbm.at[pl.ds(pl.multiple_of(i * tm, tm), tm), :], o_ref, sem)

    @pl.when(f == 0)
    def _():
        residual_copy().start()

    hn = hn_ref[...]
    ff = (_silu(_dot(hn, wg_ref[...])) * _dot(hn, wu_ref[...])).astype(BF16)

    @pl.when(f == 0)
    def _():
        residual_copy().wait()

    o_ref[...] += _dot(ff, wd_ref[...])

    if final_norm:
        @pl.when(f == pl.num_programs(1) - 1)
        def _():
            o_ref[...] = _rms(o_ref[...], g_ref[...])


def _ffn(hn, wg, wu, wd, h, g, *, final_norm):
    m, d = h.shape
    dff = wg.shape[1]
    tm = _pick_tile(m, 1024, 16)
    tf = _pick_tile(dff, 512, LANES)
    blocks = tm * d * 2 + 3 * d * tf * 2 + tm * d * 4 + d * 4
    kern = functools.partial(_ffn_kernel, final_norm=final_norm)
    return pl.pallas_call(
        kern,
        grid=(m // tm, dff // tf),
        in_specs=[
            pl.BlockSpec((tm, d), lambda i, f: (i, 0)),
            pl.BlockSpec((d, tf), lambda i, f: (0, f)),
            pl.BlockSpec((d, tf), lambda i, f: (0, f)),
            pl.BlockSpec((tf, d), lambda i, f: (f, 0)),
            pl.BlockSpec(memory_space=pl.ANY),
            pl.BlockSpec((1, d), lambda i, f: (0, 0)),
        ],
        out_specs=pl.BlockSpec((tm, d), lambda i, f: (i, 0)),
        out_shape=jax.ShapeDtypeStruct((m, d), F32),
        scratch_shapes=[pltpu.SemaphoreType.DMA(())],
        compiler_params=pltpu.CompilerParams(
            dimension_semantics=("parallel", "arbitrary"),
            vmem_limit_bytes=_vmem_limit(blocks, 0, 4 * tm * tf * 4 + tm * d * 4)),
        name="ffn",
    )(hn, wg, wu, wd, h, g)


def kernel(x, ln_mix_g, w_in, conv_qkv_w, A_log, dt_bias, gdn_norm_g, w_proj_a, conv_sc_w, w_proj_b,
           w_out, ln_ffn_g, w_gate, w_up, w_down, ln_final_g):
    b, t, d = x.shape
    depth = w_in.shape[0]
    n_heads = A_log.shape[1]
    v_width = w_proj_a.shape[1]
    qk_width = (conv_qkv_w.shape[2] - v_width) // 2
    sc_width = w_proj_b.shape[1]
    head_dim = gdn_norm_g.shape[1]
    assert qk_width == n_heads * head_dim and v_width == n_heads * head_dim
    assert head_dim == LANES and n_heads % GDN_HEADS_PER_STEP == 0 and 2 * n_heads <= GATE_LANES
    assert t % GDN_CHUNK == 0 and sc_width == v_width

    col_q, col_k, col_v = 0, qk_width, 2 * qk_width
    col_z = col_v + v_width
    col_gates = col_z + v_width
    rest = col_gates + 2 * n_heads
    col_b = col_gates
    col_c = col_b + sc_width
    col_h = col_c + sc_width
    col_ga = col_h + sc_width
    col_gb = col_ga + d

    h2 = x.reshape(b * t, d)
    for l in range(depth):
        w_t = jnp.swapaxes(w_in[l], 0, 1)
        w_main = _wprep(w_t, col_gates=col_gates, gate_cols=rest - col_gates)
        w_g_hi, w_g_lo = _gprep(w_t, col_gates=col_gates, gate_cols=rest - col_gates)
        gate_params = jnp.zeros((8, GATE_LANES), F32)
        gate_params = gate_params.at[0, n_heads:2 * n_heads].set(A_log[l])
        gate_params = gate_params.at[1, n_heads:2 * n_heads].set(dt_bias[l])

        later = [w_proj_a[l], w_proj_b[l], w_out[l], w_gate[l], w_up[l], w_down[l]]
        proj2, gates2, gct, later = _inproj(h2, ln_mix_g[l][None, :], w_main, w_g_hi, w_g_lo,
                                            gate_params, conv_qkv_w[l], later,
                                            n_heads=n_heads, seq_len=t)
        wa_bf, wb_bf, wout_bf, wgate_bf, wup_bf, wdown_bf = later
        proj3 = proj2.reshape(b, t, -1)
        o_a, o_b = _gdn(proj3, gates2.reshape(b, t, GATE_LANES), gct,
                        gdn_norm_g[l][None, :], conv_sc_w[l], n_heads=n_heads, head_dim=head_dim,
                        col_q=col_q, col_k=col_k, col_v=col_v, col_z=col_z,
                        col_b=col_b, col_c=col_c, col_h=col_h, v_width=v_width)
        merged = _merge(o_a.reshape(b * t, v_width), o_b.reshape(b * t, sc_width),
                        wa_bf, wb_bf, proj2, col_ga=col_ga, col_gb=col_gb)
        h_mid, hn = _outproj(merged, wout_bf, h2, ln_ffn_g[l][None, :])
        last = l == depth - 1
        g_last = ln_final_g[None, :] if last else jnp.ones((1, d), F32)
        h2 = _ffn(hn, wgate_bf, wup_bf, wdown_bf, h_mid, g_last, final_norm=last)
    return h2.reshape(b, t, d)
```

```python
import functools
import math

import jax
import jax.numpy as jnp
from jax import lax
from jax.experimental import pallas as pl
from jax.experimental.pallas import tpu as pltpu

EPS = 1e-6
F32 = jnp.float32
BF16 = jnp.bfloat16

LANES = 128
MXU_COLS = 256
V7X_VMEM_CAP_BYTES = 56 * 1024 * 1024
GDN_CHUNK = 128
GDN_HEADS_PER_STEP = 4
GATE_LANES = LANES


def _vmem_limit(block_bytes, scratch_bytes=0, temp_bytes=0):
    need = 2 * block_bytes + scratch_bytes + temp_bytes + (4 << 20)
    return int(min(V7X_VMEM_CAP_BYTES, max(need, 16 << 20)))


def _pick_tile(n, target, align):
    t = min(n, target)
    t -= t % align
    while t >= align:
        if n % t == 0:
            return t
        t -= align
    return n


def _sigmoid(x):
    return 0.5 * jnp.tanh(0.5 * x) + 0.5


def _silu(x):
    h = 0.5 * x
    return h + h * jnp.tanh(h)


def _dot(a, b):
    return jnp.dot(a, b, preferred_element_type=F32)


def _dot_nt(a, b):
    return lax.dot_general(a, b, (((1,), (1,)), ((), ())), preferred_element_type=F32)


def _dot_tn(a, b):
    return lax.dot_general(a, b, (((0,), (0,)), ((), ())), preferred_element_type=F32)


def _rms(x, g):
    return x * lax.rsqrt(jnp.mean(x * x, axis=-1, keepdims=True) + EPS) * g


def _gprep_kernel(g_ref, hi_ref, lo_ref):
    g = g_ref[...]
    pad = jnp.zeros((GATE_LANES - g.shape[0], g.shape[1]), F32)
    w = jnp.concatenate([g, pad], axis=0).T
    hi = w.astype(BF16)
    hi_ref[...] = hi
    lo_ref[...] = (w - hi.astype(F32)).astype(BF16)


def _gprep(w_t, *, col_gates, gate_cols):
    d = w_t.shape[1]
    out = jax.ShapeDtypeStruct((d, GATE_LANES), BF16)
    return pl.pallas_call(
        _gprep_kernel,
        grid=(1,),
        in_specs=[pl.BlockSpec((pl.Element(gate_cols), pl.Element(d)), lambda i: (col_gates, 0))],
        out_specs=[pl.BlockSpec((d, GATE_LANES), lambda i: (0, 0))] * 2,
        out_shape=[out, out],
        name="gprep",
    )(w_t)


NORM_ROWS = 2 * GDN_CHUNK


def _norm_kernel(x_ref, g_ref, wg_hi_ref, wg_lo_ref, gp_ref, xn_ref, gates_ref, gct_ref, *, n_heads):
    rows = NORM_ROWS
    ch_per_pass = rows // GDN_CHUNK
    gp = gp_ref[...]
    row_in_chunk = lax.broadcasted_iota(jnp.int32, (rows, GATE_LANES), 0) & (GDN_CHUNK - 1)
    lane = lax.broadcasted_iota(jnp.int32, (rows, GATE_LANES), 1)

    def norm_rows(r, carry):
        r0 = pl.multiple_of(r * rows, rows)
        y = _rms(x_ref[pl.ds(r0, rows), :], g_ref[...])
        hi = y.astype(BF16)
        xn_ref[pl.ds(r0, rows), :] = hi
        lo = (y - hi.astype(F32)).astype(BF16)
        w_hi = wg_hi_ref[...]
        hi_terms = _dot(hi, jnp.concatenate([w_hi, wg_lo_ref[...]], axis=1))
        gates = hi_terms[:, :GATE_LANES] + hi_terms[:, GATE_LANES:] + _dot(lo, w_hi)
        xa = gates + gp[1:2, :]
        softplus = jnp.maximum(xa, 0.0) + jnp.log(1.0 + jnp.exp(-jnp.abs(xa)))
        gcum = -jnp.exp(gp[0:1, :]) * softplus
        shift = 1
        while shift < GDN_CHUNK:
            gcum = gcum + jnp.where(row_in_chunk >= shift, pltpu.roll(gcum, shift, 0), 0.0)
            shift *= 2
        gates_ref[pl.ds(r0, rows), :] = jnp.where(lane < n_heads, _sigmoid(gates), gcum)
        for c in range(ch_per_pass):
            gct_ref[r * ch_per_pass + c] = gcum[c * GDN_CHUNK:(c + 1) * GDN_CHUNK, :].T
        return carry

    lax.fori_loop(0, x_ref.shape[0] // rows, norm_rows, 0)


def _norm(x2, g, wg_hi, wg_lo, gate_params, *, n_heads):
    m, d = x2.shape
    tm = _pick_tile(m, 1024, NORM_ROWS)
    n_ch = tm // GDN_CHUNK
    blocks = tm * d * 6 + 2 * tm * GATE_LANES * 4 + 2 * d * GATE_LANES * 2
    kern = functools.partial(_norm_kernel, n_heads=n_heads)
    return pl.pallas_call(
        kern,
        grid=(m // tm,),
        in_specs=[
            pl.BlockSpec((tm, d), lambda i: (i, 0)),
            pl.BlockSpec((1, d), lambda i: (0, 0)),
            pl.BlockSpec((d, GATE_LANES), lambda i: (0, 0)),
            pl.BlockSpec((d, GATE_LANES), lambda i: (0, 0)),
            pl.BlockSpec((8, GATE_LANES), lambda i: (0, 0)),
        ],
        out_specs=[
            pl.BlockSpec((tm, d), lambda i: (i, 0)),
            pl.BlockSpec((tm, GATE_LANES), lambda i: (i, 0)),
            pl.BlockSpec((n_ch, GATE_LANES, GDN_CHUNK), lambda i: (i, 0, 0)),
        ],
        out_shape=[
            jax.ShapeDtypeStruct((m, d), BF16),
            jax.ShapeDtypeStruct((m, GATE_LANES), F32),
            jax.ShapeDtypeStruct((m // GDN_CHUNK, GATE_LANES, GDN_CHUNK), F32),
        ],
        compiler_params=pltpu.CompilerParams(
            dimension_semantics=("parallel",),
            vmem_limit_bytes=_vmem_limit(blocks, 0, 8 * NORM_ROWS * d * 4)),
        name="norm",
    )(x2, g, wg_hi, wg_lo, gate_params)


def _inproj_kernel(*refs, n_side, n_tiles, n_conv, seq_len):
    xn_ref, wt_ref, cw_ref = refs[:3]
    side_in = refs[3:3 + n_side]
    out_ref = refs[3 + n_side]
    side_out = refs[4 + n_side:4 + 2 * n_side]
    wbf_ref, cbuf_ref, halo_ref = refs[4 + 2 * n_side:]
    jn = pl.program_id(0)
    im = pl.program_id(1)
    tm, tn = out_ref.shape
    lane_tiles = tn // LANES
    prep_cols = tn // pl.num_programs(1)

    @pl.when(jnp.logical_and(jn == 0, im == 0))
    def _():
        halo_ref[...] = jnp.zeros_like(halo_ref)

    def side_jobs():
        c0 = pl.multiple_of(im * prep_cols, prep_cols)
        w_t = wt_ref[pl.ds(c0, prep_cols), :].T.astype(BF16)
        slot = (jn % 2) * lane_tiles + im * (prep_cols // LANES)
        for c in range(prep_cols // LANES):
            wbf_ref[slot + c] = w_t[:, c * LANES:(c + 1) * LANES]
        for s_in, s_out in zip(side_in, side_out):
            s_out[...] = s_in[...].astype(s_out.dtype)

    def weights():
        slot = ((jn + 1) % 2) * lane_tiles
        return jnp.concatenate([wbf_ref[slot + lt] for lt in range(lane_tiles)], axis=1)

    @pl.when(jn == 0)
    def _():
        side_jobs()

    @pl.when(jn > n_conv)
    def _():
        side_jobs()
        out_ref[...] = _dot(xn_ref[...], weights())

    @pl.when(jnp.logical_and(jn >= 1, jn <= n_conv))
    def _():
        side_jobs()
        cw = cw_ref[...]
        taps = cw.shape[0]
        starts_seq = (im * tm) % seq_len == 0
        acc = _dot(xn_ref[...], weights())
        for lt in range(lane_tiles):
            lanes = slice(lt * LANES, (lt + 1) * LANES)
            x = acc[:, lanes]
            cbuf_ref[lt, pl.ds(0, 8), :] = jnp.where(starts_seq, 0.0, halo_ref[lt])
            cbuf_ref[lt, pl.ds(8, tm), :] = x
            y = x * cw[taps - 1:taps, lanes]
            for s in range(1, taps):
                y = y + cbuf_ref[lt, pl.ds(8 - s, tm), :] * cw[taps - 1 - s:taps - s, lanes]
            halo_ref[lt] = x[tm - 8:tm, :]
            out_ref[:, lanes] = _silu(y)


def _slab_rows(n_rows, steps):
    for r in range(16, n_rows + 1, 16):
        if n_rows % r == 0 and n_rows // r <= steps:
            return r
    return n_rows


def _inproj(xn, w_t, conv_w, side, *, col_gates, gate_cols, seq_len):
    m, d = xn.shape
    n = w_t.shape[0] - gate_cols
    taps, conv_cols = conv_w.shape
    tm = _pick_tile(seq_len, 1024, 16)
    tn = _pick_tile(math.gcd(math.gcd(n, conv_cols), col_gates), 1024, LANES)
    n_m = m // tm
    assert m % tm == 0 and n % tn == 0 and taps <= 8 and gate_cols % 8 == 0
    assert tn % n_m == 0 and (tn // n_m) % LANES == 0
    n_tiles = n // tn
    n_conv = conv_cols // tn
    n_plain = col_gates // tn
    steps = (n_tiles + 1) * n_m
    side_specs, side_shapes, side_bytes = [], [], 0
    for w in side:
        r = _slab_rows(w.shape[0], steps)
        last = w.shape[0] // r - 1
        side_specs.append(pl.BlockSpec((r, w.shape[1]),
                                       lambda j, i, last=last: (jnp.minimum(j * n_m + i, last), 0)))
        side_shapes.append(jax.ShapeDtypeStruct(w.shape, BF16))
        side_bytes += r * w.shape[1] * 6

    def weight_rows(j, i):
        tile = jnp.minimum(j, n_tiles - 1)
        return pl.multiple_of(tile * tn + jnp.where(tile >= n_plain, gate_cols, 0), 8), 0

    blocks = tm * d * 2 + tn * d * 4 + tm * tn * 4 + 8 * tn * 4 + side_bytes
    scratch = 2 * d * tn * 2 + (tn // LANES) * (tm + 16) * LANES * 4
    kern = functools.partial(_inproj_kernel, n_side=len(side), n_tiles=n_tiles, n_conv=n_conv,
                             seq_len=seq_len)
    outs = pl.pallas_call(
        kern,
        grid=(n_tiles + 1, n_m),
        in_specs=[
            pl.BlockSpec((tm, d), lambda j, i: (jnp.where(j == 0, 0, i), 0)),
            pl.BlockSpec((pl.Element(tn), pl.Element(d)), weight_rows),
            pl.BlockSpec((taps, tn), lambda j, i: (0, jnp.clip(j - 1, 0, n_conv - 1))),
        ] + side_specs,
        out_specs=[
            pl.BlockSpec((tm, tn), lambda j, i: (jnp.where(j == 0, 0, i), jnp.maximum(j - 1, 0))),
        ] + side_specs,
        out_shape=[jax.ShapeDtypeStruct((m, n), F32)] + side_shapes,
        scratch_shapes=[
            pltpu.VMEM((2 * (tn // LANES), d, LANES), BF16),
            pltpu.VMEM((tn // LANES, tm + 8, LANES), F32),
            pltpu.VMEM((tn // LANES, 8, LANES), F32),
        ],
        compiler_params=pltpu.CompilerParams(
            dimension_semantics=("arbitrary", "arbitrary"),
            vmem_limit_bytes=_vmem_limit(blocks, scratch, 2 * tm * tn * 4)),
        name="inproj",
    )(xn, w_t, conv_w, *side)
    return outs[0], outs[1:]


def _inv_unit_lower_minus_eye(a_list, rowi, coli):
    n = GDN_CHUNK
    diag16 = (rowi >> 4) == (coli >> 4)
    x = [jnp.where(diag16, a, 0.0) for a in a_list]
    e = [-t for t in x]
    xb = [t.astype(BF16) for t in x]
    x = [_dot(t, t) for t in xb]
    for step in range(3):
        xb = [t.astype(BF16) for t in x]
        if step < 2:
            both = [_dot(jnp.concatenate([ei.astype(BF16), xi], axis=0), xi) for ei, xi in zip(e, xb)]
            ex = [t[:n] for t in both]
            x_next = [t[n:] for t in both]
        else:
            ex = [_dot(ei.astype(BF16), xi) for ei, xi in zip(e, xb)]
            x_next = x
        e = [ei + xi + exi for ei, xi, exi in zip(e, x, ex)]
        x = x_next

    half = 16
    while half < n:
        rows = [slice(r, r + half) for r in range(half, n, 2 * half)]
        pick = lambda t: jnp.concatenate([t[r] for r in rows], axis=0)
        p = lax.broadcasted_iota(jnp.int32, (n // 2, n), 0)
        ci = lax.broadcasted_iota(jnp.int32, (n // 2, n), 1)
        ri = (((p // half) * 2 + 1) * half) + (p % half)
        off = jnp.logical_and((ri >> half.bit_length()) == (ci >> half.bit_length()), ci < (ri & -half))
        eb = [t.astype(BF16) for t in e]
        y = [jnp.where(off, pick(a), 0.0) for a in a_list]
        z = [yi + _dot(yi.astype(BF16), ei) for yi, ei in zip(y, eb)]
        zero = jnp.zeros((half, n), F32)
        z_full = []
        for zi in z:
            pieces = []
            for k in range(len(rows)):
                pieces += [zero, zi[k * half:(k + 1) * half]]
            z_full.append(jnp.concatenate(pieces, axis=0).astype(BF16))
        ez = [_dot(pick(ei).astype(BF16), zf) for ei, zf in zip(e, z_full)]
        new = [pick(ei) - zi - ezi for ei, zi, ezi in zip(e, z, ez)]
        merged = []
        for ei, ni in zip(e, new):
            pieces = []
            for k, r in enumerate(rows):
                pieces += [ei[r.start - half:r.start], ni[k * half:(k + 1) * half]]
            merged.append(jnp.concatenate(pieces, axis=0))
        e = merged
        half *= 2
    return e


def _gdn_kernel(q_ref, k_ref, v_ref, z_ref, gates_ref, gct_ref, ng_ref, scb_ref, scc_ref, sch_ref,
                cws_ref, o_ref, ob_ref, s_ref, halo_ref, *, n_heads, head_dim):
    c_len = GDN_CHUNK
    hb = GDN_HEADS_PER_STEP
    tb = q_ref.shape[1]
    n_chunks = tb // c_len
    hg = pl.program_id(1)

    @pl.when(pl.program_id(2) == 0)
    def _():
        s_ref[...] = jnp.zeros_like(s_ref)
        halo_ref[:, 0:8, :] = jnp.zeros((hb, 8, head_dim), F32)

    gates = gates_ref[0]
    gate_lane = lax.broadcasted_iota(jnp.int32, (tb, GATE_LANES), 1)

    def causal_conv(load_x, cw_ref):
        cw = cw_ref[...]
        taps = cw.shape[0]
        out = []
        for j in range(hb):
            lanes = slice(j * head_dim, (j + 1) * head_dim)
            x = load_x(lanes)
            halo_ref[j, pl.ds(8, tb), :] = x
            y = x * cw[taps - 1:taps, lanes]
            for s in range(1, taps):
                y = y + halo_ref[j, pl.ds(8 - s, tb), :] * cw[taps - 1 - s:taps - s, lanes]
            halo_ref[j, pl.ds(0, 8), :] = x[tb - 8:tb, :]
            out.append(y)
        return out

    head_lanes = [slice(j * head_dim, (j + 1) * head_dim) for j in range(hb)]
    yq = [q_ref[0, :, ln] for ln in head_lanes]
    yk = [k_ref[0, :, ln] for ln in head_lanes]
    yv = [v_ref[0, :, ln] for ln in head_lanes]

    rowi = lax.broadcasted_iota(jnp.int32, (c_len, c_len), 0)
    coli = lax.broadcasted_iota(jnp.int32, (c_len, c_len), 1)
    incl = rowi >= coli
    strict = rowi > coli

    qh, kh, vh, beta, gcol = [], [], [], [], []
    for j in range(hb):
        head = hg * hb + j
        qj = yq[j]
        kj = yk[j]
        qh.append(qj * (lax.rsqrt(jnp.sum(qj * qj, axis=1, keepdims=True) + EPS) * head_dim ** -0.5))
        kh.append(kj * lax.rsqrt(jnp.sum(kj * kj, axis=1, keepdims=True) + EPS))
        vh.append(yv[j])
        beta.append(jnp.sum(jnp.where(gate_lane == head, gates, 0.0), axis=1, keepdims=True))
        gcol.append(jnp.sum(jnp.where(gate_lane == head + n_heads, gates, 0.0), axis=1, keepdims=True))

    probs = [(j, c) for j in range(hb) for c in range(n_chunks)]
    rows = {c: slice(c * c_len, (c + 1) * c_len) for c in range(n_chunks)}
    g_c = [gcol[j][rows[c]] for j, c in probs]
    g_r = [gct_ref[c, pl.ds(hg * hb + j + n_heads, 1), :] for j, c in probs]
    g_last = [g[c_len - 1:c_len, :] for g in g_c]
    decay = [jnp.where(incl, jnp.exp(jnp.where(incl, gc - gr, 0.0)), 0.0) for gc, gr in zip(g_c, g_r)]
    e_g = [jnp.exp(g) for g in g_c]
    kc = [kh[j][rows[c]] for j, c in probs]
    qc = [qh[j][rows[c]] for j, c in probs]
    bb = [beta[j][rows[c]] for j, c in probs]
    kb = [k * b_ for k, b_ in zip(kc, bb)]
    kbf = [k.astype(BF16) for k in kc]
    kk = [_dot_nt(a.astype(BF16), b_) for a, b_ in zip(kb, kbf)]
    qk = [_dot_nt(a.astype(BF16), b_) for a, b_ in zip(qc, kbf)]
    a_low = [jnp.where(strict, t * d_, 0.0) for t, d_ in zip(kk, decay)]
    attn = [(t * d_).astype(BF16) for t, d_ in zip(qk, decay)]
    e_inv = _inv_unit_lower_minus_eye(a_low, rowi, coli)

    ysc = causal_conv(lambda ln: scc_ref[0, :, ln] * sch_ref[0, :, ln], cws_ref)
    for j in range(hb):
        lanes = slice(j * head_dim, (j + 1) * head_dim)
        ob_ref[0, :, lanes] = (scb_ref[0, :, lanes] * ysc[j]).astype(ob_ref.dtype)

    rhs = [jnp.concatenate([vh[j][rows[c]] * b_, kb_ * eg], axis=1)
           for (j, c), b_, kb_, eg in zip(probs, bb, kb, e_g)]
    sol = [r + _dot(e.astype(BF16), r.astype(BF16)) for r, e in zip(rhs, e_inv)]
    u = [t[:, :head_dim] for t in sol]
    wq = [jnp.concatenate([t[:, head_dim:], q * eg], axis=0).astype(BF16)
          for t, q, eg in zip(sol, qc, e_g)]
    k_dec = [(k * jnp.exp(gl - g)).astype(BF16) for k, gl, g in zip(kc, g_last, g_c)]
    e_last = [jnp.exp(gl) for gl in g_last]

    z = z_ref[0]
    ng = ng_ref[...]
    s = [s_ref[j] for j in range(hb)]
    for c in range(n_chunks):
        idx = [j * n_chunks + c for j in range(hb)]
        ws = [_dot(wq[i], s[j].astype(BF16)) for j, i in enumerate(idx)]
        v_bf = [(u[i] - ws[j][:c_len]).astype(BF16) for j, i in enumerate(idx)]
        o = [ws[j][c_len:] + _dot(attn[i], v_bf[j]) for j, i in enumerate(idx)]
        s = [s[j] * e_last[i] + _dot_tn(k_dec[i], v_bf[j]) for j, i in enumerate(idx)]
        for j in range(hb):
            lanes = slice(j * head_dim, (j + 1) * head_dim)
            zc = z[rows[c], lanes]
            o_ref[0, rows[c], lanes] = (_rms(o[j], ng) * _silu(zc)).astype(o_ref.dtype)
    for j in range(hb):
        s_ref[j] = s[j]


def _gdn(proj3, gates3, gct, norm_g, conv_sc_w, *, n_heads, head_dim, col_q, col_k, col_v, col_z,
         col_b, col_c, col_h, v_width):
    b, t, _ = proj3.shape
    hb = GDN_HEADS_PER_STEP
    gw = hb * head_dim
    tb = _pick_tile(t, 4 * GDN_CHUNK, GDN_CHUNK)
    n_ch = tb // GDN_CHUNK
    steps = t // tb
    taps_sc = conv_sc_w.shape[0]
    qb, kb_, vb, zb = col_q // gw, col_k // gw, col_v // gw, col_z // gw
    sb, sc, sh = col_b // gw, col_c // gw, col_h // gw
    blocks = 7 * tb * gw * 4 + 2 * tb * GATE_LANES * 4 + 8 * gw * 4 + 2 * tb * gw * 2
    scratch = hb * head_dim * head_dim * 4 + (tb + 8) * gw * 4
    kern = functools.partial(_gdn_kernel, n_heads=n_heads, head_dim=head_dim)
    out = jax.ShapeDtypeStruct((b, t, v_width), BF16)
    return pl.pallas_call(
        kern,
        grid=(b, n_heads // hb, steps),
        in_specs=[
            pl.BlockSpec((1, tb, gw), lambda i, h, s: (i, s, qb + h)),
            pl.BlockSpec((1, tb, gw), lambda i, h, s: (i, s, kb_ + h)),
            pl.BlockSpec((1, tb, gw), lambda i, h, s: (i, s, vb + h)),
            pl.BlockSpec((1, tb, gw), lambda i, h, s: (i, s, zb + h)),
            pl.BlockSpec((1, tb, GATE_LANES), lambda i, h, s: (i, s, 0)),
            pl.BlockSpec((n_ch, GATE_LANES, GDN_CHUNK), lambda i, h, s: (i * steps + s, 0, 0)),
            pl.BlockSpec((1, head_dim), lambda i, h, s: (0, 0)),
            pl.BlockSpec((1, tb, gw), lambda i, h, s: (i, s, sb + h)),
            pl.BlockSpec((1, tb, gw), lambda i, h, s: (i, s, sc + h)),
            pl.BlockSpec((1, tb, gw), lambda i, h, s: (i, s, sh + h)),
            pl.BlockSpec((taps_sc, gw), lambda i, h, s: (0, h)),
        ],
        out_specs=[pl.BlockSpec((1, tb, gw), lambda i, h, s: (i, s, h))] * 2,
        out_shape=[out, out],
        scratch_shapes=[
            pltpu.VMEM((hb, head_dim, head_dim), F32),
            pltpu.VMEM((hb, tb + 8, head_dim), F32),
        ],
        compiler_params=pltpu.CompilerParams(
            dimension_semantics=("parallel", "parallel", "arbitrary"),
            vmem_limit_bytes=_vmem_limit(blocks, scratch, 16 << 20)),
        name="gdn",
    )(proj3, proj3, proj3, proj3, gates3, gct, norm_g, proj3, proj3, proj3, conv_sc_w)


def _merge_kernel(*refs, n_side):
    a_ref, b_ref, wa_ref, wb_ref, ga_ref, gb_ref = refs[:6]
    side_in = refs[6:6 + n_side]
    o_ref = refs[6 + n_side]
    side_out = refs[7 + n_side:]
    for s_in, s_out in zip(side_in, side_out):
        s_out[...] = s_in[...].astype(s_out.dtype)
    pa = _dot(a_ref[...], wa_ref[...])
    pb = _dot(b_ref[...], wb_ref[...])
    o_ref[...] = (_sigmoid(ga_ref[...]) * pa + _sigmoid(gb_ref[...]) * pb).astype(o_ref.dtype)


def _merge(o_a, o_b, wa, wb, proj2, side, *, col_ga, col_gb):
    m, ka = o_a.shape
    kb_ = o_b.shape[1]
    d = wa.shape[1]
    tm = _pick_tile(m, 1024, 16)
    tn = _pick_tile(d, 512, LANES)
    nj = d // tn
    steps = (m // tm) * nj
    ga0, gb0 = col_ga // tn, col_gb // tn
    side_specs, side_shapes, side_bytes = [], [], 0
    for w in side:
        r = _slab_rows(w.shape[0], steps)
        last = w.shape[0] // r - 1
        side_specs.append(pl.BlockSpec((r, w.shape[1]),
                                       lambda i, j, last=last: (jnp.minimum(i * nj + j, last), 0)))
        side_shapes.append(jax.ShapeDtypeStruct(w.shape, BF16))
        side_bytes += r * w.shape[1] * 6
    blocks = (tm * (ka + kb_) * 2 + (ka + kb_) * tn * 2 + 2 * tm * tn * 4 + tm * tn * 2
              + side_bytes)
    kern = functools.partial(_merge_kernel, n_side=len(side))
    outs = pl.pallas_call(
        kern,
        grid=(m // tm, nj),
        in_specs=[
            pl.BlockSpec((tm, ka), lambda i, j: (i, 0)),
            pl.BlockSpec((tm, kb_), lambda i, j: (i, 0)),
            pl.BlockSpec((ka, tn), lambda i, j: (0, j)),
            pl.BlockSpec((kb_, tn), lambda i, j: (0, j)),
            pl.BlockSpec((tm, tn), lambda i, j: (i, ga0 + j)),
            pl.BlockSpec((tm, tn), lambda i, j: (i, gb0 + j)),
        ] + side_specs,
        out_specs=[pl.BlockSpec((tm, tn), lambda i, j: (i, j))] + side_specs,
        out_shape=[jax.ShapeDtypeStruct((m, d), BF16)] + side_shapes,
        compiler_params=pltpu.CompilerParams(
            dimension_semantics=("arbitrary", "arbitrary"),
            vmem_limit_bytes=_vmem_limit(blocks, 0, 4 * tm * tn * 4)),
        name="merge",
    )(o_a, o_b, wa, wb, proj2, proj2, *side)
    return outs[0], outs[1:]


def _outproj_kernel(m_ref, w_ref, x_ref, g_ref, h_ref, hn_ref):
    h = x_ref[...] + _dot(m_ref[...], w_ref[...])
    h_ref[...] = h
    hn_ref[...] = _rms(h, g_ref[...]).astype(hn_ref.dtype)


def _outproj(merged, w_out, x2, g):
    m, d = x2.shape
    k = merged.shape[1]
    tm = _pick_tile(m, 512, 16)
    blocks = tm * k * 2 + k * d * 2 + 2 * tm * d * 4 + tm * d * 2 + d * 4
    return pl.pallas_call(
        _outproj_kernel,
        grid=(m // tm,),
        in_specs=[
            pl.BlockSpec((tm, k), lambda i: (i, 0)),
            pl.BlockSpec((k, d), lambda i: (0, 0)),
            pl.BlockSpec((tm, d), lambda i: (i, 0)),
            pl.BlockSpec((1, d), lambda i: (0, 0)),
        ],
        out_specs=[
            pl.BlockSpec((tm, d), lambda i: (i, 0)),
            pl.BlockSpec((tm, d), lambda i: (i, 0)),
        ],
        out_shape=[
            jax.ShapeDtypeStruct((m, d), F32),
            jax.ShapeDtypeStruct((m, d), BF16),
        ],
        compiler_params=pltpu.CompilerParams(
            dimension_semantics=("parallel",),
            vmem_limit_bytes=_vmem_limit(blocks, 0, 3 * tm * d * 4)),
        name="outproj",
    )(merged, w_out, x2, g)


def _ffn_kernel(hn_ref, wg_ref, wu_ref, wd_ref, h_hbm, g_ref, o_ref, sem, *, final_norm):
    i = pl.program_id(0)
    f = pl.program_id(1)
    tm = o_ref.shape[0]

    def residual_copy():
        return pltpu.make_async_copy(h_hbm.at[pl.ds(pl.multiple_of(i * tm, tm), tm), :], o_ref, sem)

    @pl.when(f == 0)
    def _():
        residual_copy().start()

    hn = hn_ref[...]
    ff = (_silu(_dot(hn, wg_ref[...])) * _dot(hn, wu_ref[...])).astype(BF16)

    @pl.when(f == 0)
    def _():
        residual_copy().wait()

    o_ref[...] += _dot(ff, wd_ref[...])

    if final_norm:
        @pl.when(f == pl.num_programs(1) - 1)
        def _():
            o_ref[...] = _rms(o_ref[...], g_ref[...])


def _ffn(hn, wg, wu, wd, h, g, *, final_norm):
    m, d = h.shape
    dff = wg.shape[1]
    tm = _pick_tile(m, 1024, 16)
    tf = _pick_tile(dff, 512, LANES)
    blocks = tm * d * 2 + 3 * d * tf * 2 + tm * d * 4 + d * 4
    kern = functools.partial(_ffn_kernel, final_norm=final_norm)
    return pl.pallas_call(
        kern,
        grid=(m // tm, dff // tf),
        in_specs=[
            pl.BlockSpec((tm, d), lambda i, f: (i, 0)),
            pl.BlockSpec((d, tf), lambda i, f: (0, f)),
            pl.BlockSpec((d, tf), lambda i, f: (0, f)),
            pl.BlockSpec((tf, d), lambda i, f: (f, 0)),
            pl.BlockSpec(memory_space=pl.ANY),
            pl.BlockSpec((1, d), lambda i, f: (0, 0)),
        ],
        out_specs=pl.BlockSpec((tm, d), lambda i, f: (i, 0)),
        out_shape=jax.ShapeDtypeStruct((m, d), F32),
        scratch_shapes=[pltpu.SemaphoreType.DMA(())],
        compiler_params=pltpu.CompilerParams(
            dimension_semantics=("parallel", "arbitrary"),
            vmem_limit_bytes=_vmem_limit(blocks, 0, 4 * tm * tf * 4 + tm * d * 4)),
        name="ffn",
    )(hn, wg, wu, wd, h, g)


def kernel(x, ln_mix_g, w_in, conv_qkv_w, A_log, dt_bias, gdn_norm_g, w_proj_a, conv_sc_w, w_proj_b,
           w_out, ln_ffn_g, w_gate, w_up, w_down, ln_final_g):
    b, t, d = x.shape
    depth = w_in.shape[0]
    n_heads = A_log.shape[1]
    v_width = w_proj_a.shape[1]
    qk_width = (conv_qkv_w.shape[2] - v_width) // 2
    sc_width = w_proj_b.shape[1]
    head_dim = gdn_norm_g.shape[1]
    assert qk_width == n_heads * head_dim and v_width == n_heads * head_dim
    assert head_dim == LANES and n_heads % GDN_HEADS_PER_STEP == 0 and 2 * n_heads <= GATE_LANES
    assert t % GDN_CHUNK == 0 and sc_width == v_width

    col_q, col_k, col_v = 0, qk_width, 2 * qk_width
    col_z = col_v + v_width
    col_gates = col_z + v_width
    rest = col_gates + 2 * n_heads
    col_b = col_gates
    col_c = col_b + sc_width
    col_h = col_c + sc_width
    col_ga = col_h + sc_width
    col_gb = col_ga + d

    h2 = x.reshape(b * t, d)
    for l in range(depth):
        w_t = jnp.swapaxes(w_in[l], 0, 1)
        w_g_hi, w_g_lo = _gprep(w_t, col_gates=col_gates, gate_cols=rest - col_gates)
        gate_params = jnp.zeros((8, GATE_LANES), F32)
        gate_params = gate_params.at[0, n_heads:2 * n_heads].set(A_log[l])
        gate_params = gate_params.at[1, n_heads:2 * n_heads].set(dt_bias[l])

        xn, gates2, gct = _norm(h2, ln_mix_g[l][None, :], w_g_hi, w_g_lo, gate_params, n_heads=n_heads)
        proj2, (wa_bf, wb_bf, wout_bf) = _inproj(
            xn, w_t, conv_qkv_w[l], [w_proj_a[l], w_proj_b[l], w_out[l]], col_gates=col_gates,
            gate_cols=rest - col_gates, seq_len=t)
        proj3 = proj2.reshape(b, t, -1)
        o_a, o_b = _gdn(proj3, gates2.reshape(b, t, GATE_LANES), gct,
                        gdn_norm_g[l][None, :], conv_sc_w[l], n_heads=n_heads, head_dim=head_dim,
                        col_q=col_q, col_k=col_k, col_v=col_v, col_z=col_z,
                        col_b=col_b, col_c=col_c, col_h=col_h, v_width=v_width)
        merged, (wgate_bf, wup_bf, wdown_bf) = _merge(
            o_a.reshape(b * t, v_width), o_b.reshape(b * t, sc_width), wa_bf, wb_bf, proj2,
            [w_gate[l], w_up[l], w_down[l]], col_ga=col_ga, col_gb=col_gb)
        h_mid, hn = _outproj(merged, wout_bf, h2, ln_ffn_g[l][None, :])
        last = l == depth - 1
        g_last = ln_final_g[None, :] if last else jnp.ones((1, d), F32)
        h2 = _ffn(hn, wgate_bf, wup_bf, wdown_bf, h_mid, g_last, final_norm=last)
    return h2.reshape(b, t, d)
```

```python
import functools
import math

import jax
import jax.numpy as jnp
from jax import lax
from jax.experimental import pallas as pl
from jax.experimental.pallas import tpu as pltpu

EPS = 1e-6
F32 = jnp.float32
BF16 = jnp.bfloat16

LANES = 128
MXU_COLS = 256
V7X_VMEM_CAP_BYTES = 56 * 1024 * 1024
GDN_CHUNK = 128
GDN_HEADS_PER_STEP = 4
GATE_LANES = LANES


def _vmem_limit(block_bytes, scratch_bytes=0, temp_bytes=0):
    need = 2 * block_bytes + scratch_bytes + temp_bytes + (4 << 20)
    return int(min(V7X_VMEM_CAP_BYTES, max(need, 16 << 20)))


def _pick_tile(n, target, align):
    t = min(n, target)
    t -= t % align
    while t >= align:
        if n % t == 0:
            return t
        t -= align
    return n


def _sigmoid(x):
    return 0.5 * jnp.tanh(0.5 * x) + 0.5


def _silu(x):
    h = 0.5 * x
    return h + h * jnp.tanh(h)


def _dot(a, b):
    return jnp.dot(a, b, preferred_element_type=F32)


def _dot_nt(a, b):
    return lax.dot_general(a, b, (((1,), (1,)), ((), ())), preferred_element_type=F32)


def _dot_tn(a, b):
    return lax.dot_general(a, b, (((0,), (0,)), ((), ())), preferred_element_type=F32)


def _rms(x, g):
    return x * lax.rsqrt(jnp.mean(x * x, axis=-1, keepdims=True) + EPS) * g


def _gprep_kernel(g_ref, hi_ref, lo_ref):
    g = g_ref[...]
    pad = jnp.zeros((GATE_LANES - g.shape[0], g.shape[1]), F32)
    w = jnp.concatenate([g, pad], axis=0).T
    hi = w.astype(BF16)
    hi_ref[...] = hi
    lo_ref[...] = (w - hi.astype(F32)).astype(BF16)


def _gprep(w_t, *, col_gates, gate_cols):
    d = w_t.shape[1]
    out = jax.ShapeDtypeStruct((d, GATE_LANES), BF16)
    return pl.pallas_call(
        _gprep_kernel,
        grid=(1,),
        in_specs=[pl.BlockSpec((pl.Element(gate_cols), pl.Element(d)), lambda i: (col_gates, 0))],
        out_specs=[pl.BlockSpec((d, GATE_LANES), lambda i: (0, 0))] * 2,
        out_shape=[out, out],
        name="gprep",
    )(w_t)


NORM_ROWS = 2 * GDN_CHUNK


def _norm_kernel(x_ref, g_ref, wg_hi_ref, wg_lo_ref, gp_ref, xn_ref, gates_ref, gct_ref, *, n_heads):
    rows = NORM_ROWS
    ch_per_pass = rows // GDN_CHUNK
    gp = gp_ref[...]
    row_in_chunk = lax.broadcasted_iota(jnp.int32, (rows, GATE_LANES), 0) & (GDN_CHUNK - 1)
    lane = lax.broadcasted_iota(jnp.int32, (rows, GATE_LANES), 1)

    def norm_rows(r, carry):
        r0 = pl.multiple_of(r * rows, rows)
        y = _rms(x_ref[pl.ds(r0, rows), :], g_ref[...])
        hi = y.astype(BF16)
        xn_ref[pl.ds(r0, rows), :] = hi
        lo = (y - hi.astype(F32)).astype(BF16)
        w_hi = wg_hi_ref[...]
        hi_terms = _dot(hi, jnp.concatenate([w_hi, wg_lo_ref[...]], axis=1))
        gates = hi_terms[:, :GATE_LANES] + hi_terms[:, GATE_LANES:] + _dot(lo, w_hi)
        xa = gates + gp[1:2, :]
        softplus = jnp.maximum(xa, 0.0) + jnp.log(1.0 + jnp.exp(-jnp.abs(xa)))
        gcum = -jnp.exp(gp[0:1, :]) * softplus
        shift = 1
        while shift < GDN_CHUNK:
            gcum = gcum + jnp.where(row_in_chunk >= shift, pltpu.roll(gcum, shift, 0), 0.0)
            shift *= 2
        gates_ref[pl.ds(r0, rows), :] = jnp.where(lane < n_heads, _sigmoid(gates), gcum)
        for c in range(ch_per_pass):
            gct_ref[r * ch_per_pass + c] = gcum[c * GDN_CHUNK:(c + 1) * GDN_CHUNK, :].T
        return carry

    lax.fori_loop(0, x_ref.shape[0] // rows, norm_rows, 0)


def _norm(x2, g, wg_hi, wg_lo, gate_params, *, n_heads):
    m, d = x2.shape
    tm = _pick_tile(m, 1024, NORM_ROWS)
    n_ch = tm // GDN_CHUNK
    blocks = tm * d * 6 + 2 * tm * GATE_LANES * 4 + 2 * d * GATE_LANES * 2
    kern = functools.partial(_norm_kernel, n_heads=n_heads)
    return pl.pallas_call(
        kern,
        grid=(m // tm,),
        in_specs=[
            pl.BlockSpec((tm, d), lambda i: (i, 0)),
            pl.BlockSpec((1, d), lambda i: (0, 0)),
            pl.BlockSpec((d, GATE_LANES), lambda i: (0, 0)),
            pl.BlockSpec((d, GATE_LANES), lambda i: (0, 0)),
            pl.BlockSpec((8, GATE_LANES), lambda i: (0, 0)),
        ],
        out_specs=[
            pl.BlockSpec((tm, d), lambda i: (i, 0)),
            pl.BlockSpec((tm, GATE_LANES), lambda i: (i, 0)),
            pl.BlockSpec((n_ch, GATE_LANES, GDN_CHUNK), lambda i: (i, 0, 0)),
        ],
        out_shape=[
            jax.ShapeDtypeStruct((m, d), BF16),
            jax.ShapeDtypeStruct((m, GATE_LANES), F32),
            jax.ShapeDtypeStruct((m // GDN_CHUNK, GATE_LANES, GDN_CHUNK), F32),
        ],
        compiler_params=pltpu.CompilerParams(
            dimension_semantics=("parallel",),
            vmem_limit_bytes=_vmem_limit(blocks, 0, 8 * NORM_ROWS * d * 4)),
        name="norm",
    )(x2, g, wg_hi, wg_lo, gate_params)


def _inproj_kernel(*refs, n_side, n_tiles, n_conv, seq_len):
    xn_ref, wt_ref, cw_ref = refs[:3]
    side_in = refs[3:3 + n_side]
    out_ref = refs[3 + n_side]
    side_out = refs[4 + n_side:4 + 2 * n_side]
    wbf_ref, cbuf_ref, halo_ref = refs[4 + 2 * n_side:]
    jn = pl.program_id(0)
    im = pl.program_id(1)
    tm, tn = out_ref.shape
    lane_tiles = tn // LANES
    prep_cols = wt_ref.shape[0]

    @pl.when(jnp.logical_and(jn == 0, im == 0))
    def _():
        halo_ref[...] = jnp.zeros_like(halo_ref)

    def side_jobs():
        w_t = wt_ref[...].T.astype(BF16)
        slot = (jn % 2) * lane_tiles + im * (prep_cols // LANES)
        for c in range(prep_cols // LANES):
            wbf_ref[slot + c] = w_t[:, c * LANES:(c + 1) * LANES]
        for s_in, s_out in zip(side_in, side_out):
            s_out[...] = s_in[...].astype(s_out.dtype)

    def weights():
        slot = ((jn + 1) % 2) * lane_tiles
        return jnp.concatenate([wbf_ref[slot + lt] for lt in range(lane_tiles)], axis=1)

    @pl.when(jn == 0)
    def _():
        side_jobs()

    @pl.when(jn > n_conv)
    def _():
        side_jobs()
        out_ref[...] = _dot(xn_ref[...], weights())

    @pl.when(jnp.logical_and(jn >= 1, jn <= n_conv))
    def _():
        side_jobs()
        cw = cw_ref[...]
        taps = cw.shape[0]
        starts_seq = (im * tm) % seq_len == 0
        acc = _dot(xn_ref[...], weights())
        for lt in range(lane_tiles):
            lanes = slice(lt * LANES, (lt + 1) * LANES)
            x = acc[:, lanes]
            cbuf_ref[lt, pl.ds(0, 8), :] = jnp.where(starts_seq, 0.0, halo_ref[lt])
            cbuf_ref[lt, pl.ds(8, tm), :] = x
            y = x * cw[taps - 1:taps, lanes]
            for s in range(1, taps):
                y = y + cbuf_ref[lt, pl.ds(8 - s, tm), :] * cw[taps - 1 - s:taps - s, lanes]
            halo_ref[lt] = x[tm - 8:tm, :]
            out_ref[:, lanes] = _silu(y)


def _slab_rows(n_rows, steps):
    for r in range(16, n_rows + 1, 16):
        if n_rows % r == 0 and n_rows // r <= steps:
            return r
    return n_rows


def _inproj(xn, w_t, conv_w, side, *, col_gates, gate_cols, seq_len):
    m, d = xn.shape
    n = w_t.shape[0] - gate_cols
    taps, conv_cols = conv_w.shape
    tm = _pick_tile(seq_len, 1024, 16)
    tn = _pick_tile(math.gcd(math.gcd(n, conv_cols), col_gates), 1024, LANES)
    n_m = m // tm
    assert m % tm == 0 and n % tn == 0 and taps <= 8 and gate_cols % 8 == 0
    assert tn % n_m == 0 and (tn // n_m) % LANES == 0
    n_tiles = n // tn
    n_conv = conv_cols // tn
    n_plain = col_gates // tn
    steps = (n_tiles + 1) * n_m
    side_specs, side_shapes, side_bytes = [], [], 0
    for w in side:
        r = _slab_rows(w.shape[0], steps)
        last = w.shape[0] // r - 1
        side_specs.append(pl.BlockSpec((r, w.shape[1]),
                                       lambda j, i, last=last: (jnp.minimum(j * n_m + i, last), 0)))
        side_shapes.append(jax.ShapeDtypeStruct(w.shape, BF16))
        side_bytes += r * w.shape[1] * 6

    prep_cols = tn // n_m

    def weight_rows(j, i):
        tile = jnp.minimum(j, n_tiles - 1)
        start = tile * tn + jnp.where(tile >= n_plain, gate_cols, 0) + i * prep_cols
        return pl.multiple_of(start, 8), 0

    blocks = tm * d * 2 + prep_cols * d * 4 + tm * tn * 4 + 8 * tn * 4 + side_bytes
    scratch = 2 * d * tn * 2 + (tn // LANES) * (tm + 16) * LANES * 4
    kern = functools.partial(_inproj_kernel, n_side=len(side), n_tiles=n_tiles, n_conv=n_conv,
                             seq_len=seq_len)
    outs = pl.pallas_call(
        kern,
        grid=(n_tiles + 1, n_m),
        in_specs=[
            pl.BlockSpec((tm, d), lambda j, i: (jnp.where(j == 0, 0, i), 0)),
            pl.BlockSpec((pl.Element(prep_cols), pl.Element(d)), weight_rows),
            pl.BlockSpec((taps, tn), lambda j, i: (0, jnp.clip(j - 1, 0, n_conv - 1))),
        ] + side_specs,
        out_specs=[
            pl.BlockSpec((tm, tn), lambda j, i: (jnp.where(j == 0, 0, i), jnp.maximum(j - 1, 0))),
        ] + side_specs,
        out_shape=[jax.ShapeDtypeStruct((m, n), F32)] + side_shapes,
        scratch_shapes=[
            pltpu.VMEM((2 * (tn // LANES), d, LANES), BF16),
            pltpu.VMEM((tn // LANES, tm + 8, LANES), F32),
            pltpu.VMEM((tn // LANES, 8, LANES), F32),
        ],
        compiler_params=pltpu.CompilerParams(
            dimension_semantics=("arbitrary", "arbitrary"),
            vmem_limit_bytes=_vmem_limit(blocks, scratch, 2 * tm * tn * 4)),
        name="inproj",
    )(xn, w_t, conv_w, *side)
    return outs[0], outs[1:]


def _inv_unit_lower_minus_eye(a_list, rowi, coli):
    n = GDN_CHUNK
    diag16 = (rowi >> 4) == (coli >> 4)
    x = [jnp.where(diag16, a, 0.0) for a in a_list]
    e = [-t for t in x]
    xb = [t.astype(BF16) for t in x]
    x = [_dot(t, t) for t in xb]
    for step in range(3):
        xb = [t.astype(BF16) for t in x]
        if step < 2:
            both = [_dot(jnp.concatenate([ei.astype(BF16), xi], axis=0), xi) for ei, xi in zip(e, xb)]
            ex = [t[:n] for t in both]
            x_next = [t[n:] for t in both]
        else:
            ex = [_dot(ei.astype(BF16), xi) for ei, xi in zip(e, xb)]
            x_next = x
        e = [ei + xi + exi for ei, xi, exi in zip(e, x, ex)]
        x = x_next

    half = 16
    while half < n:
        rows = [slice(r, r + half) for r in range(half, n, 2 * half)]
        pick = lambda t: jnp.concatenate([t[r] for r in rows], axis=0)
        p = lax.broadcasted_iota(jnp.int32, (n // 2, n), 0)
        ci = lax.broadcasted_iota(jnp.int32, (n // 2, n), 1)
        ri = (((p // half) * 2 + 1) * half) + (p % half)
        off = jnp.logical_and((ri >> half.bit_length()) == (ci >> half.bit_length()), ci < (ri & -half))
        eb = [t.astype(BF16) for t in e]
        y = [jnp.where(off, pick(a), 0.0) for a in a_list]
        z = [yi + _dot(yi.astype(BF16), ei) for yi, ei in zip(y, eb)]
        zero = jnp.zeros((half, n), F32)
        z_full = []
        for zi in z:
            pieces = []
            for k in range(len(rows)):
                pieces += [zero, zi[k * half:(k + 1) * half]]
            z_full.append(jnp.concatenate(pieces, axis=0).astype(BF16))
        ez = [_dot(pick(ei).astype(BF16), zf) for ei, zf in zip(e, z_full)]
        new = [pick(ei) - zi - ezi for ei, zi, ezi in zip(e, z, ez)]
        merged = []
        for ei, ni in zip(e, new):
            pieces = []
            for k, r in enumerate(rows):
                pieces += [ei[r.start - half:r.start], ni[k * half:(k + 1) * half]]
            merged.append(jnp.concatenate(pieces, axis=0))
        e = merged
        half *= 2
    return e


def _gdn_kernel(q_ref, k_ref, v_ref, z_ref, gates_ref, gct_ref, ng_ref, scb_ref, scc_ref, sch_ref,
                cws_ref, o_ref, ob_ref, s_ref, halo_ref, *, n_heads, head_dim):
    c_len = GDN_CHUNK
    hb = GDN_HEADS_PER_STEP
    tb = q_ref.shape[1]
    n_chunks = tb // c_len
    hg = pl.program_id(1)

    @pl.when(pl.program_id(2) == 0)
    def _():
        s_ref[...] = jnp.zeros_like(s_ref)
        halo_ref[:, 0:8, :] = jnp.zeros((hb, 8, head_dim), F32)

    gates = gates_ref[0]
    gate_lane = lax.broadcasted_iota(jnp.int32, (tb, GATE_LANES), 1)

    def causal_conv(load_x, cw_ref):
        cw = cw_ref[...]
        taps = cw.shape[0]
        out = []
        for j in range(hb):
            lanes = slice(j * head_dim, (j + 1) * head_dim)
            x = load_x(lanes)
            halo_ref[j, pl.ds(8, tb), :] = x
            y = x * cw[taps - 1:taps, lanes]
            for s in range(1, taps):
                y = y + halo_ref[j, pl.ds(8 - s, tb), :] * cw[taps - 1 - s:taps - s, lanes]
            halo_ref[j, pl.ds(0, 8), :] = x[tb - 8:tb, :]
            out.append(y)
        return out

    head_lanes = [slice(j * head_dim, (j + 1) * head_dim) for j in range(hb)]
    yq = [q_ref[0, :, ln] for ln in head_lanes]
    yk = [k_ref[0, :, ln] for ln in head_lanes]
    yv = [v_ref[0, :, ln] for ln in head_lanes]

    rowi = lax.broadcasted_iota(jnp.int32, (c_len, c_len), 0)
    coli = lax.broadcasted_iota(jnp.int32, (c_len, c_len), 1)
    incl = rowi >= coli
    strict = rowi > coli

    qh, kh, vh, beta, gcol = [], [], [], [], []
    for j in range(hb):
        head = hg * hb + j
        qj = yq[j]
        kj = yk[j]
        qh.append(qj * (lax.rsqrt(jnp.sum(qj * qj, axis=1, keepdims=True) + EPS) * head_dim ** -0.5))
        kh.append(kj * lax.rsqrt(jnp.sum(kj * kj, axis=1, keepdims=True) + EPS))
        vh.append(yv[j])
        beta.append(jnp.sum(jnp.where(gate_lane == head, gates, 0.0), axis=1, keepdims=True))
        gcol.append(jnp.sum(jnp.where(gate_lane == head + n_heads, gates, 0.0), axis=1, keepdims=True))

    probs = [(j, c) for j in range(hb) for c in range(n_chunks)]
    rows = {c: slice(c * c_len, (c + 1) * c_len) for c in range(n_chunks)}
    g_c = [gcol[j][rows[c]] for j, c in probs]
    g_r = [gct_ref[c, pl.ds(hg * hb + j + n_heads, 1), :] for j, c in probs]
    g_last = [g[c_len - 1:c_len, :] for g in g_c]
    decay = [jnp.where(incl, jnp.exp(jnp.where(incl, gc - gr, 0.0)), 0.0) for gc, gr in zip(g_c, g_r)]
    e_g = [jnp.exp(g) for g in g_c]
    kc = [kh[j][rows[c]] for j, c in probs]
    qc = [qh[j][rows[c]] for j, c in probs]
    bb = [beta[j][rows[c]] for j, c in probs]
    kb = [k * b_ for k, b_ in zip(kc, bb)]
    kbf = [k.astype(BF16) for k in kc]
    kk = [_dot_nt(a.astype(BF16), b_) for a, b_ in zip(kb, kbf)]
    qk = [_dot_nt(a.astype(BF16), b_) for a, b_ in zip(qc, kbf)]
    a_low = [jnp.where(strict, t * d_, 0.0) for t, d_ in zip(kk, decay)]
    attn = [(t * d_).astype(BF16) for t, d_ in zip(qk, decay)]
    e_inv = _inv_unit_lower_minus_eye(a_low, rowi, coli)

    ysc = causal_conv(lambda ln: scc_ref[0, :, ln] * sch_ref[0, :, ln], cws_ref)
    for j in range(hb):
        lanes = slice(j * head_dim, (j + 1) * head_dim)
        ob_ref[0, :, lanes] = (scb_ref[0, :, lanes] * ysc[j]).astype(ob_ref.dtype)

    rhs = [jnp.concatenate([vh[j][rows[c]] * b_, kb_ * eg], axis=1)
           for (j, c), b_, kb_, eg in zip(probs, bb, kb, e_g)]
    sol = [r + _dot(e.astype(BF16), r.astype(BF16)) for r, e in zip(rhs, e_inv)]
    u = [t[:, :head_dim] for t in sol]
    wq = [jnp.concatenate([t[:, head_dim:], q * eg], axis=0).astype(BF16)
          for t, q, eg in zip(sol, qc, e_g)]
    k_dec = [(k * jnp.exp(gl - g)).astype(BF16) for k, gl, g in zip(kc, g_last, g_c)]
    e_last = [jnp.exp(gl) for gl in g_last]

    z = z_ref[0]
    ng = ng_ref[...]
    s = [s_ref[j] for j in range(hb)]
    for c in range(n_chunks):
        idx = [j * n_chunks + c for j in range(hb)]
        ws = [_dot(wq[i], s[j].astype(BF16)) for j, i in enumerate(idx)]
        v_bf = [(u[i] - ws[j][:c_len]).astype(BF16) for j, i in enumerate(idx)]
        o = [ws[j][c_len:] + _dot(attn[i], v_bf[j]) for j, i in enumerate(idx)]
        s = [s[j] * e_last[i] + _dot_tn(k_dec[i], v_bf[j]) for j, i in enumerate(idx)]
        for j in range(hb):
            lanes = slice(j * head_dim, (j + 1) * head_dim)
            zc = z[rows[c], lanes]
            o_ref[0, rows[c], lanes] = (_rms(o[j], ng) * _silu(zc)).astype(o_ref.dtype)
    for j in range(hb):
        s_ref[j] = s[j]


def _gdn(proj3, gates3, gct, norm_g, conv_sc_w, *, n_heads, head_dim, col_q, col_k, col_v, col_z,
         col_b, col_c, col_h, v_width):
    b, t, _ = proj3.shape
    hb = GDN_HEADS_PER_STEP
    gw = hb * head_dim
    tb = _pick_tile(t, 4 * GDN_CHUNK, GDN_CHUNK)
    n_ch = tb // GDN_CHUNK
    steps = t // tb
    taps_sc = conv_sc_w.shape[0]
    qb, kb_, vb, zb = col_q // gw, col_k // gw, col_v // gw, col_z // gw
    sb, sc, sh = col_b // gw, col_c // gw, col_h // gw
    blocks = 7 * tb * gw * 4 + 2 * tb * GATE_LANES * 4 + 8 * gw * 4 + 2 * tb * gw * 2
    scratch = hb * head_dim * head_dim * 4 + (tb + 8) * gw * 4
    kern = functools.partial(_gdn_kernel, n_heads=n_heads, head_dim=head_dim)
    out = jax.ShapeDtypeStruct((b, t, v_width), BF16)
    return pl.pallas_call(
        kern,
        grid=(b, n_heads // hb, steps),
        in_specs=[
            pl.BlockSpec((1, tb, gw), lambda i, h, s: (i, s, qb + h)),
            pl.BlockSpec((1, tb, gw), lambda i, h, s: (i, s, kb_ + h)),
            pl.BlockSpec((1, tb, gw), lambda i, h, s: (i, s, vb + h)),
            pl.BlockSpec((1, tb, gw), lambda i, h, s: (i, s, zb + h)),
            pl.BlockSpec((1, tb, GATE_LANES), lambda i, h, s: (i, s, 0)),
            pl.BlockSpec((n_ch, GATE_LANES, GDN_CHUNK), lambda i, h, s: (i * steps + s, 0, 0)),
            pl.BlockSpec((1, head_dim), lambda i, h, s: (0, 0)),
            pl.BlockSpec((1, tb, gw), lambda i, h, s: (i, s, sb + h)),
            pl.BlockSpec((1, tb, gw), lambda i, h, s: (i, s, sc + h)),
            pl.BlockSpec((1, tb, gw), lambda i, h, s: (i, s, sh + h)),
            pl.BlockSpec((taps_sc, gw), lambda i, h, s: (0, h)),
        ],
        out_specs=[pl.BlockSpec((1, tb, gw), lambda i, h, s: (i, s, h))] * 2,
        out_shape=[out, out],
        scratch_shapes=[
            pltpu.VMEM((hb, head_dim, head_dim), F32),
            pltpu.VMEM((hb, tb + 8, head_dim), F32),
        ],
        compiler_params=pltpu.CompilerParams(
            dimension_semantics=("parallel", "parallel", "arbitrary"),
            vmem_limit_bytes=_vmem_limit(blocks, scratch, 16 << 20)),
        name="gdn",
    )(proj3, proj3, proj3, proj3, gates3, gct, norm_g, proj3, proj3, proj3, conv_sc_w)


def _merge_kernel(a_ref, b_ref, wa_ref, wb_ref, ga_ref, gb_ref, o_ref):
    pa = _dot(a_ref[...], wa_ref[...])
    pb = _dot(b_ref[...], wb_ref[...])
    o_ref[...] = (_sigmoid(ga_ref[...]) * pa + _sigmoid(gb_ref[...]) * pb).astype(o_ref.dtype)


def _merge(o_a, o_b, wa, wb, proj2, *, col_ga, col_gb):
    m, ka = o_a.shape
    kb_ = o_b.shape[1]
    d = wa.shape[1]
    tm = _pick_tile(m, 1024, 16)
    tn = _pick_tile(d, 512, LANES)
    ga0, gb0 = col_ga // tn, col_gb // tn
    blocks = tm * (ka + kb_) * 2 + (ka + kb_) * tn * 2 + 2 * tm * tn * 4 + tm * tn * 2
    return pl.pallas_call(
        _merge_kernel,
        grid=(m // tm, d // tn),
        in_specs=[
            pl.BlockSpec((tm, ka), lambda i, j: (i, 0)),
            pl.BlockSpec((tm, kb_), lambda i, j: (i, 0)),
            pl.BlockSpec((ka, tn), lambda i, j: (0, j)),
            pl.BlockSpec((kb_, tn), lambda i, j: (0, j)),
            pl.BlockSpec((tm, tn), lambda i, j: (i, ga0 + j)),
            pl.BlockSpec((tm, tn), lambda i, j: (i, gb0 + j)),
        ],
        out_specs=pl.BlockSpec((tm, tn), lambda i, j: (i, j)),
        out_shape=jax.ShapeDtypeStruct((m, d), BF16),
        compiler_params=pltpu.CompilerParams(
            dimension_semantics=("parallel", "parallel"),
            vmem_limit_bytes=_vmem_limit(blocks, 0, 4 * tm * tn * 4)),
        name="merge",
    )(o_a, o_b, wa, wb, proj2, proj2)


def _outproj_kernel(m_ref, w_ref, x_ref, g_ref, h_ref, hn_ref):
    h = x_ref[...] + _dot(m_ref[...], w_ref[...])
    h_ref[...] = h
    hn_ref[...] = _rms(h, g_ref[...]).astype(hn_ref.dtype)


def _outproj(merged, w_out, x2, g):
    m, d = x2.shape
    k = merged.shape[1]
    tm = _pick_tile(m, 512, 16)
    blocks = tm * k * 2 + k * d * 2 + 2 * tm * d * 4 + tm * d * 2 + d * 4
    return pl.pallas_call(
        _outproj_kernel,
        grid=(m // tm,),
        in_specs=[
            pl.BlockSpec((tm, k), lambda i: (i, 0)),
            pl.BlockSpec((k, d), lambda i: (0, 0)),
            pl.BlockSpec((tm, d), lambda i: (i, 0)),
            pl.BlockSpec((1, d), lambda i: (0, 0)),
        ],
        out_specs=[
            pl.BlockSpec((tm, d), lambda i: (i, 0)),
            pl.BlockSpec((tm, d), lambda i: (i, 0)),
        ],
        out_shape=[
            jax.ShapeDtypeStruct((m, d), F32),
            jax.ShapeDtypeStruct((m, d), BF16),
        ],
        compiler_params=pltpu.CompilerParams(
            dimension_semantics=("parallel",),
            vmem_limit_bytes=_vmem_limit(blocks, 0, 3 * tm * d * 4)),
        name="outproj",
    )(merged, w_out, x2, g)


def _ffn_kernel(hn_ref, wg_ref, wu_ref, wd_ref, h_hbm, g_ref, o_ref, sem, *, final_norm):
    i = pl.program_id(0)
    f = pl.program_id(1)
    tm = o_ref.shape[0]

    def residual_copy():
        return pltpu.make_async_copy(h_hbm.at[pl.ds(pl.multiple_of(i * tm, tm), tm), :], o_ref, sem)

    @pl.when(f == 0)
    def _():
        residual_copy().start()

    hn = hn_ref[...]
    ff = (_silu(_dot(hn, wg_ref[...])) * _dot(hn, wu_ref[...])).astype(BF16)

    @pl.when(f == 0)
    def _():
        residual_copy().wait()

    o_ref[...] += _dot(ff, wd_ref[...])

    if final_norm:
        @pl.when(f == pl.num_programs(1) - 1)
        def _():
            o_ref[...] = _rms(o_ref[...], g_ref[...])


def _ffn(hn, wg, wu, wd, h, g, *, final_norm):
    m, d = h.shape
    dff = wg.shape[1]
    tm = _pick_tile(m, 1024, 16)
    tf = _pick_tile(dff, 512, LANES)
    blocks = tm * d * 2 + 3 * d * tf * 2 + tm * d * 4 + d * 4
    kern = functools.partial(_ffn_kernel, final_norm=final_norm)
    return pl.pallas_call(
        kern,
        grid=(m // tm, dff // tf),
        in_specs=[
            pl.BlockSpec((tm, d), lambda i, f: (i, 0)),
            pl.BlockSpec((d, tf), lambda i, f: (0, f)),
            pl.BlockSpec((d, tf), lambda i, f: (0, f)),
            pl.BlockSpec((tf, d), lambda i, f: (f, 0)),
            pl.BlockSpec(memory_space=pl.ANY),
            pl.BlockSpec((1, d), lambda i, f: (0, 0)),
        ],
        out_specs=pl.BlockSpec((tm, d), lambda i, f: (i, 0)),
        out_shape=jax.ShapeDtypeStruct((m, d), F32),
        scratch_shapes=[pltpu.SemaphoreType.DMA(())],
        compiler_params=pltpu.CompilerParams(
            dimension_semantics=("parallel", "arbitrary"),
            vmem_limit_bytes=_vmem_limit(blocks, 0, 4 * tm * tf * 4 + tm * d * 4)),
        name="ffn",
    )(hn, wg, wu, wd, h, g)


def kernel(x, ln_mix_g, w_in, conv_qkv_w, A_log, dt_bias, gdn_norm_g, w_proj_a, conv_sc_w, w_proj_b,
           w_out, ln_ffn_g, w_gate, w_up, w_down, ln_final_g):
    b, t, d = x.shape
    depth = w_in.shape[0]
    n_heads = A_log.shape[1]
    v_width = w_proj_a.shape[1]
    qk_width = (conv_qkv_w.shape[2] - v_width) // 2
    sc_width = w_proj_b.shape[1]
    head_dim = gdn_norm_g.shape[1]
    assert qk_width == n_heads * head_dim and v_width == n_heads * head_dim
    assert head_dim == LANES and n_heads % GDN_HEADS_PER_STEP == 0 and 2 * n_heads <= GATE_LANES
    assert t % GDN_CHUNK == 0 and sc_width == v_width

    col_q, col_k, col_v = 0, qk_width, 2 * qk_width
    col_z = col_v + v_width
    col_gates = col_z + v_width
    rest = col_gates + 2 * n_heads
    col_b = col_gates
    col_c = col_b + sc_width
    col_h = col_c + sc_width
    col_ga = col_h + sc_width
    col_gb = col_ga + d

    h2 = x.reshape(b * t, d)
    for l in range(depth):
        w_t = jnp.swapaxes(w_in[l], 0, 1)
        w_g_hi, w_g_lo = _gprep(w_t, col_gates=col_gates, gate_cols=rest - col_gates)
        gate_params = jnp.zeros((8, GATE_LANES), F32)
        gate_params = gate_params.at[0, n_heads:2 * n_heads].set(A_log[l])
        gate_params = gate_params.at[1, n_heads:2 * n_heads].set(dt_bias[l])

        xn, gates2, gct = _norm(h2, ln_mix_g[l][None, :], w_g_hi, w_g_lo, gate_params, n_heads=n_heads)
        later = [w_proj_a[l], w_proj_b[l], w_out[l], w_gate[l], w_up[l], w_down[l]]
        proj2, (wa_bf, wb_bf, wout_bf, wgate_bf, wup_bf, wdown_bf) = _inproj(
            xn, w_t, conv_qkv_w[l], later, col_gates=col_gates, gate_cols=rest - col_gates, seq_len=t)
        proj3 = proj2.reshape(b, t, -1)
        o_a, o_b = _gdn(proj3, gates2.reshape(b, t, GATE_LANES), gct,
                        gdn_norm_g[l][None, :], conv_sc_w[l], n_heads=n_heads, head_dim=head_dim,
                        col_q=col_q, col_k=col_k, col_v=col_v, col_z=col_z,
                        col_b=col_b, col_c=col_c, col_h=col_h, v_width=v_width)
        merged = _merge(o_a.reshape(b * t, v_width), o_b.reshape(b * t, sc_width), wa_bf, wb_bf, proj2,
                        col_ga=col_ga, col_gb=col_gb)
        h_mid, hn = _outproj(merged, wout_bf, h2, ln_ffn_g[l][None, :])
        last = l == depth - 1
        g_last = ln_final_g[None, :] if last else jnp.ones((1, d), F32)
        h2 = _ffn(hn, wgate_bf, wup_bf, wdown_bf, h_mid, g_last, final_norm=last)
    return h2.reshape(b, t, d)
```

```python
import functools
import math

import jax
import jax.numpy as jnp
from jax import lax
from jax.experimental import pallas as pl
from jax.experimental.pallas import tpu as pltpu

EPS = 1e-6
F32 = jnp.float32
BF16 = jnp.bfloat16

LANES = 128
MXU_COLS = 256
V7X_VMEM_CAP_BYTES = 56 * 1024 * 1024
GDN_CHUNK = 128
GDN_HEADS_PER_STEP = 8
GATE_LANES = LANES


def _vmem_limit(block_bytes, scratch_bytes=0, temp_bytes=0):
    need = 2 * block_bytes + scratch_bytes + temp_bytes + (4 << 20)
    return int(min(V7X_VMEM_CAP_BYTES, max(need, 16 << 20)))


def _pick_tile(n, target, align):
    t = min(n, target)
    t -= t % align
    while t >= align:
        if n % t == 0:
            return t
        t -= align
    return n


def _sigmoid(x):
    return 0.5 * jnp.tanh(0.5 * x) + 0.5


def _silu(x):
    h = 0.5 * x
    return h + h * jnp.tanh(h)


def _dot(a, b):
    return jnp.dot(a, b, preferred_element_type=F32)


def _dot_nt(a, b):
    return lax.dot_general(a, b, (((1,), (1,)), ((), ())), preferred_element_type=F32)


def _dot_tn(a, b):
    return lax.dot_general(a, b, (((0,), (0,)), ((), ())), preferred_element_type=F32)


def _rms(x, g):
    return x * lax.rsqrt(jnp.mean(x * x, axis=-1, keepdims=True) + EPS) * g


def _gprep_kernel(g_ref, hi_ref, lo_ref):
    g = g_ref[...]
    pad = jnp.zeros((GATE_LANES - g.shape[0], g.shape[1]), F32)
    w = jnp.concatenate([g, pad], axis=0).T
    hi = w.astype(BF16)
    hi_ref[...] = hi
    lo_ref[...] = (w - hi.astype(F32)).astype(BF16)


def _gprep(w_t, *, col_gates, gate_cols):
    d = w_t.shape[1]
    out = jax.ShapeDtypeStruct((d, GATE_LANES), BF16)
    return pl.pallas_call(
        _gprep_kernel,
        grid=(1,),
        in_specs=[pl.BlockSpec((pl.Element(gate_cols), pl.Element(d)), lambda i: (col_gates, 0))],
        out_specs=[pl.BlockSpec((d, GATE_LANES), lambda i: (0, 0))] * 2,
        out_shape=[out, out],
        name="gprep",
    )(w_t)


NORM_ROWS = 2 * GDN_CHUNK


def _norm_kernel(x_ref, g_ref, wg_hi_ref, wg_lo_ref, gp_ref, xn_ref, gates_ref, gct_ref, *, n_heads):
    rows = NORM_ROWS
    ch_per_pass = rows // GDN_CHUNK
    gp = gp_ref[...]
    row_in_chunk = lax.broadcasted_iota(jnp.int32, (rows, GATE_LANES), 0) & (GDN_CHUNK - 1)
    lane = lax.broadcasted_iota(jnp.int32, (rows, GATE_LANES), 1)

    def norm_rows(r, carry):
        r0 = pl.multiple_of(r * rows, rows)
        y = _rms(x_ref[pl.ds(r0, rows), :], g_ref[...])
        hi = y.astype(BF16)
        xn_ref[pl.ds(r0, rows), :] = hi
        lo = (y - hi.astype(F32)).astype(BF16)
        w_hi = wg_hi_ref[...]
        hi_terms = _dot(hi, jnp.concatenate([w_hi, wg_lo_ref[...]], axis=1))
        gates = hi_terms[:, :GATE_LANES] + hi_terms[:, GATE_LANES:] + _dot(lo, w_hi)
        xa = gates + gp[1:2, :]
        softplus = jnp.maximum(xa, 0.0) + jnp.log(1.0 + jnp.exp(-jnp.abs(xa)))
        gcum = -jnp.exp(gp[0:1, :]) * softplus
        shift = 1
        while shift < GDN_CHUNK:
            gcum = gcum + jnp.where(row_in_chunk >= shift, pltpu.roll(gcum, shift, 0), 0.0)
            shift *= 2
        gates_ref[pl.ds(r0, rows), :] = jnp.where(lane < n_heads, _sigmoid(gates), gcum)
        for c in range(ch_per_pass):
            gct_ref[r * ch_per_pass + c] = gcum[c * GDN_CHUNK:(c + 1) * GDN_CHUNK, :].T
        return carry

    lax.fori_loop(0, x_ref.shape[0] // rows, norm_rows, 0)


def _norm(x2, g, wg_hi, wg_lo, gate_params, *, n_heads):
    m, d = x2.shape
    tm = _pick_tile(m, 1024, NORM_ROWS)
    n_ch = tm // GDN_CHUNK
    blocks = tm * d * 6 + 2 * tm * GATE_LANES * 4 + 2 * d * GATE_LANES * 2
    kern = functools.partial(_norm_kernel, n_heads=n_heads)
    return pl.pallas_call(
        kern,
        grid=(m // tm,),
        in_specs=[
            pl.BlockSpec((tm, d), lambda i: (i, 0)),
            pl.BlockSpec((1, d), lambda i: (0, 0)),
            pl.BlockSpec((d, GATE_LANES), lambda i: (0, 0)),
            pl.BlockSpec((d, GATE_LANES), lambda i: (0, 0)),
            pl.BlockSpec((8, GATE_LANES), lambda i: (0, 0)),
        ],
        out_specs=[
            pl.BlockSpec((tm, d), lambda i: (i, 0)),
            pl.BlockSpec((tm, GATE_LANES), lambda i: (i, 0)),
            pl.BlockSpec((n_ch, GATE_LANES, GDN_CHUNK), lambda i: (i, 0, 0)),
        ],
        out_shape=[
            jax.ShapeDtypeStruct((m, d), BF16),
            jax.ShapeDtypeStruct((m, GATE_LANES), F32),
            jax.ShapeDtypeStruct((m // GDN_CHUNK, GATE_LANES, GDN_CHUNK), F32),
        ],
        compiler_params=pltpu.CompilerParams(
            dimension_semantics=("parallel",),
            vmem_limit_bytes=_vmem_limit(blocks, 0, 8 * NORM_ROWS * d * 4)),
        name="norm",
    )(x2, g, wg_hi, wg_lo, gate_params)


def _inproj_kernel(*refs, n_side, n_tiles, n_conv, seq_len):
    xn_ref, wt_ref, cw_ref = refs[:3]
    side_in = refs[3:3 + n_side]
    out_ref = refs[3 + n_side]
    side_out = refs[4 + n_side:4 + 2 * n_side]
    wbf_ref, cbuf_ref, halo_ref = refs[4 + 2 * n_side:]
    jn = pl.program_id(0)
    im = pl.program_id(1)
    tm, tn = out_ref.shape
    lane_tiles = tn // LANES
    prep_cols = wt_ref.shape[0]

    @pl.when(jnp.logical_and(jn == 0, im == 0))
    def _():
        halo_ref[...] = jnp.zeros_like(halo_ref)

    def side_jobs():
        w_t = wt_ref[...].T.astype(BF16)
        slot = (jn % 2) * lane_tiles + im * (prep_cols // LANES)
        for c in range(prep_cols // LANES):
            wbf_ref[slot + c] = w_t[:, c * LANES:(c + 1) * LANES]
        for s_in, s_out in zip(side_in, side_out):
            s_out[...] = s_in[...].astype(s_out.dtype)

    def weights():
        slot = ((jn + 1) % 2) * lane_tiles
        return jnp.concatenate([wbf_ref[slot + lt] for lt in range(lane_tiles)], axis=1)

    @pl.when(jn == 0)
    def _():
        side_jobs()

    @pl.when(jn > n_conv)
    def _():
        side_jobs()
        out_ref[...] = _dot(xn_ref[...], weights())

    @pl.when(jnp.logical_and(jn >= 1, jn <= n_conv))
    def _():
        side_jobs()
        cw = cw_ref[...]
        taps = cw.shape[0]
        starts_seq = (im * tm) % seq_len == 0
        acc = _dot(xn_ref[...], weights())
        for lt in range(lane_tiles):
            lanes = slice(lt * LANES, (lt + 1) * LANES)
            x = acc[:, lanes]
            cbuf_ref[lt, pl.ds(0, 8), :] = jnp.where(starts_seq, 0.0, halo_ref[lt])
            cbuf_ref[lt, pl.ds(8, tm), :] = x
            y = x * cw[taps - 1:taps, lanes]
            for s in range(1, taps):
                y = y + cbuf_ref[lt, pl.ds(8 - s, tm), :] * cw[taps - 1 - s:taps - s, lanes]
            halo_ref[lt] = x[tm - 8:tm, :]
            out_ref[:, lanes] = _silu(y)


def _slab_rows(n_rows, steps):
    for r in range(16, n_rows + 1, 16):
        if n_rows % r == 0 and n_rows // r <= steps:
            return r
    return n_rows


def _inproj(xn, w_t, conv_w, side, *, col_gates, gate_cols, seq_len):
    m, d = xn.shape
    n = w_t.shape[0] - gate_cols
    taps, conv_cols = conv_w.shape
    tm = _pick_tile(seq_len, 1024, 16)
    tn = _pick_tile(math.gcd(math.gcd(n, conv_cols), col_gates), 1024, LANES)
    n_m = m // tm
    assert m % tm == 0 and n % tn == 0 and taps <= 8 and gate_cols % 8 == 0
    assert tn % n_m == 0 and (tn // n_m) % LANES == 0
    n_tiles = n // tn
    n_conv = conv_cols // tn
    n_plain = col_gates // tn
    steps = (n_tiles + 1) * n_m
    side_specs, side_shapes, side_bytes = [], [], 0
    for w in side:
        r = _slab_rows(w.shape[0], steps)
        last = w.shape[0] // r - 1
        side_specs.append(pl.BlockSpec((r, w.shape[1]),
                                       lambda j, i, last=last: (jnp.minimum(j * n_m + i, last), 0)))
        side_shapes.append(jax.ShapeDtypeStruct(w.shape, BF16))
        side_bytes += r * w.shape[1] * 6

    prep_cols = tn // n_m

    def weight_rows(j, i):
        tile = jnp.minimum(j, n_tiles - 1)
        start = tile * tn + jnp.where(tile >= n_plain, gate_cols, 0) + i * prep_cols
        return pl.multiple_of(start, 8), 0

    blocks = tm * d * 2 + prep_cols * d * 4 + tm * tn * 4 + 8 * tn * 4 + side_bytes
    scratch = 2 * d * tn * 2 + (tn // LANES) * (tm + 16) * LANES * 4
    kern = functools.partial(_inproj_kernel, n_side=len(side), n_tiles=n_tiles, n_conv=n_conv,
                             seq_len=seq_len)
    outs = pl.pallas_call(
        kern,
        grid=(n_tiles + 1, n_m),
        in_specs=[
            pl.BlockSpec((tm, d), lambda j, i: (jnp.where(j == 0, 0, i), 0)),
            pl.BlockSpec((pl.Element(prep_cols), pl.Element(d)), weight_rows),
            pl.BlockSpec((taps, tn), lambda j, i: (0, jnp.clip(j - 1, 0, n_conv - 1))),
        ] + side_specs,
        out_specs=[
            pl.BlockSpec((tm, tn), lambda j, i: (jnp.where(j == 0, 0, i), jnp.maximum(j - 1, 0))),
        ] + side_specs,
        out_shape=[jax.ShapeDtypeStruct((m, n), F32)] + side_shapes,
        scratch_shapes=[
            pltpu.VMEM((2 * (tn // LANES), d, LANES), BF16),
            pltpu.VMEM((tn // LANES, tm + 8, LANES), F32),
            pltpu.VMEM((tn // LANES, 8, LANES), F32),
        ],
        compiler_params=pltpu.CompilerParams(
            dimension_semantics=("arbitrary", "arbitrary"),
            vmem_limit_bytes=_vmem_limit(blocks, scratch, 2 * tm * tn * 4)),
        name="inproj",
    )(xn, w_t, conv_w, *side)
    return outs[0], outs[1:]


def _inv_unit_lower_minus_eye(a_list, rowi, coli):
    n = GDN_CHUNK
    diag16 = (rowi >> 4) == (coli >> 4)
    x = [jnp.where(diag16, a, 0.0) for a in a_list]
    e = [-t for t in x]
    xb = [t.astype(BF16) for t in x]
    x = [_dot(t, t) for t in xb]
    for step in range(3):
        xb = [t.astype(BF16) for t in x]
        if step < 2:
            both = [_dot(jnp.concatenate([ei.astype(BF16), xi], axis=0), xi) for ei, xi in zip(e, xb)]
            ex = [t[:n] for t in both]
            x_next = [t[n:] for t in both]
        else:
            ex = [_dot(ei.astype(BF16), xi) for ei, xi in zip(e, xb)]
            x_next = x
        e = [ei + xi + exi for ei, xi, exi in zip(e, x, ex)]
        x = x_next

    half = 16
    while half < n:
        rows = [slice(r, r + half) for r in range(half, n, 2 * half)]
        pick = lambda t: jnp.concatenate([t[r] for r in rows], axis=0)
        p = lax.broadcasted_iota(jnp.int32, (n // 2, n), 0)
        ci = lax.broadcasted_iota(jnp.int32, (n // 2, n), 1)
        ri = (((p // half) * 2 + 1) * half) + (p % half)
        off = jnp.logical_and((ri >> half.bit_length()) == (ci >> half.bit_length()), ci < (ri & -half))
        eb = [t.astype(BF16) for t in e]
        y = [jnp.where(off, pick(a), 0.0) for a in a_list]
        z = [yi + _dot(yi.astype(BF16), ei) for yi, ei in zip(y, eb)]
        zero = jnp.zeros((half, n), F32)
        z_full = []
        for zi in z:
            pieces = []
            for k in range(len(rows)):
                pieces += [zero, zi[k * half:(k + 1) * half]]
            z_full.append(jnp.concatenate(pieces, axis=0).astype(BF16))
        ez = [_dot(pick(ei).astype(BF16), zf) for ei, zf in zip(e, z_full)]
        new = [pick(ei) - zi - ezi for ei, zi, ezi in zip(e, z, ez)]
        merged = []
        for ei, ni in zip(e, new):
            pieces = []
            for k, r in enumerate(rows):
                pieces += [ei[r.start - half:r.start], ni[k * half:(k + 1) * half]]
            merged.append(jnp.concatenate(pieces, axis=0))
        e = merged
        half *= 2
    return e


def _gdn_kernel(q_ref, k_ref, v_ref, z_ref, gates_ref, gct_ref, ng_ref, scb_ref, scc_ref, sch_ref,
                cws_ref, o_ref, ob_ref, s_ref, halo_ref, *, n_heads, head_dim):
    c_len = GDN_CHUNK
    hb = GDN_HEADS_PER_STEP
    tb = q_ref.shape[1]
    n_chunks = tb // c_len
    hg = pl.program_id(1)

    @pl.when(pl.program_id(2) == 0)
    def _():
        s_ref[...] = jnp.zeros_like(s_ref)
        halo_ref[:, 0:8, :] = jnp.zeros((hb, 8, head_dim), F32)

    gates = gates_ref[0]
    gate_lane = lax.broadcasted_iota(jnp.int32, (tb, GATE_LANES), 1)

    def causal_conv(load_x, cw_ref):
        cw = cw_ref[...]
        taps = cw.shape[0]
        out = []
        for j in range(hb):
            lanes = slice(j * head_dim, (j + 1) * head_dim)
            x = load_x(lanes)
            halo_ref[j, pl.ds(8, tb), :] = x
            y = x * cw[taps - 1:taps, lanes]
            for s in range(1, taps):
                y = y + halo_ref[j, pl.ds(8 - s, tb), :] * cw[taps - 1 - s:taps - s, lanes]
            halo_ref[j, pl.ds(0, 8), :] = x[tb - 8:tb, :]
            out.append(y)
        return out

    head_lanes = [slice(j * head_dim, (j + 1) * head_dim) for j in range(hb)]
    yq = [q_ref[0, :, ln] for ln in head_lanes]
    yk = [k_ref[0, :, ln] for ln in head_lanes]
    yv = [v_ref[0, :, ln] for ln in head_lanes]

    rowi = lax.broadcasted_iota(jnp.int32, (c_len, c_len), 0)
    coli = lax.broadcasted_iota(jnp.int32, (c_len, c_len), 1)
    incl = rowi >= coli
    strict = rowi > coli

    qh, kh, vh, beta, gcol = [], [], [], [], []
    for j in range(hb):
        head = hg * hb + j
        qj = yq[j]
        kj = yk[j]
        qh.append(qj * (lax.rsqrt(jnp.sum(qj * qj, axis=1, keepdims=True) + EPS) * head_dim ** -0.5))
        kh.append(kj * lax.rsqrt(jnp.sum(kj * kj, axis=1, keepdims=True) + EPS))
        vh.append(yv[j])
        beta.append(jnp.sum(jnp.where(gate_lane == head, gates, 0.0), axis=1, keepdims=True))
        gcol.append(jnp.sum(jnp.where(gate_lane == head + n_heads, gates, 0.0), axis=1, keepdims=True))

    probs = [(j, c) for j in range(hb) for c in range(n_chunks)]
    rows = {c: slice(c * c_len, (c + 1) * c_len) for c in range(n_chunks)}
    g_c = [gcol[j][rows[c]] for j, c in probs]
    g_r = [gct_ref[c, pl.ds(hg * hb + j + n_heads, 1), :] for j, c in probs]
    g_last = [g[c_len - 1:c_len, :] for g in g_c]
    decay = [jnp.where(incl, jnp.exp(jnp.where(incl, gc - gr, 0.0)), 0.0) for gc, gr in zip(g_c, g_r)]
    e_g = [jnp.exp(g) for g in g_c]
    kc = [kh[j][rows[c]] for j, c in probs]
    qc = [qh[j][rows[c]] for j, c in probs]
    bb = [beta[j][rows[c]] for j, c in probs]
    kb = [k * b_ for k, b_ in zip(kc, bb)]
    kbf = [k.astype(BF16) for k in kc]
    kk = [_dot_nt(a.astype(BF16), b_) for a, b_ in zip(kb, kbf)]
    qk = [_dot_nt(a.astype(BF16), b_) for a, b_ in zip(qc, kbf)]
    a_low = [jnp.where(strict, t * d_, 0.0) for t, d_ in zip(kk, decay)]
    attn = [(t * d_).astype(BF16) for t, d_ in zip(qk, decay)]
    e_inv = _inv_unit_lower_minus_eye(a_low, rowi, coli)

    ysc = causal_conv(lambda ln: scc_ref[0, :, ln] * sch_ref[0, :, ln], cws_ref)
    for j in range(hb):
        lanes = slice(j * head_dim, (j + 1) * head_dim)
        ob_ref[0, :, lanes] = (scb_ref[0, :, lanes] * ysc[j]).astype(ob_ref.dtype)

    rhs = [jnp.concatenate([vh[j][rows[c]] * b_, kb_ * eg], axis=1)
           for (j, c), b_, kb_, eg in zip(probs, bb, kb, e_g)]
    sol = [r + _dot(e.astype(BF16), r.astype(BF16)) for r, e in zip(rhs, e_inv)]
    u = [t[:, :head_dim] for t in sol]
    wq = [jnp.concatenate([t[:, head_dim:], q * eg], axis=0).astype(BF16)
          for t, q, eg in zip(sol, qc, e_g)]
    k_dec = [(k * jnp.exp(gl - g)).astype(BF16) for k, gl, g in zip(kc, g_last, g_c)]
    e_last = [jnp.exp(gl) for gl in g_last]

    z = z_ref[0]
    ng = ng_ref[...]
    s = [s_ref[j] for j in range(hb)]
    for c in range(n_chunks):
        idx = [j * n_chunks + c for j in range(hb)]
        ws = [_dot(wq[i], s[j].astype(BF16)) for j, i in enumerate(idx)]
        v_bf = [(u[i] - ws[j][:c_len]).astype(BF16) for j, i in enumerate(idx)]
        o = [ws[j][c_len:] + _dot(attn[i], v_bf[j]) for j, i in enumerate(idx)]
        s = [s[j] * e_last[i] + _dot_tn(k_dec[i], v_bf[j]) for j, i in enumerate(idx)]
        for j in range(hb):
            lanes = slice(j * head_dim, (j + 1) * head_dim)
            zc = z[rows[c], lanes]
            o_ref[0, rows[c], lanes] = (_rms(o[j], ng) * _silu(zc)).astype(o_ref.dtype)
    for j in range(hb):
        s_ref[j] = s[j]


def _gdn(proj3, gates3, gct, norm_g, conv_sc_w, *, n_heads, head_dim, col_q, col_k, col_v, col_z,
         col_b, col_c, col_h, v_width):
    b, t, _ = proj3.shape
    hb = GDN_HEADS_PER_STEP
    gw = hb * head_dim
    tb = _pick_tile(t, 4 * GDN_CHUNK, GDN_CHUNK)
    n_ch = tb // GDN_CHUNK
    steps = t // tb
    taps_sc = conv_sc_w.shape[0]
    qb, kb_, vb, zb = col_q // gw, col_k // gw, col_v // gw, col_z // gw
    sb, sc, sh = col_b // gw, col_c // gw, col_h // gw
    blocks = 7 * tb * gw * 4 + 2 * tb * GATE_LANES * 4 + 8 * gw * 4 + 2 * tb * gw * 2
    scratch = hb * head_dim * head_dim * 4 + (tb + 8) * gw * 4
    kern = functools.partial(_gdn_kernel, n_heads=n_heads, head_dim=head_dim)
    out = jax.ShapeDtypeStruct((b, t, v_width), BF16)
    return pl.pallas_call(
        kern,
        grid=(b, n_heads // hb, steps),
        in_specs=[
            pl.BlockSpec((1, tb, gw), lambda i, h, s: (i, s, qb + h)),
            pl.BlockSpec((1, tb, gw), lambda i, h, s: (i, s, kb_ + h)),
            pl.BlockSpec((1, tb, gw), lambda i, h, s: (i, s, vb + h)),
            pl.BlockSpec((1, tb, gw), lambda i, h, s: (i, s, zb + h)),
            pl.BlockSpec((1, tb, GATE_LANES), lambda i, h, s: (i, s, 0)),
            pl.BlockSpec((n_ch, GATE_LANES, GDN_CHUNK), lambda i, h, s: (i * steps + s, 0, 0)),
            pl.BlockSpec((1, head_dim), lambda i, h, s: (0, 0)),
            pl.BlockSpec((1, tb, gw), lambda i, h, s: (i, s, sb + h)),
            pl.BlockSpec((1, tb, gw), lambda i, h, s: (i, s, sc + h)),
            pl.BlockSpec((1, tb, gw), lambda i, h, s: (i, s, sh + h)),
            pl.BlockSpec((taps_sc, gw), lambda i, h, s: (0, h)),
        ],
        out_specs=[pl.BlockSpec((1, tb, gw), lambda i, h, s: (i, s, h))] * 2,
        out_shape=[out, out],
        scratch_shapes=[
            pltpu.VMEM((hb, head_dim, head_dim), F32),
            pltpu.VMEM((hb, tb + 8, head_dim), F32),
        ],
        compiler_params=pltpu.CompilerParams(
            dimension_semantics=("parallel", "parallel", "arbitrary"),
            vmem_limit_bytes=_vmem_limit(blocks, scratch, 16 << 20)),
        name="gdn",
    )(proj3, proj3, proj3, proj3, gates3, gct, norm_g, proj3, proj3, proj3, conv_sc_w)


def _merge_kernel(a_ref, b_ref, wa_ref, wb_ref, ga_ref, gb_ref, o_ref):
    pa = _dot(a_ref[...], wa_ref[...])
    pb = _dot(b_ref[...], wb_ref[...])
    o_ref[...] = (_sigmoid(ga_ref[...]) * pa + _sigmoid(gb_ref[...]) * pb).astype(o_ref.dtype)


def _merge(o_a, o_b, wa, wb, proj2, *, col_ga, col_gb):
    m, ka = o_a.shape
    kb_ = o_b.shape[1]
    d = wa.shape[1]
    tm = _pick_tile(m, 1024, 16)
    tn = _pick_tile(d, 512, LANES)
    ga0, gb0 = col_ga // tn, col_gb // tn
    blocks = tm * (ka + kb_) * 2 + (ka + kb_) * tn * 2 + 2 * tm * tn * 4 + tm * tn * 2
    return pl.pallas_call(
        _merge_kernel,
        grid=(m // tm, d // tn),
        in_specs=[
            pl.BlockSpec((tm, ka), lambda i, j: (i, 0)),
            pl.BlockSpec((tm, kb_), lambda i, j: (i, 0)),
            pl.BlockSpec((ka, tn), lambda i, j: (0, j)),
            pl.BlockSpec((kb_, tn), lambda i, j: (0, j)),
            pl.BlockSpec((tm, tn), lambda i, j: (i, ga0 + j)),
            pl.BlockSpec((tm, tn), lambda i, j: (i, gb0 + j)),
        ],
        out_specs=pl.BlockSpec((tm, tn), lambda i, j: (i, j)),
        out_shape=jax.ShapeDtypeStruct((m, d), BF16),
        compiler_params=pltpu.CompilerParams(
            dimension_semantics=("parallel", "parallel"),
            vmem_limit_bytes=_vmem_limit(blocks, 0, 4 * tm * tn * 4)),
        name="merge",
    )(o_a, o_b, wa, wb, proj2, proj2)


def _outproj_kernel(m_ref, w_ref, x_ref, g_ref, h_ref, hn_ref):
    h = x_ref[...] + _dot(m_ref[...], w_ref[...])
    h_ref[...] = h
    hn_ref[...] = _rms(h, g_ref[...]).astype(hn_ref.dtype)


def _outproj(merged, w_out, x2, g):
    m, d = x2.shape
    k = merged.shape[1]
    tm = _pick_tile(m, 512, 16)
    blocks = tm * k * 2 + k * d * 2 + 2 * tm * d * 4 + tm * d * 2 + d * 4
    return pl.pallas_call(
        _outproj_kernel,
        grid=(m // tm,),
        in_specs=[
            pl.BlockSpec((tm, k), lambda i: (i, 0)),
            pl.BlockSpec((k, d), lambda i: (0, 0)),
            pl.BlockSpec((tm, d), lambda i: (i, 0)),
            pl.BlockSpec((1, d), lambda i: (0, 0)),
        ],
        out_specs=[
            pl.BlockSpec((tm, d), lambda i: (i, 0)),
            pl.BlockSpec((tm, d), lambda i: (i, 0)),
        ],
        out_shape=[
            jax.ShapeDtypeStruct((m, d), F32),
            jax.ShapeDtypeStruct((m, d), BF16),
        ],
        compiler_params=pltpu.CompilerParams(
            dimension_semantics=("parallel",),
            vmem_limit_bytes=_vmem_limit(blocks, 0, 3 * tm * d * 4)),
        name="outproj",
    )(merged, w_out, x2, g)


def _ffn_kernel(hn_ref, wg_ref, wu_ref, wd_ref, h_hbm, g_ref, o_ref, sem, *, final_norm):
    i = pl.program_id(0)
    f = pl.program_id(1)
    tm = o_ref.shape[0]

    def residual_copy():
        return pltpu.make_async_copy(h_hbm.at[pl.ds(pl.multiple_of(i * tm, tm), tm), :], o_ref, sem)

    @pl.when(f == 0)
    def _():
        residual_copy().start()

    hn = hn_ref[...]
    ff = (_silu(_dot(hn, wg_ref[...])) * _dot(hn, wu_ref[...])).astype(BF16)

    @pl.when(f == 0)
    def _():
        residual_copy().wait()

    o_ref[...] += _dot(ff, wd_ref[...])

    if final_norm:
        @pl.when(f == pl.num_programs(1) - 1)
        def _():
            o_ref[...] = _rms(o_ref[...], g_ref[...])


def _ffn(hn, wg, wu, wd, h, g, *, final_norm):
    m, d = h.shape
    dff = wg.shape[1]
    tm = _pick_tile(m, 1024, 16)
    tf = _pick_tile(dff, 512, LANES)
    blocks = tm * d * 2 + 3 * d * tf * 2 + tm * d * 4 + d * 4
    kern = functools.partial(_ffn_kernel, final_norm=final_norm)
    return pl.pallas_call(
        kern,
        grid=(m // tm, dff // tf),
        in_specs=[
            pl.BlockSpec((tm, d), lambda i, f: (i, 0)),
            pl.BlockSpec((d, tf), lambda i, f: (0, f)),
            pl.BlockSpec((d, tf), lambda i, f: (0, f)),
            pl.BlockSpec((tf, d), lambda i, f: (f, 0)),
            pl.BlockSpec(memory_space=pl.ANY),
            pl.BlockSpec((1, d), lambda i, f: (0, 0)),
        ],
        out_specs=pl.BlockSpec((tm, d), lambda i, f: (i, 0)),
        out_shape=jax.ShapeDtypeStruct((m, d), F32),
        scratch_shapes=[pltpu.SemaphoreType.DMA(())],
        compiler_params=pltpu.CompilerParams(
            dimension_semantics=("parallel", "arbitrary"),
            vmem_limit_bytes=_vmem_limit(blocks, 0, 4 * tm * tf * 4 + tm * d * 4)),
        name="ffn",
    )(hn, wg, wu, wd, h, g)


def kernel(x, ln_mix_g, w_in, conv_qkv_w, A_log, dt_bias, gdn_norm_g, w_proj_a, conv_sc_w, w_proj_b,
           w_out, ln_ffn_g, w_gate, w_up, w_down, ln_final_g):
    b, t, d = x.shape
    depth = w_in.shape[0]
    n_heads = A_log.shape[1]
    v_width = w_proj_a.shape[1]
    qk_width = (conv_qkv_w.shape[2] - v_width) // 2
    sc_width = w_proj_b.shape[1]
    head_dim = gdn_norm_g.shape[1]
    assert qk_width == n_heads * head_dim and v_width == n_heads * head_dim
    assert head_dim == LANES and n_heads % GDN_HEADS_PER_STEP == 0 and 2 * n_heads <= GATE_LANES
    assert t % GDN_CHUNK == 0 and sc_width == v_width

    col_q, col_k, col_v = 0, qk_width, 2 * qk_width
    col_z = col_v + v_width
    col_gates = col_z + v_width
    rest = col_gates + 2 * n_heads
    col_b = col_gates
    col_c = col_b + sc_width
    col_h = col_c + sc_width
    col_ga = col_h + sc_width
    col_gb = col_ga + d

    h2 = x.reshape(b * t, d)
    for l in range(depth):
        w_t = jnp.swapaxes(w_in[l], 0, 1)
        w_g_hi, w_g_lo = _gprep(w_t, col_gates=col_gates, gate_cols=rest - col_gates)
        gate_params = jnp.pad(jnp.stack([A_log[l], dt_bias[l]]),
                              ((0, 6), (n_heads, GATE_LANES - 2 * n_heads)))

        xn, gates2, gct = _norm(h2, ln_mix_g[l][None, :], w_g_hi, w_g_lo, gate_params, n_heads=n_heads)
        later = [w_proj_a[l], w_proj_b[l], w_out[l], w_gate[l], w_up[l], w_down[l]]
        proj2, (wa_bf, wb_bf, wout_bf, wgate_bf, wup_bf, wdown_bf) = _inproj(
            xn, w_t, conv_qkv_w[l], later, col_gates=col_gates, gate_cols=rest - col_gates, seq_len=t)
        proj3 = proj2.reshape(b, t, -1)
        o_a, o_b = _gdn(proj3, gates2.reshape(b, t, GATE_LANES), gct,
                        gdn_norm_g[l][None, :], conv_sc_w[l], n_heads=n_heads, head_dim=head_dim,
                        col_q=col_q, col_k=col_k, col_v=col_v, col_z=col_z,
                        col_b=col_b, col_c=col_c, col_h=col_h, v_width=v_width)
        merged = _merge(o_a.reshape(b * t, v_width), o_b.reshape(b * t, sc_width), wa_bf, wb_bf, proj2,
                        col_ga=col_ga, col_gb=col_gb)
        h_mid, hn = _outproj(merged, wout_bf, h2, ln_ffn_g[l][None, :])
        last = l == depth - 1
        g_last = ln_final_g[None, :] if last else jnp.ones((1, d), F32)
        h2 = _ffn(hn, wgate_bf, wup_bf, wdown_bf, h_mid, g_last, final_norm=last)
    return h2.reshape(b, t, d)
```

```python
import functools
import math

import jax
import jax.numpy as jnp
from jax import lax
from jax.experimental import pallas as pl
from jax.experimental.pallas import tpu as pltpu

EPS = 1e-6
F32 = jnp.float32
BF16 = jnp.bfloat16

LANES = 128
MXU_COLS = 256
V7X_VMEM_CAP_BYTES = 56 * 1024 * 1024
GDN_CHUNK = 128
GDN_HEADS_PER_STEP = 16
GATE_LANES = LANES


def _vmem_limit(block_bytes, scratch_bytes=0, temp_bytes=0):
    need = 2 * block_bytes + scratch_bytes + temp_bytes + (4 << 20)
    return int(min(V7X_VMEM_CAP_BYTES, max(need, 16 << 20)))


def _pick_tile(n, target, align):
    t = min(n, target)
    t -= t % align
    while t >= align:
        if n % t == 0:
            return t
        t -= align
    return n


def _sigmoid(x):
    return 0.5 * jnp.tanh(0.5 * x) + 0.5


def _silu(x):
    h = 0.5 * x
    return h + h * jnp.tanh(h)


def _dot(a, b):
    return jnp.dot(a, b, preferred_element_type=F32)


def _dot_nt(a, b):
    return lax.dot_general(a, b, (((1,), (1,)), ((), ())), preferred_element_type=F32)


def _dot_tn(a, b):
    return lax.dot_general(a, b, (((0,), (0,)), ((), ())), preferred_element_type=F32)


def _rms(x, g):
    return x * lax.rsqrt(jnp.mean(x * x, axis=-1, keepdims=True) + EPS) * g


def _gprep_kernel(g_ref, hi_ref, lo_ref):
    g = g_ref[...]
    pad = jnp.zeros((GATE_LANES - g.shape[0], g.shape[1]), F32)
    w = jnp.concatenate([g, pad], axis=0).T
    hi = w.astype(BF16)
    hi_ref[...] = hi
    lo_ref[...] = (w - hi.astype(F32)).astype(BF16)


def _gprep(w_t, *, col_gates, gate_cols):
    d = w_t.shape[1]
    out = jax.ShapeDtypeStruct((d, GATE_LANES), BF16)
    return pl.pallas_call(
        _gprep_kernel,
        grid=(1,),
        in_specs=[pl.BlockSpec((pl.Element(gate_cols), pl.Element(d)), lambda i: (col_gates, 0))],
        out_specs=[pl.BlockSpec((d, GATE_LANES), lambda i: (0, 0))] * 2,
        out_shape=[out, out],
        name="gprep",
    )(w_t)


NORM_ROWS = 2 * GDN_CHUNK


def _norm_kernel(x_ref, g_ref, wg_hi_ref, wg_lo_ref, gp_ref, xn_ref, gates_ref, gct_ref, *, n_heads):
    rows = NORM_ROWS
    ch_per_pass = rows // GDN_CHUNK
    gp = gp_ref[...]
    row_in_chunk = lax.broadcasted_iota(jnp.int32, (rows, GATE_LANES), 0) & (GDN_CHUNK - 1)
    lane = lax.broadcasted_iota(jnp.int32, (rows, GATE_LANES), 1)

    def norm_rows(r, carry):
        r0 = pl.multiple_of(r * rows, rows)
        y = _rms(x_ref[pl.ds(r0, rows), :], g_ref[...])
        hi = y.astype(BF16)
        xn_ref[pl.ds(r0, rows), :] = hi
        lo = (y - hi.astype(F32)).astype(BF16)
        w_hi = wg_hi_ref[...]
        hi_terms = _dot(hi, jnp.concatenate([w_hi, wg_lo_ref[...]], axis=1))
        gates = hi_terms[:, :GATE_LANES] + hi_terms[:, GATE_LANES:] + _dot(lo, w_hi)
        xa = gates + gp[1:2, :]
        softplus = jnp.maximum(xa, 0.0) + jnp.log(1.0 + jnp.exp(-jnp.abs(xa)))
        gcum = -jnp.exp(gp[0:1, :]) * softplus
        shift = 1
        while shift < GDN_CHUNK:
            gcum = gcum + jnp.where(row_in_chunk >= shift, pltpu.roll(gcum, shift, 0), 0.0)
            shift *= 2
        gates_ref[pl.ds(r0, rows), :] = jnp.where(lane < n_heads, _sigmoid(gates), gcum)
        for c in range(ch_per_pass):
            gct_ref[r * ch_per_pass + c] = gcum[c * GDN_CHUNK:(c + 1) * GDN_CHUNK, :].T
        return carry

    lax.fori_loop(0, x_ref.shape[0] // rows, norm_rows, 0)


def _norm(x2, g, wg_hi, wg_lo, gate_params, *, n_heads):
    m, d = x2.shape
    tm = _pick_tile(m, 1024, NORM_ROWS)
    n_ch = tm // GDN_CHUNK
    blocks = tm * d * 6 + 2 * tm * GATE_LANES * 4 + 2 * d * GATE_LANES * 2
    kern = functools.partial(_norm_kernel, n_heads=n_heads)
    return pl.pallas_call(
        kern,
        grid=(m // tm,),
        in_specs=[
            pl.BlockSpec((tm, d), lambda i: (i, 0)),
            pl.BlockSpec((1, d), lambda i: (0, 0)),
            pl.BlockSpec((d, GATE_LANES), lambda i: (0, 0)),
            pl.BlockSpec((d, GATE_LANES), lambda i: (0, 0)),
            pl.BlockSpec((8, GATE_LANES), lambda i: (0, 0)),
        ],
        out_specs=[
            pl.BlockSpec((tm, d), lambda i: (i, 0)),
            pl.BlockSpec((tm, GATE_LANES), lambda i: (i, 0)),
            pl.BlockSpec((n_ch, GATE_LANES, GDN_CHUNK), lambda i: (i, 0, 0)),
        ],
        out_shape=[
            jax.ShapeDtypeStruct((m, d), BF16),
            jax.ShapeDtypeStruct((m, GATE_LANES), F32),
            jax.ShapeDtypeStruct((m // GDN_CHUNK, GATE_LANES, GDN_CHUNK), F32),
        ],
        compiler_params=pltpu.CompilerParams(
            dimension_semantics=("parallel",),
            vmem_limit_bytes=_vmem_limit(blocks, 0, 8 * NORM_ROWS * d * 4)),
        name="norm",
    )(x2, g, wg_hi, wg_lo, gate_params)


def _inproj_kernel(*refs, n_side, n_tiles, n_conv, seq_len):
    xn_ref, wt_ref, cw_ref = refs[:3]
    side_in = refs[3:3 + n_side]
    out_ref = refs[3 + n_side]
    side_out = refs[4 + n_side:4 + 2 * n_side]
    wbf_ref, cbuf_ref, halo_ref = refs[4 + 2 * n_side:]
    jn = pl.program_id(0)
    im = pl.program_id(1)
    tm, tn = out_ref.shape
    lane_tiles = tn // LANES
    prep_cols = wt_ref.shape[0]

    @pl.when(jnp.logical_and(jn == 0, im == 0))
    def _():
        halo_ref[...] = jnp.zeros_like(halo_ref)

    def side_jobs():
        w_t = wt_ref[...].T.astype(BF16)
        slot = (jn % 2) * lane_tiles + im * (prep_cols // LANES)
        for c in range(prep_cols // LANES):
            wbf_ref[slot + c] = w_t[:, c * LANES:(c + 1) * LANES]
        for s_in, s_out in zip(side_in, side_out):
            s_out[...] = s_in[...].astype(s_out.dtype)

    def weights():
        slot = ((jn + 1) % 2) * lane_tiles
        return jnp.concatenate([wbf_ref[slot + lt] for lt in range(lane_tiles)], axis=1)

    @pl.when(jn == 0)
    def _():
        side_jobs()

    @pl.when(jn > n_conv)
    def _():
        out_ref[...] = _dot(xn_ref[...], weights())
        side_jobs()

    @pl.when(jnp.logical_and(jn >= 1, jn <= n_conv))
    def _():
        cw = cw_ref[...]
        taps = cw.shape[0]
        starts_seq = (im * tm) % seq_len == 0
        acc = _dot(xn_ref[...], weights())
        for lt in range(lane_tiles):
            lanes = slice(lt * LANES, (lt + 1) * LANES)
            x = acc[:, lanes]
            cbuf_ref[lt, pl.ds(0, 8), :] = jnp.where(starts_seq, 0.0, halo_ref[lt])
            cbuf_ref[lt, pl.ds(8, tm), :] = x
            y = x * cw[taps - 1:taps, lanes]
            for s in range(1, taps):
                y = y + cbuf_ref[lt, pl.ds(8 - s, tm), :] * cw[taps - 1 - s:taps - s, lanes]
            halo_ref[lt] = x[tm - 8:tm, :]
            out_ref[:, lanes] = _silu(y)
        side_jobs()


def _slab_rows(n_rows, steps):
    for r in range(16, n_rows + 1, 16):
        if n_rows % r == 0 and n_rows // r <= steps:
            return r
    return n_rows


def _inproj(xn, w_t, conv_w, side, *, col_gates, gate_cols, seq_len):
    m, d = xn.shape
    n = w_t.shape[0] - gate_cols
    taps, conv_cols = conv_w.shape
    tm = _pick_tile(seq_len, 1024, 16)
    tn = _pick_tile(math.gcd(math.gcd(n, conv_cols), col_gates), 1024, LANES)
    n_m = m // tm
    assert m % tm == 0 and n % tn == 0 and taps <= 8 and gate_cols % 8 == 0
    assert tn % n_m == 0 and (tn // n_m) % LANES == 0
    n_tiles = n // tn
    n_conv = conv_cols // tn
    n_plain = col_gates // tn
    steps = (n_tiles + 1) * n_m
    side_specs, side_shapes, side_bytes = [], [], 0
    for w in side:
        r = _slab_rows(w.shape[0], steps)
        last = w.shape[0] // r - 1
        side_specs.append(pl.BlockSpec((r, w.shape[1]),
                                       lambda j, i, last=last: (jnp.minimum(j * n_m + i, last), 0)))
        side_shapes.append(jax.ShapeDtypeStruct(w.shape, BF16))
        side_bytes += r * w.shape[1] * 6

    prep_cols = tn // n_m

    def weight_rows(j, i):
        tile = jnp.minimum(j, n_tiles - 1)
        start = tile * tn + jnp.where(tile >= n_plain, gate_cols, 0) + i * prep_cols
        return pl.multiple_of(start, 8), 0

    blocks = tm * d * 2 + prep_cols * d * 4 + tm * tn * 4 + 8 * tn * 4 + side_bytes
    scratch = 2 * d * tn * 2 + (tn // LANES) * (tm + 16) * LANES * 4
    kern = functools.partial(_inproj_kernel, n_side=len(side), n_tiles=n_tiles, n_conv=n_conv,
                             seq_len=seq_len)
    outs = pl.pallas_call(
        kern,
        grid=(n_tiles + 1, n_m),
        in_specs=[
            pl.BlockSpec((tm, d), lambda j, i: (jnp.where(j == 0, 0, i), 0)),
            pl.BlockSpec((pl.Element(prep_cols), pl.Element(d)), weight_rows),
            pl.BlockSpec((taps, tn), lambda j, i: (0, jnp.clip(j - 1, 0, n_conv - 1))),
        ] + side_specs,
        out_specs=[
            pl.BlockSpec((tm, tn), lambda j, i: (jnp.where(j == 0, 0, i), jnp.maximum(j - 1, 0))),
        ] + side_specs,
        out_shape=[jax.ShapeDtypeStruct((m, n), F32)] + side_shapes,
        scratch_shapes=[
            pltpu.VMEM((2 * (tn // LANES), d, LANES), BF16),
            pltpu.VMEM((tn // LANES, tm + 8, LANES), F32),
            pltpu.VMEM((tn // LANES, 8, LANES), F32),
        ],
        compiler_params=pltpu.CompilerParams(
            dimension_semantics=("arbitrary", "arbitrary"),
            vmem_limit_bytes=_vmem_limit(blocks, scratch, 2 * tm * tn * 4)),
        name="inproj",
    )(xn, w_t, conv_w, *side)
    return outs[0], outs[1:]


def _inv_unit_lower_minus_eye(a_list, rowi, coli):
    n = GDN_CHUNK
    diag16 = (rowi >> 4) == (coli >> 4)
    x = [jnp.where(diag16, a, 0.0) for a in a_list]
    e = [-t for t in x]
    xb = [t.astype(BF16) for t in x]
    x = [_dot(t, t) for t in xb]
    for step in range(3):
        xb = [t.astype(BF16) for t in x]
        if step < 2:
            both = [_dot(jnp.concatenate([ei.astype(BF16), xi], axis=0), xi) for ei, xi in zip(e, xb)]
            ex = [t[:n] for t in both]
            x_next = [t[n:] for t in both]
        else:
            ex = [_dot(ei.astype(BF16), xi) for ei, xi in zip(e, xb)]
            x_next = x
        e = [ei + xi + exi for ei, xi, exi in zip(e, x, ex)]
        x = x_next

    half = 16
    while half < n:
        rows = [slice(r, r + half) for r in range(half, n, 2 * half)]
        pick = lambda t: jnp.concatenate([t[r] for r in rows], axis=0)
        p = lax.broadcasted_iota(jnp.int32, (n // 2, n), 0)
        ci = lax.broadcasted_iota(jnp.int32, (n // 2, n), 1)
        ri = (((p // half) * 2 + 1) * half) + (p % half)
        off = jnp.logical_and((ri >> half.bit_length()) == (ci >> half.bit_length()), ci < (ri & -half))
        eb = [t.astype(BF16) for t in e]
        y = [jnp.where(off, pick(a), 0.0) for a in a_list]
        z = [yi + _dot(yi.astype(BF16), ei) for yi, ei in zip(y, eb)]
        zero = jnp.zeros((half, n), F32)
        z_full = []
        for zi in z:
            pieces = []
            for k in range(len(rows)):
                pieces += [zero, zi[k * half:(k + 1) * half]]
            z_full.append(jnp.concatenate(pieces, axis=0).astype(BF16))
        ez = [_dot(pick(ei).astype(BF16), zf) for ei, zf in zip(e, z_full)]
        new = [pick(ei) - zi - ezi for ei, zi, ezi in zip(e, z, ez)]
        merged = []
        for ei, ni in zip(e, new):
            pieces = []
            for k, r in enumerate(rows):
                pieces += [ei[r.start - half:r.start], ni[k * half:(k + 1) * half]]
            merged.append(jnp.concatenate(pieces, axis=0))
        e = merged
        half *= 2
    return e


def _gdn_kernel(q_ref, k_ref, v_ref, z_ref, gates_ref, gct_ref, ng_ref, scb_ref, scc_ref, sch_ref,
                cws_ref, o_ref, ob_ref, s_ref, halo_ref, *, n_heads, head_dim):
    c_len = GDN_CHUNK
    hb = GDN_HEADS_PER_STEP
    tb = q_ref.shape[1]
    n_chunks = tb // c_len
    hg = pl.program_id(1)

    @pl.when(pl.program_id(2) == 0)
    def _():
        s_ref[...] = jnp.zeros_like(s_ref)
        halo_ref[:, 0:8, :] = jnp.zeros((hb, 8, head_dim), F32)

    gates = gates_ref[0]
    gate_lane = lax.broadcasted_iota(jnp.int32, (tb, GATE_LANES), 1)

    def causal_conv(load_x, cw_ref):
        cw = cw_ref[...]
        taps = cw.shape[0]
        out = []
        for j in range(hb):
            lanes = slice(j * head_dim, (j + 1) * head_dim)
            x = load_x(lanes)
            halo_ref[j, pl.ds(8, tb), :] = x
            y = x * cw[taps - 1:taps, lanes]
            for s in range(1, taps):
                y = y + halo_ref[j, pl.ds(8 - s, tb), :] * cw[taps - 1 - s:taps - s, lanes]
            halo_ref[j, pl.ds(0, 8), :] = x[tb - 8:tb, :]
            out.append(y)
        return out

    head_lanes = [slice(j * head_dim, (j + 1) * head_dim) for j in range(hb)]
    yq = [q_ref[0, :, ln] for ln in head_lanes]
    yk = [k_ref[0, :, ln] for ln in head_lanes]
    yv = [v_ref[0, :, ln] for ln in head_lanes]

    rowi = lax.broadcasted_iota(jnp.int32, (c_len, c_len), 0)
    coli = lax.broadcasted_iota(jnp.int32, (c_len, c_len), 1)
    incl = rowi >= coli
    strict = rowi > coli

    qh, kh, vh, beta, gcol = [], [], [], [], []
    for j in range(hb):
        head = hg * hb + j
        qj = yq[j]
        kj = yk[j]
        qh.append(qj * (lax.rsqrt(jnp.sum(qj * qj, axis=1, keepdims=True) + EPS) * head_dim ** -0.5))
        kh.append(kj * lax.rsqrt(jnp.sum(kj * kj, axis=1, keepdims=True) + EPS))
        vh.append(yv[j])
        beta.append(jnp.sum(jnp.where(gate_lane == head, gates, 0.0), axis=1, keepdims=True))
        gcol.append(jnp.sum(jnp.where(gate_lane == head + n_heads, gates, 0.0), axis=1, keepdims=True))

    probs = [(j, c) for j in range(hb) for c in range(n_chunks)]
    rows = {c: slice(c * c_len, (c + 1) * c_len) for c in range(n_chunks)}
    g_c = [gcol[j][rows[c]] for j, c in probs]
    g_r = [gct_ref[c, pl.ds(hg * hb + j + n_heads, 1), :] for j, c in probs]
    g_last = [g[c_len - 1:c_len, :] for g in g_c]
    decay = [jnp.where(incl, jnp.exp(jnp.where(incl, gc - gr, 0.0)), 0.0) for gc, gr in zip(g_c, g_r)]
    e_g = [jnp.exp(g) for g in g_c]
    kc = [kh[j][rows[c]] for j, c in probs]
    qc = [qh[j][rows[c]] for j, c in probs]
    bb = [beta[j][rows[c]] for j, c in probs]
    kb = [k * b_ for k, b_ in zip(kc, bb)]
    kbf = [k.astype(BF16) for k in kc]
    kk = [_dot_nt(a.astype(BF16), b_) for a, b_ in zip(kb, kbf)]
    qk = [_dot_nt(a.astype(BF16), b_) for a, b_ in zip(qc, kbf)]
    a_low = [jnp.where(strict, t * d_, 0.0) for t, d_ in zip(kk, decay)]
    attn = [(t * d_).astype(BF16) for t, d_ in zip(qk, decay)]
    e_inv = _inv_unit_lower_minus_eye(a_low, rowi, coli)

    ysc = causal_conv(lambda ln: scc_ref[0, :, ln] * sch_ref[0, :, ln], cws_ref)
    for j in range(hb):
        lanes = slice(j * head_dim, (j + 1) * head_dim)
        ob_ref[0, :, lanes] = (scb_ref[0, :, lanes] * ysc[j]).astype(ob_ref.dtype)

    rhs = [jnp.concatenate([vh[j][rows[c]] * b_, kb_ * eg], axis=1)
           for (j, c), b_, kb_, eg in zip(probs, bb, kb, e_g)]
    sol = [r + _dot(e.astype(BF16), r.astype(BF16)) for r, e in zip(rhs, e_inv)]
    u = [t[:, :head_dim] for t in sol]
    wq = [jnp.concatenate([t[:, head_dim:], q * eg], axis=0).astype(BF16)
          for t, q, eg in zip(sol, qc, e_g)]
    k_dec = [(k * jnp.exp(gl - g)).astype(BF16) for k, gl, g in zip(kc, g_last, g_c)]
    e_last = [jnp.exp(gl) for gl in g_last]

    z = z_ref[0]
    ng = ng_ref[...]
    s = [s_ref[j] for j in range(hb)]
    for c in range(n_chunks):
        idx = [j * n_chunks + c for j in range(hb)]
        ws = [_dot(wq[i], s[j].astype(BF16)) for j, i in enumerate(idx)]
        v_bf = [(u[i] - ws[j][:c_len]).astype(BF16) for j, i in enumerate(idx)]
        o = [ws[j][c_len:] + _dot(attn[i], v_bf[j]) for j, i in enumerate(idx)]
        s = [s[j] * e_last[i] + _dot_tn(k_dec[i], v_bf[j]) for j, i in enumerate(idx)]
        for j in range(hb):
            lanes = slice(j * head_dim, (j + 1) * head_dim)
            zc = z[rows[c], lanes]
            o_ref[0, rows[c], lanes] = (_rms(o[j], ng) * _silu(zc)).astype(o_ref.dtype)
    for j in range(hb):
        s_ref[j] = s[j]


def _gdn(proj3, gates3, gct, norm_g, conv_sc_w, *, n_heads, head_dim, col_q, col_k, col_v, col_z,
         col_b, col_c, col_h, v_width):
    b, t, _ = proj3.shape
    hb = GDN_HEADS_PER_STEP
    gw = hb * head_dim
    tb = _pick_tile(t, 2 * GDN_CHUNK, GDN_CHUNK)
    n_ch = tb // GDN_CHUNK
    steps = t // tb
    taps_sc = conv_sc_w.shape[0]
    qb, kb_, vb, zb = col_q // gw, col_k // gw, col_v // gw, col_z // gw
    sb, sc, sh = col_b // gw, col_c // gw, col_h // gw
    blocks = 7 * tb * gw * 4 + 2 * tb * GATE_LANES * 4 + 8 * gw * 4 + 2 * tb * gw * 2
    scratch = hb * head_dim * head_dim * 4 + (tb + 8) * gw * 4
    kern = functools.partial(_gdn_kernel, n_heads=n_heads, head_dim=head_dim)
    out = jax.ShapeDtypeStruct((b, t, v_width), BF16)
    return pl.pallas_call(
        kern,
        grid=(b, n_heads // hb, steps),
        in_specs=[
            pl.BlockSpec((1, tb, gw), lambda i, h, s: (i, s, qb + h)),
            pl.BlockSpec((1, tb, gw), lambda i, h, s: (i, s, kb_ + h)),
            pl.BlockSpec((1, tb, gw), lambda i, h, s: (i, s, vb + h)),
            pl.BlockSpec((1, tb, gw), lambda i, h, s: (i, s, zb + h)),
            pl.BlockSpec((1, tb, GATE_LANES), lambda i, h, s: (i, s, 0)),
            pl.BlockSpec((n_ch, GATE_LANES, GDN_CHUNK), lambda i, h, s: (i * steps + s, 0, 0)),
            pl.BlockSpec((1, head_dim), lambda i, h, s: (0, 0)),
            pl.BlockSpec((1, tb, gw), lambda i, h, s: (i, s, sb + h)),
            pl.BlockSpec((1, tb, gw), lambda i, h, s: (i, s, sc + h)),
            pl.BlockSpec((1, tb, gw), lambda i, h, s: (i, s, sh + h)),
            pl.BlockSpec((taps_sc, gw), lambda i, h, s: (0, h)),
        ],
        out_specs=[pl.BlockSpec((1, tb, gw), lambda i, h, s: (i, s, h))] * 2,
        out_shape=[out, out],
        scratch_shapes=[
            pltpu.VMEM((hb, head_dim, head_dim), F32),
            pltpu.VMEM((hb, tb + 8, head_dim), F32),
        ],
        compiler_params=pltpu.CompilerParams(
            dimension_semantics=("parallel", "parallel", "arbitrary"),
            vmem_limit_bytes=_vmem_limit(blocks, scratch, 16 << 20)),
        name="gdn",
    )(proj3, proj3, proj3, proj3, gates3, gct, norm_g, proj3, proj3, proj3, conv_sc_w)


def _merge_kernel(a_ref, b_ref, wa_ref, wb_ref, ga_ref, gb_ref, o_ref):
    pa = _dot(a_ref[...], wa_ref[...])
    pb = _dot(b_ref[...], wb_ref[...])
    o_ref[...] = (_sigmoid(ga_ref[...]) * pa + _sigmoid(gb_ref[...]) * pb).astype(o_ref.dtype)


def _merge(o_a, o_b, wa, wb, proj2, *, col_ga, col_gb):
    m, ka = o_a.shape
    kb_ = o_b.shape[1]
    d = wa.shape[1]
    tm = _pick_tile(m, 1024, 16)
    tn = _pick_tile(d, 512, LANES)
    ga0, gb0 = col_ga // tn, col_gb // tn
    blocks = tm * (ka + kb_) * 2 + (ka + kb_) * tn * 2 + 2 * tm * tn * 4 + tm * tn * 2
    return pl.pallas_call(
        _merge_kernel,
        grid=(m // tm, d // tn),
        in_specs=[
            pl.BlockSpec((tm, ka), lambda i, j: (i, 0)),
            pl.BlockSpec((tm, kb_), lambda i, j: (i, 0)),
            pl.BlockSpec((ka, tn), lambda i, j: (0, j)),
            pl.BlockSpec((kb_, tn), lambda i, j: (0, j)),
            pl.BlockSpec((tm, tn), lambda i, j: (i, ga0 + j)),
            pl.BlockSpec((tm, tn), lambda i, j: (i, gb0 + j)),
        ],
        out_specs=pl.BlockSpec((tm, tn), lambda i, j: (i, j)),
        out_shape=jax.ShapeDtypeStruct((m, d), BF16),
        compiler_params=pltpu.CompilerParams(
            dimension_semantics=("parallel", "parallel"),
            vmem_limit_bytes=_vmem_limit(blocks, 0, 4 * tm * tn * 4)),
        name="merge",
    )(o_a, o_b, wa, wb, proj2, proj2)


def _outproj_kernel(m_ref, w_ref, x_ref, g_ref, h_ref, hn_ref):
    h = x_ref[...] + _dot(m_ref[...], w_ref[...])
    h_ref[...] = h
    hn_ref[...] = _rms(h, g_ref[...]).astype(hn_ref.dtype)


def _outproj(merged, w_out, x2, g):
    m, d = x2.shape
    k = merged.shape[1]
    tm = _pick_tile(m, 512, 16)
    blocks = tm * k * 2 + k * d * 2 + 2 * tm * d * 4 + tm * d * 2 + d * 4
    return pl.pallas_call(
        _outproj_kernel,
        grid=(m // tm,),
        in_specs=[
            pl.BlockSpec((tm, k), lambda i: (i, 0)),
            pl.BlockSpec((k, d), lambda i: (0, 0)),
            pl.BlockSpec((tm, d), lambda i: (i, 0)),
            pl.BlockSpec((1, d), lambda i: (0, 0)),
        ],
        out_specs=[
            pl.BlockSpec((tm, d), lambda i: (i, 0)),
            pl.BlockSpec((tm, d), lambda i: (i, 0)),
        ],
        out_shape=[
            jax.ShapeDtypeStruct((m, d), F32),
            jax.ShapeDtypeStruct((m, d), BF16),
        ],
        compiler_params=pltpu.CompilerParams(
            dimension_semantics=("parallel",),
            vmem_limit_bytes=_vmem_limit(blocks, 0, 3 * tm * d * 4)),
        name="outproj",
    )(merged, w_out, x2, g)


def _ffn_kernel(hn_ref, wg_ref, wu_ref, wd_ref, h_hbm, g_ref, o_ref, sem, *, final_norm):
    i = pl.program_id(0)
    f = pl.program_id(1)
    tm = o_ref.shape[0]

    def residual_copy():
        return pltpu.make_async_copy(h_hbm.at[pl.ds(pl.multiple_of(i * tm, tm), tm), :], o_ref, sem)

    @pl.when(f == 0)
    def _():
        residual_copy().start()

    hn = hn_ref[...]
    ff = (_silu(_dot(hn, wg_ref[...])) * _dot(hn, wu_ref[...])).astype(BF16)

    @pl.when(f == 0)
    def _():
        residual_copy().wait()

    o_ref[...] += _dot(ff, wd_ref[...])

    if final_norm:
        @pl.when(f == pl.num_programs(1) - 1)
        def _():
            o_ref[...] = _rms(o_ref[...], g_ref[...])


def _ffn(hn, wg, wu, wd, h, g, *, final_norm):
    m, d = h.shape
    dff = wg.shape[1]
    tm = _pick_tile(m, 1024, 16)
    tf = _pick_tile(dff, 512, LANES)
    blocks = tm * d * 2 + 3 * d * tf * 2 + tm * d * 4 + d * 4
    kern = functools.partial(_ffn_kernel, final_norm=final_norm)
    return pl.pallas_call(
        kern,
        grid=(m // tm, dff // tf),
        in_specs=[
            pl.BlockSpec((tm, d), lambda i, f: (i, 0)),
            pl.BlockSpec((d, tf), lambda i, f: (0, f)),
            pl.BlockSpec((d, tf), lambda i, f: (0, f)),
            pl.BlockSpec((tf, d), lambda i, f: (f, 0)),
            pl.BlockSpec(memory_space=pl.ANY),
            pl.BlockSpec((1, d), lambda i, f: (0, 0)),
        ],
        out_specs=pl.BlockSpec((tm, d), lambda i, f: (i, 0)),
        out_shape=jax.ShapeDtypeStruct((m, d), F32),
        scratch_shapes=[pltpu.SemaphoreType.DMA(())],
        compiler_params=pltpu.CompilerParams(
            dimension_semantics=("parallel", "arbitrary"),
            vmem_limit_bytes=_vmem_limit(blocks, 0, 4 * tm * tf * 4 + tm * d * 4)),
        name="ffn",
    )(hn, wg, wu, wd, h, g)


def kernel(x, ln_mix_g, w_in, conv_qkv_w, A_log, dt_bias, gdn_norm_g, w_proj_a, conv_sc_w, w_proj_b,
           w_out, ln_ffn_g, w_gate, w_up, w_down, ln_final_g):
    b, t, d = x.shape
    depth = w_in.shape[0]
    n_heads = A_log.shape[1]
    v_width = w_proj_a.shape[1]
    qk_width = (conv_qkv_w.shape[2] - v_width) // 2
    sc_width = w_proj_b.shape[1]
    head_dim = gdn_norm_g.shape[1]
    assert qk_width == n_heads * head_dim and v_width == n_heads * head_dim
    assert head_dim == LANES and n_heads % GDN_HEADS_PER_STEP == 0 and 2 * n_heads <= GATE_LANES
    assert t % GDN_CHUNK == 0 and sc_width == v_width

    col_q, col_k, col_v = 0, qk_width, 2 * qk_width
    col_z = col_v + v_width
    col_gates = col_z + v_width
    rest = col_gates + 2 * n_heads
    col_b = col_gates
    col_c = col_b + sc_width
    col_h = col_c + sc_width
    col_ga = col_h + sc_width
    col_gb = col_ga + d

    h2 = x.reshape(b * t, d)
    for l in range(depth):
        w_t = jnp.swapaxes(w_in[l], 0, 1)
        w_g_hi, w_g_lo = _gprep(w_t, col_gates=col_gates, gate_cols=rest - col_gates)
        gate_params = jnp.pad(jnp.stack([A_log[l], dt_bias[l]]),
                              ((0, 6), (n_heads, GATE_LANES - 2 * n_heads)))

        xn, gates2, gct = _norm(h2, ln_mix_g[l][None, :], w_g_hi, w_g_lo, gate_params, n_heads=n_heads)
        later = [w_proj_a[l], w_proj_b[l], w_out[l], w_gate[l], w_up[l], w_down[l]]
        proj2, (wa_bf, wb_bf, wout_bf, wgate_bf, wup_bf, wdown_bf) = _inproj(
            xn, w_t, conv_qkv_w[l], later, col_gates=col_gates, gate_cols=rest - col_gates, seq_len=t)
        proj3 = proj2.reshape(b, t, -1)
        o_a, o_b = _gdn(proj3, gates2.reshape(b, t, GATE_LANES), gct,
                        gdn_norm_g[l][None, :], conv_sc_w[l], n_heads=n_heads, head_dim=head_dim,
                        col_q=col_q, col_k=col_k, col_v=col_v, col_z=col_z,
                        col_b=col_b, col_c=col_c, col_h=col_h, v_width=v_width)
        merged = _merge(o_a.reshape(b * t, v_width), o_b.reshape(b * t, sc_width), wa_bf, wb_bf, proj2,
                        col_ga=col_ga, col_gb=col_gb)
        h_mid, hn = _outproj(merged, wout_bf, h2, ln_ffn_g[l][None, :])
        last = l == depth - 1
        g_last = ln_final_g[None, :] if last else jnp.ones((1, d), F32)
        h2 = _ffn(hn, wgate_bf, wup_bf, wdown_bf, h_mid, g_last, final_norm=last)
    return h2.reshape(b, t, d)
```

```python
import functools
import math

import jax
import jax.numpy as jnp
from jax import lax
from jax.experimental import pallas as pl
from jax.experimental.pallas import tpu as pltpu

EPS = 1e-6
F32 = jnp.float32
BF16 = jnp.bfloat16

LANES = 128
MXU_COLS = 256
V7X_VMEM_CAP_BYTES = 56 * 1024 * 1024
GDN_CHUNK = 128
GDN_HEADS_PER_STEP = 16
GATE_LANES = LANES


def _vmem_limit(block_bytes, scratch_bytes=0, temp_bytes=0):
    need = 2 * block_bytes + scratch_bytes + temp_bytes + (4 << 20)
    return int(min(V7X_VMEM_CAP_BYTES, max(need, 16 << 20)))


def _pick_tile(n, target, align):
    t = min(n, target)
    t -= t % align
    while t >= align:
        if n % t == 0:
            return t
        t -= align
    return n


def _sigmoid(x):
    return 0.5 * jnp.tanh(0.5 * x) + 0.5


def _silu(x):
    h = 0.5 * x
    return h + h * jnp.tanh(h)


def _dot(a, b):
    return jnp.dot(a, b, preferred_element_type=F32)


def _dot_nt(a, b):
    return lax.dot_general(a, b, (((1,), (1,)), ((), ())), preferred_element_type=F32)


def _dot_tn(a, b):
    return lax.dot_general(a, b, (((0,), (0,)), ((), ())), preferred_element_type=F32)


def _rms(x, g):
    return x * lax.rsqrt(jnp.mean(x * x, axis=-1, keepdims=True) + EPS) * g


def _gprep_kernel(g_ref, hi_ref, lo_ref):
    g = g_ref[...]
    pad = jnp.zeros((GATE_LANES - g.shape[0], g.shape[1]), F32)
    w = jnp.concatenate([g, pad], axis=0).T
    hi = w.astype(BF16)
    hi_ref[...] = hi
    lo_ref[...] = (w - hi.astype(F32)).astype(BF16)


def _gprep(w_t, *, col_gates, gate_cols):
    d = w_t.shape[1]
    out = jax.ShapeDtypeStruct((d, GATE_LANES), BF16)
    return pl.pallas_call(
        _gprep_kernel,
        grid=(1,),
        in_specs=[pl.BlockSpec((pl.Element(gate_cols), pl.Element(d)), lambda i: (col_gates, 0))],
        out_specs=[pl.BlockSpec((d, GATE_LANES), lambda i: (0, 0))] * 2,
        out_shape=[out, out],
        name="gprep",
    )(w_t)


NORM_ROWS = 2 * GDN_CHUNK


def _norm_kernel(x_ref, g_ref, wg_hi_ref, wg_lo_ref, gp_ref, xn_ref, gates_ref, gct_ref, *, n_heads):
    rows = NORM_ROWS
    ch_per_pass = rows // GDN_CHUNK
    gp = gp_ref[...]
    row_in_chunk = lax.broadcasted_iota(jnp.int32, (rows, GATE_LANES), 0) & (GDN_CHUNK - 1)
    lane = lax.broadcasted_iota(jnp.int32, (rows, GATE_LANES), 1)

    def norm_rows(r, carry):
        r0 = pl.multiple_of(r * rows, rows)
        y = _rms(x_ref[pl.ds(r0, rows), :], g_ref[...])
        hi = y.astype(BF16)
        xn_ref[pl.ds(r0, rows), :] = hi
        lo = (y - hi.astype(F32)).astype(BF16)
        w_hi = wg_hi_ref[...]
        hi_terms = _dot(hi, jnp.concatenate([w_hi, wg_lo_ref[...]], axis=1))
        gates = hi_terms[:, :GATE_LANES] + hi_terms[:, GATE_LANES:] + _dot(lo, w_hi)
        xa = gates + gp[1:2, :]
        softplus = jnp.maximum(xa, 0.0) + jnp.log(1.0 + jnp.exp(-jnp.abs(xa)))
        gcum = -jnp.exp(gp[0:1, :]) * softplus
        shift = 1
        while shift < GDN_CHUNK:
            gcum = gcum + jnp.where(row_in_chunk >= shift, pltpu.roll(gcum, shift, 0), 0.0)
            shift *= 2
        gates_ref[pl.ds(r0, rows), :] = jnp.where(lane < n_heads, _sigmoid(gates), gcum)
        for c in range(ch_per_pass):
            gct_ref[r * ch_per_pass + c] = gcum[c * GDN_CHUNK:(c + 1) * GDN_CHUNK, :].T
        return carry

    lax.fori_loop(0, x_ref.shape[0] // rows, norm_rows, 0)


def _norm(x2, g, wg_hi, wg_lo, gate_params, *, n_heads):
    m, d = x2.shape
    tm = _pick_tile(m, 1024, NORM_ROWS)
    n_ch = tm // GDN_CHUNK
    blocks = tm * d * 6 + 2 * tm * GATE_LANES * 4 + 2 * d * GATE_LANES * 2
    kern = functools.partial(_norm_kernel, n_heads=n_heads)
    return pl.pallas_call(
        kern,
        grid=(m // tm,),
        in_specs=[
            pl.BlockSpec((tm, d), lambda i: (i, 0)),
            pl.BlockSpec((1, d), lambda i: (0, 0)),
            pl.BlockSpec((d, GATE_LANES), lambda i: (0, 0)),
            pl.BlockSpec((d, GATE_LANES), lambda i: (0, 0)),
            pl.BlockSpec((8, GATE_LANES), lambda i: (0, 0)),
        ],
        out_specs=[
            pl.BlockSpec((tm, d), lambda i: (i, 0)),
            pl.BlockSpec((tm, GATE_LANES), lambda i: (i, 0)),
            pl.BlockSpec((n_ch, GATE_LANES, GDN_CHUNK), lambda i: (i, 0, 0)),
        ],
        out_shape=[
            jax.ShapeDtypeStruct((m, d), BF16),
            jax.ShapeDtypeStruct((m, GATE_LANES), F32),
            jax.ShapeDtypeStruct((m // GDN_CHUNK, GATE_LANES, GDN_CHUNK), F32),
        ],
        compiler_params=pltpu.CompilerParams(
            dimension_semantics=("parallel",),
            vmem_limit_bytes=_vmem_limit(blocks, 0, 8 * NORM_ROWS * d * 4)),
        name="norm",
    )(x2, g, wg_hi, wg_lo, gate_params)


def _inproj_kernel(*refs, n_side, n_tiles, n_conv, seq_len):
    xn_ref, wt_ref, cw_ref = refs[:3]
    side_in = refs[3:3 + n_side]
    out_ref = refs[3 + n_side]
    side_out = refs[4 + n_side:4 + 2 * n_side]
    wbf_ref, cbuf_ref, halo_ref = refs[4 + 2 * n_side:]
    jn = pl.program_id(0)
    im = pl.program_id(1)
    tm, tn = out_ref.shape
    lane_tiles = tn // LANES
    prep_cols = wt_ref.shape[0]

    @pl.when(jnp.logical_and(jn == 0, im == 0))
    def _():
        halo_ref[...] = jnp.zeros_like(halo_ref)

    def side_jobs():
        w_t = wt_ref[...].T.astype(BF16)
        slot = (jn % 2) * lane_tiles + im * (prep_cols // LANES)
        for c in range(prep_cols // LANES):
            wbf_ref[slot + c] = w_t[:, c * LANES:(c + 1) * LANES]
        for s_in, s_out in zip(side_in, side_out):
            s_out[...] = s_in[...].astype(s_out.dtype)

    def weights():
        slot = ((jn + 1) % 2) * lane_tiles
        return jnp.concatenate([wbf_ref[slot + lt] for lt in range(lane_tiles)], axis=1)

    @pl.when(jn == 0)
    def _():
        side_jobs()

    @pl.when(jn > n_conv)
    def _():
        out_ref[...] = _dot(xn_ref[...], weights())
        side_jobs()

    @pl.when(jnp.logical_and(jn >= 1, jn <= n_conv))
    def _():
        cw = cw_ref[...]
        taps = cw.shape[0]
        starts_seq = (im * tm) % seq_len == 0
        acc = _dot(xn_ref[...], weights())
        for lt in range(lane_tiles):
            lanes = slice(lt * LANES, (lt + 1) * LANES)
            x = acc[:, lanes]
            cbuf_ref[lt, pl.ds(0, 8), :] = jnp.where(starts_seq, 0.0, halo_ref[lt])
            cbuf_ref[lt, pl.ds(8, tm), :] = x
            y = x * cw[taps - 1:taps, lanes]
            for s in range(1, taps):
                y = y + cbuf_ref[lt, pl.ds(8 - s, tm), :] * cw[taps - 1 - s:taps - s, lanes]
            halo_ref[lt] = x[tm - 8:tm, :]
            out_ref[:, lanes] = _silu(y)
        side_jobs()


def _slab_rows(n_rows, steps):
    for r in range(16, n_rows + 1, 16):
        if n_rows % r == 0 and n_rows // r <= steps:
            return r
    return n_rows


def _inproj(xn, w_t, conv_w, side, *, col_gates, gate_cols, seq_len):
    m, d = xn.shape
    n = w_t.shape[0] - gate_cols
    taps, conv_cols = conv_w.shape
    tm = _pick_tile(seq_len, 1024, 16)
    tn = _pick_tile(math.gcd(math.gcd(n, conv_cols), col_gates), 1024, LANES)
    n_m = m // tm
    assert m % tm == 0 and n % tn == 0 and taps <= 8 and gate_cols % 8 == 0
    assert tn % n_m == 0 and (tn // n_m) % LANES == 0
    n_tiles = n // tn
    n_conv = conv_cols // tn
    n_plain = col_gates // tn
    steps = (n_tiles + 1) * n_m
    side_specs, side_shapes, side_bytes = [], [], 0
    for w in side:
        r = _slab_rows(w.shape[0], steps)
        last = w.shape[0] // r - 1
        side_specs.append(pl.BlockSpec((r, w.shape[1]),
                                       lambda j, i, last=last: (jnp.minimum(j * n_m + i, last), 0)))
        side_shapes.append(jax.ShapeDtypeStruct(w.shape, BF16))
        side_bytes += r * w.shape[1] * 6

    prep_cols = tn // n_m

    def weight_rows(j, i):
        tile = jnp.minimum(j, n_tiles - 1)
        start = tile * tn + jnp.where(tile >= n_plain, gate_cols, 0) + i * prep_cols
        return pl.multiple_of(start, 8), 0

    blocks = tm * d * 2 + prep_cols * d * 4 + tm * tn * 4 + 8 * tn * 4 + side_bytes
    scratch = 2 * d * tn * 2 + (tn // LANES) * (tm + 16) * LANES * 4
    kern = functools.partial(_inproj_kernel, n_side=len(side), n_tiles=n_tiles, n_conv=n_conv,
                             seq_len=seq_len)
    outs = pl.pallas_call(
        kern,
        grid=(n_tiles + 1, n_m),
        in_specs=[
            pl.BlockSpec((tm, d), lambda j, i: (jnp.where(j == 0, 0, i), 0)),
            pl.BlockSpec((pl.Element(prep_cols), pl.Element(d)), weight_rows),
            pl.BlockSpec((taps, tn), lambda j, i: (0, jnp.clip(j - 1, 0, n_conv - 1))),
        ] + side_specs,
        out_specs=[
            pl.BlockSpec((tm, tn), lambda j, i: (jnp.where(j == 0, 0, i), jnp.maximum(j - 1, 0))),
        ] + side_specs,
        out_shape=[jax.ShapeDtypeStruct((m, n), F32)] + side_shapes,
        scratch_shapes=[
            pltpu.VMEM((2 * (tn // LANES), d, LANES), BF16),
            pltpu.VMEM((tn // LANES, tm + 8, LANES), F32),
            pltpu.VMEM((tn // LANES, 8, LANES), F32),
        ],
        compiler_params=pltpu.CompilerParams(
            dimension_semantics=("arbitrary", "arbitrary"),
            vmem_limit_bytes=_vmem_limit(blocks, scratch, 2 * tm * tn * 4)),
        name="inproj",
    )(xn, w_t, conv_w, *side)
    return outs[0], outs[1:]


def _inv_unit_lower_minus_eye(a_list, rowi, coli):
    n = GDN_CHUNK
    diag16 = (rowi >> 4) == (coli >> 4)
    x = [jnp.where(diag16, a, 0.0) for a in a_list]
    e = [-t for t in x]
    xb = [t.astype(BF16) for t in x]
    x = [_dot(t, t) for t in xb]
    for step in range(3):
        xb = [t.astype(BF16) for t in x]
        if step < 2:
            both = [_dot(jnp.concatenate([ei.astype(BF16), xi], axis=0), xi) for ei, xi in zip(e, xb)]
            ex = [t[:n] for t in both]
            x_next = [t[n:] for t in both]
        else:
            ex = [_dot(ei.astype(BF16), xi) for ei, xi in zip(e, xb)]
            x_next = x
        e = [ei + xi + exi for ei, xi, exi in zip(e, x, ex)]
        x = x_next

    half = 16
    while half < n:
        rows = [slice(r, r + half) for r in range(half, n, 2 * half)]
        pick = lambda t: jnp.concatenate([t[r] for r in rows], axis=0)
        p = lax.broadcasted_iota(jnp.int32, (n // 2, n), 0)
        ci = lax.broadcasted_iota(jnp.int32, (n // 2, n), 1)
        ri = (((p // half) * 2 + 1) * half) + (p % half)
        off = jnp.logical_and((ri >> half.bit_length()) == (ci >> half.bit_length()), ci < (ri & -half))
        eb = [t.astype(BF16) for t in e]
        y = [jnp.where(off, pick(a), 0.0) for a in a_list]
        z = [yi + _dot(yi.astype(BF16), ei) for yi, ei in zip(y, eb)]
        zero = jnp.zeros((half, n), F32)
        z_full = []
        for zi in z:
            pieces = []
            for k in range(len(rows)):
                pieces += [zero, zi[k * half:(k + 1) * half]]
            z_full.append(jnp.concatenate(pieces, axis=0).astype(BF16))
        ez = [_dot(pick(ei).astype(BF16), zf) for ei, zf in zip(e, z_full)]
        new = [pick(ei) - zi - ezi for ei, zi, ezi in zip(e, z, ez)]
        merged = []
        for ei, ni in zip(e, new):
            pieces = []
            for k, r in enumerate(rows):
                pieces += [ei[r.start - half:r.start], ni[k * half:(k + 1) * half]]
            merged.append(jnp.concatenate(pieces, axis=0))
        e = merged
        half *= 2
    return e


def _gdn_kernel(q_ref, k_ref, v_ref, z_ref, gates_ref, gct_ref, ng_ref, scb_ref, scc_ref, sch_ref,
                cws_ref, o_ref, ob_ref, s_ref, halo_ref, *, n_heads, head_dim):
    c_len = GDN_CHUNK
    hb = GDN_HEADS_PER_STEP
    tb = q_ref.shape[1]
    n_chunks = tb // c_len
    hg = pl.program_id(1)

    @pl.when(pl.program_id(2) == 0)
    def _():
        s_ref[...] = jnp.zeros_like(s_ref)
        halo_ref[:, 0:8, :] = jnp.zeros((hb, 8, head_dim), F32)

    gates = gates_ref[0]
    gate_lane = lax.broadcasted_iota(jnp.int32, (tb, GATE_LANES), 1)

    def causal_conv(load_x, cw_ref):
        cw = cw_ref[...]
        taps = cw.shape[0]
        out = []
        for j in range(hb):
            lanes = slice(j * head_dim, (j + 1) * head_dim)
            x = load_x(lanes)
            halo_ref[j, pl.ds(8, tb), :] = x
            y = x * cw[taps - 1:taps, lanes]
            for s in range(1, taps):
                y = y + halo_ref[j, pl.ds(8 - s, tb), :] * cw[taps - 1 - s:taps - s, lanes]
            halo_ref[j, pl.ds(0, 8), :] = x[tb - 8:tb, :]
            out.append(y)
        return out

    head_lanes = [slice(j * head_dim, (j + 1) * head_dim) for j in range(hb)]
    yq = [q_ref[0, :, ln] for ln in head_lanes]
    yk = [k_ref[0, :, ln] for ln in head_lanes]
    yv = [v_ref[0, :, ln] for ln in head_lanes]

    rowi = lax.broadcasted_iota(jnp.int32, (c_len, c_len), 0)
    coli = lax.broadcasted_iota(jnp.int32, (c_len, c_len), 1)
    incl = rowi >= coli
    strict = rowi > coli

    qh, kh, vh, beta, gcol = [], [], [], [], []
    for j in range(hb):
        head = hg * hb + j
        qj = yq[j]
        kj = yk[j]
        qh.append(qj * (lax.rsqrt(jnp.sum(qj * qj, axis=1, keepdims=True) + EPS) * head_dim ** -0.5))
        kh.append(kj * lax.rsqrt(jnp.sum(kj * kj, axis=1, keepdims=True) + EPS))
        vh.append(yv[j])
        beta.append(jnp.sum(jnp.where(gate_lane == head, gates, 0.0), axis=1, keepdims=True))
        gcol.append(jnp.sum(jnp.where(gate_lane == head + n_heads, gates, 0.0), axis=1, keepdims=True))

    probs = [(j, c) for j in range(hb) for c in range(n_chunks)]
    rows = {c: slice(c * c_len, (c + 1) * c_len) for c in range(n_chunks)}
    g_c = [gcol[j][rows[c]] for j, c in probs]
    g_r = [gct_ref[c, pl.ds(hg * hb + j + n_heads, 1), :] for j, c in probs]
    g_last = [g[c_len - 1:c_len, :] for g in g_c]
    decay = [jnp.where(incl, jnp.exp(jnp.where(incl, gc - gr, 0.0)), 0.0) for gc, gr in zip(g_c, g_r)]
    e_g = [jnp.exp(g) for g in g_c]
    kc = [kh[j][rows[c]] for j, c in probs]
    qc = [qh[j][rows[c]] for j, c in probs]
    bb = [beta[j][rows[c]] for j, c in probs]
    kb = [k * b_ for k, b_ in zip(kc, bb)]
    kbf = [k.astype(BF16) for k in kc]
    kk = [_dot_nt(a.astype(BF16), b_) for a, b_ in zip(kb, kbf)]
    qk = [_dot_nt(a.astype(BF16), b_) for a, b_ in zip(qc, kbf)]
    a_low = [jnp.where(strict, t * d_, 0.0) for t, d_ in zip(kk, decay)]
    attn = [(t * d_).astype(BF16) for t, d_ in zip(qk, decay)]
    e_inv = _inv_unit_lower_minus_eye(a_low, rowi, coli)

    ysc = causal_conv(lambda ln: scc_ref[0, :, ln] * sch_ref[0, :, ln], cws_ref)
    for j in range(hb):
        lanes = slice(j * head_dim, (j + 1) * head_dim)
        ob_ref[0, :, lanes] = (scb_ref[0, :, lanes] * ysc[j]).astype(ob_ref.dtype)

    rhs = [jnp.concatenate([vh[j][rows[c]] * b_, kb_ * eg], axis=1)
           for (j, c), b_, kb_, eg in zip(probs, bb, kb, e_g)]
    sol = [r + _dot(e.astype(BF16), r.astype(BF16)) for r, e in zip(rhs, e_inv)]
    u = [t[:, :head_dim] for t in sol]
    wq = [jnp.concatenate([t[:, head_dim:], q * eg], axis=0).astype(BF16)
          for t, q, eg in zip(sol, qc, e_g)]
    k_dec = [(k * jnp.exp(gl - g)).astype(BF16) for k, gl, g in zip(kc, g_last, g_c)]
    e_last = [jnp.exp(gl) for gl in g_last]

    z = z_ref[0]
    ng = ng_ref[...]
    s = [s_ref[j] for j in range(hb)]
    for c in range(n_chunks):
        idx = [j * n_chunks + c for j in range(hb)]
        ws = [_dot(wq[i], s[j].astype(BF16)) for j, i in enumerate(idx)]
        v_bf = [(u[i] - ws[j][:c_len]).astype(BF16) for j, i in enumerate(idx)]
        o = [ws[j][c_len:] + _dot(attn[i], v_bf[j]) for j, i in enumerate(idx)]
        s = [s[j] * e_last[i] + _dot_tn(k_dec[i], v_bf[j]) for j, i in enumerate(idx)]
        for j in range(hb):
            lanes = slice(j * head_dim, (j + 1) * head_dim)
            zc = z[rows[c], lanes]
            o_ref[0, rows[c], lanes] = (_rms(o[j], ng) * _silu(zc)).astype(o_ref.dtype)
    for j in range(hb):
        s_ref[j] = s[j]


def _gdn(proj3, gates3, gct, norm_g, conv_sc_w, *, n_heads, head_dim, col_q, col_k, col_v, col_z,
         col_b, col_c, col_h, v_width):
    b, t, _ = proj3.shape
    hb = GDN_HEADS_PER_STEP
    gw = hb * head_dim
    tb = _pick_tile(t, 2 * GDN_CHUNK, GDN_CHUNK)
    n_ch = tb // GDN_CHUNK
    steps = t // tb
    taps_sc = conv_sc_w.shape[0]
    qb, kb_, vb, zb = col_q // gw, col_k // gw, col_v // gw, col_z // gw
    sb, sc, sh = col_b // gw, col_c // gw, col_h // gw
    blocks = 7 * tb * gw * 4 + 2 * tb * GATE_LANES * 4 + 8 * gw * 4 + 2 * tb * gw * 2
    scratch = hb * head_dim * head_dim * 4 + (tb + 8) * gw * 4
    kern = functools.partial(_gdn_kernel, n_heads=n_heads, head_dim=head_dim)
    out = jax.ShapeDtypeStruct((b, t, v_width), BF16)
    return pl.pallas_call(
        kern,
        grid=(b, n_heads // hb, steps),
        in_specs=[
            pl.BlockSpec((1, tb, gw), lambda i, h, s: (i, s, qb + h)),
            pl.BlockSpec((1, tb, gw), lambda i, h, s: (i, s, kb_ + h)),
            pl.BlockSpec((1, tb, gw), lambda i, h, s: (i, s, vb + h)),
            pl.BlockSpec((1, tb, gw), lambda i, h, s: (i, s, zb + h)),
            pl.BlockSpec((1, tb, GATE_LANES), lambda i, h, s: (i, s, 0)),
            pl.BlockSpec((n_ch, GATE_LANES, GDN_CHUNK), lambda i, h, s: (i * steps + s, 0, 0)),
            pl.BlockSpec((1, head_dim), lambda i, h, s: (0, 0)),
            pl.BlockSpec((1, tb, gw), lambda i, h, s: (i, s, sb + h)),
            pl.BlockSpec((1, tb, gw), lambda i, h, s: (i, s, sc + h)),
            pl.BlockSpec((1, tb, gw), lambda i, h, s: (i, s, sh + h)),
            pl.BlockSpec((taps_sc, gw), lambda i, h, s: (0, h)),
        ],
        out_specs=[pl.BlockSpec((1, tb, gw), lambda i, h, s: (i, s, h))] * 2,
        out_shape=[out, out],
        scratch_shapes=[
            pltpu.VMEM((hb, head_dim, head_dim), F32),
            pltpu.VMEM((hb, tb + 8, head_dim), F32),
        ],
        compiler_params=pltpu.CompilerParams(
            dimension_semantics=("parallel", "parallel", "arbitrary"),
            vmem_limit_bytes=_vmem_limit(blocks, scratch, 16 << 20)),
        name="gdn",
    )(proj3, proj3, proj3, proj3, gates3, gct, norm_g, proj3, proj3, proj3, conv_sc_w)


def _merge_kernel(a_ref, b_ref, wa_ref, wb_ref, ga_ref, gb_ref, o_ref):
    pa = _dot(a_ref[...], wa_ref[...])
    pb = _dot(b_ref[...], wb_ref[...])
    o_ref[...] = (_sigmoid(ga_ref[...]) * pa + _sigmoid(gb_ref[...]) * pb).astype(o_ref.dtype)


def _merge(o_a, o_b, wa, wb, proj2, *, col_ga, col_gb):
    m, ka = o_a.shape
    kb_ = o_b.shape[1]
    d = wa.shape[1]
    tm = _pick_tile(m, 1024, 16)
    tn = _pick_tile(d, 512, LANES)
    ga0, gb0 = col_ga // tn, col_gb // tn
    blocks = tm * (ka + kb_) * 2 + (ka + kb_) * tn * 2 + 2 * tm * tn * 4 + tm * tn * 2
    return pl.pallas_call(
        _merge_kernel,
        grid=(m // tm, d // tn),
        in_specs=[
            pl.BlockSpec((tm, ka), lambda i, j: (i, 0)),
            pl.BlockSpec((tm, kb_), lambda i, j: (i, 0)),
            pl.BlockSpec((ka, tn), lambda i, j: (0, j)),
            pl.BlockSpec((kb_, tn), lambda i, j: (0, j)),
            pl.BlockSpec((tm, tn), lambda i, j: (i, ga0 + j)),
            pl.BlockSpec((tm, tn), lambda i, j: (i, gb0 + j)),
        ],
        out_specs=pl.BlockSpec((tm, tn), lambda i, j: (i, j)),
        out_shape=jax.ShapeDtypeStruct((m, d), BF16),
        compiler_params=pltpu.CompilerParams(
            dimension_semantics=("parallel", "parallel"),
            vmem_limit_bytes=_vmem_limit(blocks, 0, 4 * tm * tn * 4)),
        name="merge",
    )(o_a, o_b, wa, wb, proj2, proj2)


def _outproj_kernel(m_ref, w_ref, x_ref, g_ref, h_ref, hn_ref):
    h = x_ref[...] + _dot(m_ref[...], w_ref[...])
    h_ref[...] = h
    hn_ref[...] = _rms(h, g_ref[...]).astype(hn_ref.dtype)


def _outproj(merged, w_out, x2, g):
    m, d = x2.shape
    k = merged.shape[1]
    tm = _pick_tile(m, 512, 16)
    blocks = tm * k * 2 + k * d * 2 + 2 * tm * d * 4 + tm * d * 2 + d * 4
    return pl.pallas_call(
        _outproj_kernel,
        grid=(m // tm,),
        in_specs=[
            pl.BlockSpec((tm, k), lambda i: (i, 0)),
            pl.BlockSpec((k, d), lambda i: (0, 0)),
            pl.BlockSpec((tm, d), lambda i: (i, 0)),
            pl.BlockSpec((1, d), lambda i: (0, 0)),
        ],
        out_specs=[
            pl.BlockSpec((tm, d), lambda i: (i, 0)),
            pl.BlockSpec((tm, d), lambda i: (i, 0)),
        ],
        out_shape=[
            jax.ShapeDtypeStruct((m, d), F32),
            jax.ShapeDtypeStruct((m, d), BF16),
        ],
        compiler_params=pltpu.CompilerParams(
            dimension_semantics=("parallel",),
            vmem_limit_bytes=_vmem_limit(blocks, 0, 3 * tm * d * 4)),
        name="outproj",
    )(merged, w_out, x2, g)


def _ffn_kernel(hn_ref, wg_ref, wu_ref, wd_ref, h_hbm, g_ref, o_ref, ff_ref, sem, *, final_norm):
    i = pl.program_id(0)
    f = pl.program_id(1)
    n_hidden = pl.num_programs(1) - 1
    tm = o_ref.shape[0]

    def residual_copy():
        return pltpu.make_async_copy(h_hbm.at[pl.ds(pl.multiple_of(i * tm, tm), tm), :], o_ref, sem)

    def hidden():
        hn = hn_ref[...]
        return (_silu(_dot(hn, wg_ref[...])) * _dot(hn, wu_ref[...])).astype(BF16)

    @pl.when(f == 0)
    def _():
        residual_copy().start()
        ff_ref[...] = hidden()

    @pl.when(f == 1)
    def _():
        residual_copy().wait()

    @pl.when(jnp.logical_and(f >= 1, f < n_hidden))
    def _():
        o_ref[...] += _dot(ff_ref[...], wd_ref[...])
        ff_ref[...] = hidden()

    @pl.when(f == n_hidden)
    def _():
        o_ref[...] += _dot(ff_ref[...], wd_ref[...])
        if final_norm:
            o_ref[...] = _rms(o_ref[...], g_ref[...])


def _ffn(hn, wg, wu, wd, h, g, *, final_norm):
    m, d = h.shape
    dff = wg.shape[1]
    tm = _pick_tile(m, 1024, 16)
    tf = _pick_tile(dff, 512, LANES)
    nf = dff // tf
    assert nf >= 2
    blocks = tm * d * 2 + 3 * d * tf * 2 + tm * d * 4 + d * 4
    kern = functools.partial(_ffn_kernel, final_norm=final_norm)
    return pl.pallas_call(
        kern,
        grid=(m // tm, nf + 1),
        in_specs=[
            pl.BlockSpec((tm, d), lambda i, f: (i, 0)),
            pl.BlockSpec((d, tf), lambda i, f: (0, jnp.minimum(f, nf - 1))),
            pl.BlockSpec((d, tf), lambda i, f: (0, jnp.minimum(f, nf - 1))),
            pl.BlockSpec((tf, d), lambda i, f: (jnp.maximum(f - 1, 0), 0)),
            pl.BlockSpec(memory_space=pl.ANY),
            pl.BlockSpec((1, d), lambda i, f: (0, 0)),
        ],
        out_specs=pl.BlockSpec((tm, d), lambda i, f: (i, 0)),
        out_shape=jax.ShapeDtypeStruct((m, d), F32),
        scratch_shapes=[pltpu.VMEM((tm, tf), BF16), pltpu.SemaphoreType.DMA(())],
        compiler_params=pltpu.CompilerParams(
            dimension_semantics=("parallel", "arbitrary"),
            vmem_limit_bytes=_vmem_limit(blocks, tm * tf * 2, 4 * tm * tf * 4 + tm * d * 4)),
        name="ffn",
    )(hn, wg, wu, wd, h, g)


def kernel(x, ln_mix_g, w_in, conv_qkv_w, A_log, dt_bias, gdn_norm_g, w_proj_a, conv_sc_w, w_proj_b,
           w_out, ln_ffn_g, w_gate, w_up, w_down, ln_final_g):
    b, t, d = x.shape
    depth = w_in.shape[0]
    n_heads = A_log.shape[1]
    v_width = w_proj_a.shape[1]
    qk_width = (conv_qkv_w.shape[2] - v_width) // 2
    sc_width = w_proj_b.shape[1]
    head_dim = gdn_norm_g.shape[1]
    assert qk_width == n_heads * head_dim and v_width == n_heads * head_dim
    assert head_dim == LANES and n_heads % GDN_HEADS_PER_STEP == 0 and 2 * n_heads <= GATE_LANES
    assert t % GDN_CHUNK == 0 and sc_width == v_width

    col_q, col_k, col_v = 0, qk_width, 2 * qk_width
    col_z = col_v + v_width
    col_gates = col_z + v_width
    rest = col_gates + 2 * n_heads
    col_b = col_gates
    col_c = col_b + sc_width
    col_h = col_c + sc_width
    col_ga = col_h + sc_width
    col_gb = col_ga + d

    h2 = x.reshape(b * t, d)
    for l in range(depth):
        w_t = jnp.swapaxes(w_in[l], 0, 1)
        w_g_hi, w_g_lo = _gprep(w_t, col_gates=col_gates, gate_cols=rest - col_gates)
        gate_params = jnp.pad(jnp.stack([A_log[l], dt_bias[l]]),
                              ((0, 6), (n_heads, GATE_LANES - 2 * n_heads)))

        xn, gates2, gct = _norm(h2, ln_mix_g[l][None, :], w_g_hi, w_g_lo, gate_params, n_heads=n_heads)
        later = [w_proj_a[l], w_proj_b[l], w_out[l], w_gate[l], w_up[l], w_down[l]]
        proj2, (wa_bf, wb_bf, wout_bf, wgate_bf, wup_bf, wdown_bf) = _inproj(
            xn, w_t, conv_qkv_w[l], later, col_gates=col_gates, gate_cols=rest - col_gates, seq_len=t)
        proj3 = proj2.reshape(b, t, -1)
        o_a, o_b = _gdn(proj3, gates2.reshape(b, t, GATE_LANES), gct,
                        gdn_norm_g[l][None, :], conv_sc_w[l], n_heads=n_heads, head_dim=head_dim,
                        col_q=col_q, col_k=col_k, col_v=col_v, col_z=col_z,
                        col_b=col_b, col_c=col_c, col_h=col_h, v_width=v_width)
        merged = _merge(o_a.reshape(b * t, v_width), o_b.reshape(b * t, sc_width), wa_bf, wb_bf, proj2,
                        col_ga=col_ga, col_gb=col_gb)
        h_mid, hn = _outproj(merged, wout_bf, h2, ln_ffn_g[l][None, :])
        last = l == depth - 1
        g_last = ln_final_g[None, :] if last else jnp.ones((1, d), F32)
        h2 = _ffn(hn, wgate_bf, wup_bf, wdown_bf, h_mid, g_last, final_norm=last)
    return h2.reshape(b, t, d)
```

```python
import functools
import math

import jax
import jax.numpy as jnp
from jax import lax
from jax.experimental import pallas as pl
from jax.experimental.pallas import tpu as pltpu

EPS = 1e-6
F32 = jnp.float32
BF16 = jnp.bfloat16

LANES = 128
V7X_VMEM_CAP_BYTES = 56 * 1024 * 1024
GDN_CHUNK = 128
GDN_HEADS_PER_STEP = 16
GATE_LANES = LANES
BF16_ROWS = 16
ROW_TILE = 1024
COL_TILE = 1024
PAIR_COL_TILE = 512
OUTPROJ_ROW_TILE = 512
GDN_ROWS_PER_STEP = 2 * GDN_CHUNK


def _vmem_limit(block_bytes, scratch_bytes=0, temp_bytes=0):
    need = 2 * block_bytes + scratch_bytes + temp_bytes + (4 << 20)
    return int(min(V7X_VMEM_CAP_BYTES, max(need, 16 << 20)))


def _pick_tile(n, target, align):
    t = min(n, target)
    t -= t % align
    while t >= align:
        if n % t == 0:
            return t
        t -= align
    return n


def _sigmoid(x):
    return 0.5 * jnp.tanh(0.5 * x) + 0.5


def _silu(x):
    h = 0.5 * x
    return h + h * jnp.tanh(h)


def _dot(a, b):
    return jnp.dot(a, b, preferred_element_type=F32)


def _dot_nt(a, b):
    return lax.dot_general(a, b, (((1,), (1,)), ((), ())), preferred_element_type=F32)


def _dot_tn(a, b):
    return lax.dot_general(a, b, (((0,), (0,)), ((), ())), preferred_element_type=F32)


def _rms(x, g):
    return x * lax.rsqrt(jnp.mean(x * x, axis=-1, keepdims=True) + EPS) * g


def _gprep_kernel(g_ref, hi_ref, lo_ref):
    g = g_ref[...]
    pad = jnp.zeros((GATE_LANES - g.shape[0], g.shape[1]), F32)
    w = jnp.concatenate([g, pad], axis=0).T
    hi = w.astype(BF16)
    hi_ref[...] = hi
    lo_ref[...] = (w - hi.astype(F32)).astype(BF16)


def _gprep(w_t, *, col_gates, gate_cols):
    d = w_t.shape[1]
    out = jax.ShapeDtypeStruct((d, GATE_LANES), BF16)
    return pl.pallas_call(
        _gprep_kernel,
        grid=(1,),
        in_specs=[pl.BlockSpec((pl.Element(gate_cols), pl.Element(d)), lambda i: (col_gates, 0))],
        out_specs=[pl.BlockSpec((d, GATE_LANES), lambda i: (0, 0))] * 2,
        out_shape=[out, out],
        name="gprep",
    )(w_t)


NORM_ROWS = 2 * GDN_CHUNK


def _norm_kernel(x_ref, g_ref, wg_hi_ref, wg_lo_ref, gp_ref, xn_ref, gates_ref, gct_ref, *, n_heads):
    rows = NORM_ROWS
    ch_per_pass = rows // GDN_CHUNK
    gp = gp_ref[...]
    row_in_chunk = lax.broadcasted_iota(jnp.int32, (rows, GATE_LANES), 0) & (GDN_CHUNK - 1)
    lane = lax.broadcasted_iota(jnp.int32, (rows, GATE_LANES), 1)

    def norm_rows(r, carry):
        r0 = pl.multiple_of(r * rows, rows)
        y = _rms(x_ref[pl.ds(r0, rows), :], g_ref[...])
        hi = y.astype(BF16)
        xn_ref[pl.ds(r0, rows), :] = hi
        lo = (y - hi.astype(F32)).astype(BF16)
        w_hi = wg_hi_ref[...]
        hi_terms = _dot(hi, jnp.concatenate([w_hi, wg_lo_ref[...]], axis=1))
        gates = hi_terms[:, :GATE_LANES] + hi_terms[:, GATE_LANES:] + _dot(lo, w_hi)
        xa = gates + gp[1:2, :]
        softplus = jnp.maximum(xa, 0.0) + jnp.log(1.0 + jnp.exp(-jnp.abs(xa)))
        gcum = -jnp.exp(gp[0:1, :]) * softplus
        shift = 1
        while shift < GDN_CHUNK:
            gcum = gcum + jnp.where(row_in_chunk >= shift, pltpu.roll(gcum, shift, 0), 0.0)
            shift *= 2
        gates_ref[pl.ds(r0, rows), :] = jnp.where(lane < n_heads, _sigmoid(gates), gcum)
        for c in range(ch_per_pass):
            gct_ref[r * ch_per_pass + c] = gcum[c * GDN_CHUNK:(c + 1) * GDN_CHUNK, :].T
        return carry

    lax.fori_loop(0, x_ref.shape[0] // rows, norm_rows, 0)


def _norm(x2, g, wg_hi, wg_lo, gate_params, *, n_heads):
    m, d = x2.shape
    tm = _pick_tile(m, ROW_TILE, NORM_ROWS)
    n_ch = tm // GDN_CHUNK
    blocks = tm * d * 6 + 2 * tm * GATE_LANES * 4 + 2 * d * GATE_LANES * 2
    kern = functools.partial(_norm_kernel, n_heads=n_heads)
    return pl.pallas_call(
        kern,
        grid=(m // tm,),
        in_specs=[
            pl.BlockSpec((tm, d), lambda i: (i, 0)),
            pl.BlockSpec((1, d), lambda i: (0, 0)),
            pl.BlockSpec((d, GATE_LANES), lambda i: (0, 0)),
            pl.BlockSpec((d, GATE_LANES), lambda i: (0, 0)),
            pl.BlockSpec((8, GATE_LANES), lambda i: (0, 0)),
        ],
        out_specs=[
            pl.BlockSpec((tm, d), lambda i: (i, 0)),
            pl.BlockSpec((tm, GATE_LANES), lambda i: (i, 0)),
            pl.BlockSpec((n_ch, GATE_LANES, GDN_CHUNK), lambda i: (i, 0, 0)),
        ],
        out_shape=[
            jax.ShapeDtypeStruct((m, d), BF16),
            jax.ShapeDtypeStruct((m, GATE_LANES), F32),
            jax.ShapeDtypeStruct((m // GDN_CHUNK, GATE_LANES, GDN_CHUNK), F32),
        ],
        compiler_params=pltpu.CompilerParams(
            dimension_semantics=("parallel",),
            vmem_limit_bytes=_vmem_limit(blocks, 0, 8 * NORM_ROWS * d * 4)),
        name="norm",
    )(x2, g, wg_hi, wg_lo, gate_params)


def _inproj_kernel(*refs, n_side, n_tiles, n_conv, seq_len):
    xn_ref, wt_ref, cw_ref = refs[:3]
    side_in = refs[3:3 + n_side]
    out_ref = refs[3 + n_side]
    side_out = refs[4 + n_side:4 + 2 * n_side]
    wbf_ref, cbuf_ref, halo_ref = refs[4 + 2 * n_side:]
    jn = pl.program_id(0)
    im = pl.program_id(1)
    tm, tn = out_ref.shape
    lane_tiles = tn // LANES
    prep_cols = wt_ref.shape[0]

    @pl.when(jnp.logical_and(jn == 0, im == 0))
    def _():
        halo_ref[...] = jnp.zeros_like(halo_ref)

    def side_jobs():
        w_t = wt_ref[...].T.astype(BF16)
        slot = (jn % 2) * lane_tiles + im * (prep_cols // LANES)
        for c in range(prep_cols // LANES):
            wbf_ref[slot + c] = w_t[:, c * LANES:(c + 1) * LANES]
        for s_in, s_out in zip(side_in, side_out):
            s_out[...] = s_in[...].astype(s_out.dtype)

    def weights():
        slot = ((jn + 1) % 2) * lane_tiles
        return jnp.concatenate([wbf_ref[slot + lt] for lt in range(lane_tiles)], axis=1)

    @pl.when(jn == 0)
    def _():
        side_jobs()

    @pl.when(jn > n_conv)
    def _():
        out_ref[...] = _dot(xn_ref[...], weights())
        side_jobs()

    @pl.when(jnp.logical_and(jn >= 1, jn <= n_conv))
    def _():
        cw = cw_ref[...]
        taps = cw.shape[0]
        starts_seq = (im * tm) % seq_len == 0
        acc = _dot(xn_ref[...], weights())
        for lt in range(lane_tiles):
            lanes = slice(lt * LANES, (lt + 1) * LANES)
            x = acc[:, lanes]
            cbuf_ref[lt, pl.ds(0, 8), :] = jnp.where(starts_seq, 0.0, halo_ref[lt])
            cbuf_ref[lt, pl.ds(8, tm), :] = x
            y = x * cw[taps - 1:taps, lanes]
            for s in range(1, taps):
                y = y + cbuf_ref[lt, pl.ds(8 - s, tm), :] * cw[taps - 1 - s:taps - s, lanes]
            halo_ref[lt] = x[tm - 8:tm, :]
            out_ref[:, lanes] = _silu(y)
        side_jobs()


def _slab_rows(n_rows, steps):
    for r in range(16, n_rows + 1, 16):
        if n_rows % r == 0 and n_rows // r <= steps:
            return r
    return n_rows


def _inproj(xn, w_t, conv_w, side, *, col_gates, gate_cols, seq_len):
    m, d = xn.shape
    n = w_t.shape[0] - gate_cols
    taps, conv_cols = conv_w.shape
    tm = _pick_tile(seq_len, ROW_TILE, BF16_ROWS)
    tn = _pick_tile(math.gcd(math.gcd(n, conv_cols), col_gates), COL_TILE, LANES)
    n_m = m // tm
    assert m % tm == 0 and n % tn == 0 and taps <= 8 and gate_cols % 8 == 0
    assert tn % n_m == 0 and (tn // n_m) % LANES == 0
    n_tiles = n // tn
    n_conv = conv_cols // tn
    n_plain = col_gates // tn
    steps = (n_tiles + 1) * n_m
    side_specs, side_shapes, side_bytes = [], [], 0
    for w in side:
        r = _slab_rows(w.shape[0], steps)
        last = w.shape[0] // r - 1
        side_specs.append(pl.BlockSpec((r, w.shape[1]),
                                       lambda j, i, last=last: (jnp.minimum(j * n_m + i, last), 0)))
        side_shapes.append(jax.ShapeDtypeStruct(w.shape, BF16))
        side_bytes += r * w.shape[1] * 6

    prep_cols = tn // n_m

    def weight_rows(j, i):
        tile = jnp.minimum(j, n_tiles - 1)
        start = tile * tn + jnp.where(tile >= n_plain, gate_cols, 0) + i * prep_cols
        return pl.multiple_of(start, 8), 0

    blocks = tm * d * 2 + prep_cols * d * 4 + tm * tn * 4 + 8 * tn * 4 + side_bytes
    scratch = 2 * d * tn * 2 + (tn // LANES) * (tm + 16) * LANES * 4
    kern = functools.partial(_inproj_kernel, n_side=len(side), n_tiles=n_tiles, n_conv=n_conv,
                             seq_len=seq_len)
    outs = pl.pallas_call(
        kern,
        grid=(n_tiles + 1, n_m),
        in_specs=[
            pl.BlockSpec((tm, d), lambda j, i: (jnp.where(j == 0, 0, i), 0)),
            pl.BlockSpec((pl.Element(prep_cols), pl.Element(d)), weight_rows),
            pl.BlockSpec((taps, tn), lambda j, i: (0, jnp.clip(j - 1, 0, n_conv - 1))),
        ] + side_specs,
        out_specs=[
            pl.BlockSpec((tm, tn), lambda j, i: (jnp.where(j == 0, 0, i), jnp.maximum(j - 1, 0))),
        ] + side_specs,
        out_shape=[jax.ShapeDtypeStruct((m, n), F32)] + side_shapes,
        scratch_shapes=[
            pltpu.VMEM((2 * (tn // LANES), d, LANES), BF16),
            pltpu.VMEM((tn // LANES, tm + 8, LANES), F32),
            pltpu.VMEM((tn // LANES, 8, LANES), F32),
        ],
        compiler_params=pltpu.CompilerParams(
            dimension_semantics=("arbitrary", "arbitrary"),
            vmem_limit_bytes=_vmem_limit(blocks, scratch, 2 * tm * tn * 4)),
        name="inproj",
    )(xn, w_t, conv_w, *side)
    return outs[0], outs[1:]


def _inv_unit_lower_minus_eye(a_list, rowi, coli):
    n = GDN_CHUNK
    diag16 = (rowi >> 4) == (coli >> 4)
    x = [jnp.where(diag16, a, 0.0) for a in a_list]
    e = [-t for t in x]
    xb = [t.astype(BF16) for t in x]
    x = [_dot(t, t) for t in xb]
    yield
    for step in range(3):
        xb = [t.astype(BF16) for t in x]
        if step < 2:
            both = [_dot(jnp.concatenate([ei.astype(BF16), xi], axis=0), xi) for ei, xi in zip(e, xb)]
            ex = [t[:n] for t in both]
            x_next = [t[n:] for t in both]
        else:
            ex = [_dot(ei.astype(BF16), xi) for ei, xi in zip(e, xb)]
            x_next = x
        yield
        e = [ei + xi + exi for ei, xi, exi in zip(e, x, ex)]
        x = x_next

    half = 16
    while half < n:
        rows = [slice(r, r + half) for r in range(half, n, 2 * half)]
        pick = lambda t: jnp.concatenate([t[r] for r in rows], axis=0)
        p = lax.broadcasted_iota(jnp.int32, (n // 2, n), 0)
        ci = lax.broadcasted_iota(jnp.int32, (n // 2, n), 1)
        ri = (((p // half) * 2 + 1) * half) + (p % half)
        off = jnp.logical_and((ri >> half.bit_length()) == (ci >> half.bit_length()), ci < (ri & -half))
        eb = [t.astype(BF16) for t in e]
        y = [jnp.where(off, pick(a), 0.0) for a in a_list]
        z = [yi + _dot(yi.astype(BF16), ei) for yi, ei in zip(y, eb)]
        yield
        zero = jnp.zeros((half, n), F32)
        z_full = []
        for zi in z:
            pieces = []
            for k in range(len(rows)):
                pieces += [zero, zi[k * half:(k + 1) * half]]
            z_full.append(jnp.concatenate(pieces, axis=0).astype(BF16))
        ez = [_dot(pick(ei).astype(BF16), zf) for ei, zf in zip(e, z_full)]
        yield
        new = [pick(ei) - zi - ezi for ei, zi, ezi in zip(e, z, ez)]
        merged = []
        for ei, ni in zip(e, new):
            pieces = []
            for k, r in enumerate(rows):
                pieces += [ei[r.start - half:r.start], ni[k * half:(k + 1) * half]]
            merged.append(jnp.concatenate(pieces, axis=0))
        e = merged
        half *= 2
    return e


def _gdn_kernel(q_ref, k_ref, v_ref, z_ref, gates_ref, gct_ref, ng_ref, scb_ref, scc_ref, sch_ref,
                cws_ref, o_ref, ob_ref, s_ref, halo_ref, *, n_heads, head_dim):
    c_len = GDN_CHUNK
    hb = GDN_HEADS_PER_STEP
    tb = q_ref.shape[1]
    n_chunks = tb // c_len
    hg = pl.program_id(1)

    @pl.when(pl.program_id(2) == 0)
    def _():
        s_ref[...] = jnp.zeros_like(s_ref)
        halo_ref[:, 0:8, :] = jnp.zeros((hb, 8, head_dim), F32)

    gates = gates_ref[0]
    gate_lane = lax.broadcasted_iota(jnp.int32, (tb, GATE_LANES), 1)

    def causal_conv(load_x, cw_ref):
        cw = cw_ref[...]
        taps = cw.shape[0]
        out = []
        for j in range(hb):
            lanes = slice(j * head_dim, (j + 1) * head_dim)
            x = load_x(lanes)
            halo_ref[j, pl.ds(8, tb), :] = x
            y = x * cw[taps - 1:taps, lanes]
            for s in range(1, taps):
                y = y + halo_ref[j, pl.ds(8 - s, tb), :] * cw[taps - 1 - s:taps - s, lanes]
            halo_ref[j, pl.ds(0, 8), :] = x[tb - 8:tb, :]
            out.append(y)
        return out

    head_lanes = [slice(j * head_dim, (j + 1) * head_dim) for j in range(hb)]
    yq = [q_ref[0, :, ln] for ln in head_lanes]
    yk = [k_ref[0, :, ln] for ln in head_lanes]
    yv = [v_ref[0, :, ln] for ln in head_lanes]

    rowi = lax.broadcasted_iota(jnp.int32, (c_len, c_len), 0)
    coli = lax.broadcasted_iota(jnp.int32, (c_len, c_len), 1)
    incl = rowi >= coli
    strict = rowi > coli

    qh, kh, vh, beta, gcol = [], [], [], [], []
    for j in range(hb):
        head = hg * hb + j
        qj = yq[j]
        kj = yk[j]
        qh.append(qj * (lax.rsqrt(jnp.sum(qj * qj, axis=1, keepdims=True) + EPS) * head_dim ** -0.5))
        kh.append(kj * lax.rsqrt(jnp.sum(kj * kj, axis=1, keepdims=True) + EPS))
        vh.append(yv[j])
        beta.append(jnp.sum(jnp.where(gate_lane == head, gates, 0.0), axis=1, keepdims=True))
        gcol.append(jnp.sum(jnp.where(gate_lane == head + n_heads, gates, 0.0), axis=1, keepdims=True))

    rows = {c: slice(c * c_len, (c + 1) * c_len) for c in range(n_chunks)}
    prepared = {}

    def prepare(c):
        g_c = [gcol[j][rows[c]] for j in range(hb)]
        g_r = [gct_ref[c, pl.ds(hg * hb + j + n_heads, 1), :] for j in range(hb)]
        g_last = [g[c_len - 1:c_len, :] for g in g_c]
        decay = [jnp.where(incl, jnp.exp(jnp.where(incl, gc - gr, 0.0)), 0.0) for gc, gr in zip(g_c, g_r)]
        e_g = [jnp.exp(g) for g in g_c]
        kc = [kh[j][rows[c]] for j in range(hb)]
        qc = [qh[j][rows[c]] for j in range(hb)]
        bb = [beta[j][rows[c]] for j in range(hb)]
        kb = [k * b_ for k, b_ in zip(kc, bb)]
        kbf = [k.astype(BF16) for k in kc]
        kk = [_dot_nt(a.astype(BF16), b_) for a, b_ in zip(kb, kbf)]
        qk = [_dot_nt(a.astype(BF16), b_) for a, b_ in zip(qc, kbf)]
        yield
        a_low = [jnp.where(strict, t * d_, 0.0) for t, d_ in zip(kk, decay)]
        attn = [(t * d_).astype(BF16) for t, d_ in zip(qk, decay)]
        e_inv = yield from _inv_unit_lower_minus_eye(a_low, rowi, coli)
        rhs = [jnp.concatenate([vh[j][rows[c]] * b_, kb_ * eg], axis=1)
               for j, (b_, kb_, eg) in enumerate(zip(bb, kb, e_g))]
        sol = [r + _dot(e.astype(BF16), r.astype(BF16)) for r, e in zip(rhs, e_inv)]
        yield
        u = [t[:, :head_dim] for t in sol]
        wq = [jnp.concatenate([t[:, head_dim:], q * eg], axis=0).astype(BF16)
              for t, q, eg in zip(sol, qc, e_g)]
        k_dec = [(k * jnp.exp(gl - g)).astype(BF16) for k, gl, g in zip(kc, g_last, g_c)]
        e_last = [jnp.exp(gl) for gl in g_last]
        prepared[c] = (u, wq, attn, k_dec, e_last)

    z = z_ref[0]
    ng = ng_ref[...]
    state = {"s": [s_ref[j] for j in range(hb)]}

    def recur(c):
        u, wq, attn, k_dec, e_last = prepared[c]
        s = state["s"]
        ws = [_dot(wq[j], s[j].astype(BF16)) for j in range(hb)]
        yield
        v_bf = [(u[j] - ws[j][:c_len]).astype(BF16) for j in range(hb)]
        o = [ws[j][c_len:] + _dot(attn[j], v_bf[j]) for j in range(hb)]
        state["s"] = [s[j] * e_last[j] + _dot_tn(k_dec[j], v_bf[j]) for j in range(hb)]
        yield
        for j in range(hb):
            lanes = slice(j * head_dim, (j + 1) * head_dim)
            zc = z[rows[c], lanes]
            o_ref[0, rows[c], lanes] = (_rms(o[j], ng) * _silu(zc)).astype(o_ref.dtype)

    def run_together(*gens):
        live = list(gens)
        while live:
            for g in list(live):
                try:
                    next(g)
                except StopIteration:
                    live.remove(g)

    def mixer_b():
        ysc = causal_conv(lambda ln: scc_ref[0, :, ln] * sch_ref[0, :, ln], cws_ref)
        yield
        for j in range(hb):
            lanes = slice(j * head_dim, (j + 1) * head_dim)
            ob_ref[0, :, lanes] = (scb_ref[0, :, lanes] * ysc[j]).astype(ob_ref.dtype)

    run_together(prepare(0))
    for c in range(n_chunks):
        if c + 1 < n_chunks:
            run_together(prepare(c + 1), recur(c))
        else:
            run_together(recur(c), mixer_b())
    for j in range(hb):
        s_ref[j] = state["s"][j]


def _gdn(proj3, gates3, gct, norm_g, conv_sc_w, *, n_heads, head_dim, col_q, col_k, col_v, col_z,
         col_b, col_c, col_h, v_width):
    b, t, _ = proj3.shape
    hb = GDN_HEADS_PER_STEP
    gw = hb * head_dim
    tb = _pick_tile(t, GDN_ROWS_PER_STEP, GDN_CHUNK)
    n_ch = tb // GDN_CHUNK
    steps = t // tb
    taps_sc = conv_sc_w.shape[0]
    qb, kb_, vb, zb = col_q // gw, col_k // gw, col_v // gw, col_z // gw
    sb, sc, sh = col_b // gw, col_c // gw, col_h // gw
    blocks = 7 * tb * gw * 4 + 2 * tb * GATE_LANES * 4 + 8 * gw * 4 + 2 * tb * gw * 2
    scratch = hb * head_dim * head_dim * 4 + (tb + 8) * gw * 4
    kern = functools.partial(_gdn_kernel, n_heads=n_heads, head_dim=head_dim)
    out = jax.ShapeDtypeStruct((b, t, v_width), BF16)
    return pl.pallas_call(
        kern,
        grid=(b, n_heads // hb, steps),
        in_specs=[
            pl.BlockSpec((1, tb, gw), lambda i, h, s: (i, s, qb + h)),
            pl.BlockSpec((1, tb, gw), lambda i, h, s: (i, s, kb_ + h)),
            pl.BlockSpec((1, tb, gw), lambda i, h, s: (i, s, vb + h)),
            pl.BlockSpec((1, tb, gw), lambda i, h, s: (i, s, zb + h)),
            pl.BlockSpec((1, tb, GATE_LANES), lambda i, h, s: (i, s, 0)),
            pl.BlockSpec((n_ch, GATE_LANES, GDN_CHUNK), lambda i, h, s: (i * steps + s, 0, 0)),
            pl.BlockSpec((1, head_dim), lambda i, h, s: (0, 0)),
            pl.BlockSpec((1, tb, gw), lambda i, h, s: (i, s, sb + h)),
            pl.BlockSpec((1, tb, gw), lambda i, h, s: (i, s, sc + h)),
            pl.BlockSpec((1, tb, gw), lambda i, h, s: (i, s, sh + h)),
            pl.BlockSpec((taps_sc, gw), lambda i, h, s: (0, h)),
        ],
        out_specs=[pl.BlockSpec((1, tb, gw), lambda i, h, s: (i, s, h))] * 2,
        out_shape=[out, out],
        scratch_shapes=[
            pltpu.VMEM((hb, head_dim, head_dim), F32),
            pltpu.VMEM((hb, tb + 8, head_dim), F32),
        ],
        compiler_params=pltpu.CompilerParams(
            dimension_semantics=("parallel", "parallel", "arbitrary"),
            vmem_limit_bytes=_vmem_limit(blocks, scratch, 16 << 20)),
        name="gdn",
    )(proj3, proj3, proj3, proj3, gates3, gct, norm_g, proj3, proj3, proj3, conv_sc_w)


def _merge_kernel(a_ref, b_ref, wa_ref, wb_ref, ga_ref, gb_ref, o_ref):
    pa = _dot(a_ref[...], wa_ref[...])
    pb = _dot(b_ref[...], wb_ref[...])
    o_ref[...] = (_sigmoid(ga_ref[...]) * pa + _sigmoid(gb_ref[...]) * pb).astype(o_ref.dtype)


def _merge(o_a, o_b, wa, wb, proj2, *, col_ga, col_gb):
    m, ka = o_a.shape
    kb_ = o_b.shape[1]
    d = wa.shape[1]
    tm = _pick_tile(m, ROW_TILE, BF16_ROWS)
    tn = _pick_tile(d, PAIR_COL_TILE, LANES)
    ga0, gb0 = col_ga // tn, col_gb // tn
    blocks = tm * (ka + kb_) * 2 + (ka + kb_) * tn * 2 + 2 * tm * tn * 4 + tm * tn * 2
    return pl.pallas_call(
        _merge_kernel,
        grid=(m // tm, d // tn),
        in_specs=[
            pl.BlockSpec((tm, ka), lambda i, j: (i, 0)),
            pl.BlockSpec((tm, kb_), lambda i, j: (i, 0)),
            pl.BlockSpec((ka, tn), lambda i, j: (0, j)),
            pl.BlockSpec((kb_, tn), lambda i, j: (0, j)),
            pl.BlockSpec((tm, tn), lambda i, j: (i, ga0 + j)),
            pl.BlockSpec((tm, tn), lambda i, j: (i, gb0 + j)),
        ],
        out_specs=pl.BlockSpec((tm, tn), lambda i, j: (i, j)),
        out_shape=jax.ShapeDtypeStruct((m, d), BF16),
        compiler_params=pltpu.CompilerParams(
            dimension_semantics=("parallel", "parallel"),
            vmem_limit_bytes=_vmem_limit(blocks, 0, 4 * tm * tn * 4)),
        name="merge",
    )(o_a, o_b, wa, wb, proj2, proj2)


def _outproj_kernel(m_ref, w_ref, x_ref, g_ref, h_ref, hn_ref):
    h = x_ref[...] + _dot(m_ref[...], w_ref[...])
    h_ref[...] = h
    hn_ref[...] = _rms(h, g_ref[...]).astype(hn_ref.dtype)


def _outproj(merged, w_out, x2, g):
    m, d = x2.shape
    k = merged.shape[1]
    tm = _pick_tile(m, OUTPROJ_ROW_TILE, BF16_ROWS)
    blocks = tm * k * 2 + k * d * 2 + 2 * tm * d * 4 + tm * d * 2 + d * 4
    return pl.pallas_call(
        _outproj_kernel,
        grid=(m // tm,),
        in_specs=[
            pl.BlockSpec((tm, k), lambda i: (i, 0)),
            pl.BlockSpec((k, d), lambda i: (0, 0)),
            pl.BlockSpec((tm, d), lambda i: (i, 0)),
            pl.BlockSpec((1, d), lambda i: (0, 0)),
        ],
        out_specs=[
            pl.BlockSpec((tm, d), lambda i: (i, 0)),
            pl.BlockSpec((tm, d), lambda i: (i, 0)),
        ],
        out_shape=[
            jax.ShapeDtypeStruct((m, d), F32),
            jax.ShapeDtypeStruct((m, d), BF16),
        ],
        compiler_params=pltpu.CompilerParams(
            dimension_semantics=("parallel",),
            vmem_limit_bytes=_vmem_limit(blocks, 0, 3 * tm * d * 4)),
        name="outproj",
    )(merged, w_out, x2, g)


def _ffn_kernel(hn_ref, wg_ref, wu_ref, wd_ref, h_hbm, g_ref, o_ref, ff_ref, sem, *, final_norm):
    i = pl.program_id(0)
    f = pl.program_id(1)
    n_hidden = pl.num_programs(1) - 1
    tm = o_ref.shape[0]

    def residual_copy():
        return pltpu.make_async_copy(h_hbm.at[pl.ds(pl.multiple_of(i * tm, tm), tm), :], o_ref, sem)

    def hidden():
        hn = hn_ref[...]
        return (_silu(_dot(hn, wg_ref[...])) * _dot(hn, wu_ref[...])).astype(BF16)

    @pl.when(f == 0)
    def _():
        residual_copy().start()
        ff_ref[...] = hidden()

    @pl.when(f == 1)
    def _():
        residual_copy().wait()

    @pl.when(jnp.logical_and(f >= 1, f < n_hidden))
    def _():
        o_ref[...] += _dot(ff_ref[...], wd_ref[...])
        ff_ref[...] = hidden()

    @pl.when(f == n_hidden)
    def _():
        o_ref[...] += _dot(ff_ref[...], wd_ref[...])
        if final_norm:
            o_ref[...] = _rms(o_ref[...], g_ref[...])


def _ffn(hn, wg, wu, wd, h, g, *, final_norm):
    m, d = h.shape
    dff = wg.shape[1]
    tm = _pick_tile(m, ROW_TILE, BF16_ROWS)
    tf = _pick_tile(dff, PAIR_COL_TILE, LANES)
    nf = dff // tf
    blocks = tm * d * 2 + 3 * d * tf * 2 + tm * d * 4 + d * 4
    kern = functools.partial(_ffn_kernel, final_norm=final_norm)
    return pl.pallas_call(
        kern,
        grid=(m // tm, nf + 1),
        in_specs=[
            pl.BlockSpec((tm, d), lambda i, f: (i, 0)),
            pl.BlockSpec((d, tf), lambda i, f: (0, jnp.minimum(f, nf - 1))),
            pl.BlockSpec((d, tf), lambda i, f: (0, jnp.minimum(f, nf - 1))),
            pl.BlockSpec((tf, d), lambda i, f: (jnp.maximum(f - 1, 0), 0)),
            pl.BlockSpec(memory_space=pl.ANY),
            pl.BlockSpec((1, d), lambda i, f: (0, 0)),
        ],
        out_specs=pl.BlockSpec((tm, d), lambda i, f: (i, 0)),
        out_shape=jax.ShapeDtypeStruct((m, d), F32),
        scratch_shapes=[pltpu.VMEM((tm, tf), BF16), pltpu.SemaphoreType.DMA(())],
        compiler_params=pltpu.CompilerParams(
            dimension_semantics=("parallel", "arbitrary"),
            vmem_limit_bytes=_vmem_limit(blocks, tm * tf * 2, 4 * tm * tf * 4 + tm * d * 4)),
        name="ffn",
    )(hn, wg, wu, wd, h, g)


def kernel(x, ln_mix_g, w_in, conv_qkv_w, A_log, dt_bias, gdn_norm_g, w_proj_a, conv_sc_w, w_proj_b,
           w_out, ln_ffn_g, w_gate, w_up, w_down, ln_final_g):
    b, t, d = x.shape
    depth = w_in.shape[0]
    n_heads = A_log.shape[1]
    v_width = w_proj_a.shape[1]
    qk_width = (conv_qkv_w.shape[2] - v_width) // 2
    sc_width = w_proj_b.shape[1]
    head_dim = gdn_norm_g.shape[1]
    assert qk_width == n_heads * head_dim and v_width == n_heads * head_dim
    assert head_dim == LANES and n_heads % GDN_HEADS_PER_STEP == 0 and 2 * n_heads <= GATE_LANES
    assert t % GDN_CHUNK == 0 and sc_width == v_width

    col_q, col_k, col_v = 0, qk_width, 2 * qk_width
    col_z = col_v + v_width
    col_gates = col_z + v_width
    rest = col_gates + 2 * n_heads
    col_b = col_gates
    col_c = col_b + sc_width
    col_h = col_c + sc_width
    col_ga = col_h + sc_width
    col_gb = col_ga + d

    h2 = x.reshape(b * t, d)
    for l in range(depth):
        w_t = jnp.swapaxes(w_in[l], 0, 1)
        w_g_hi, w_g_lo = _gprep(w_t, col_gates=col_gates, gate_cols=rest - col_gates)
        gate_params = jnp.pad(jnp.stack([A_log[l], dt_bias[l]]),
                              ((0, 6), (n_heads, GATE_LANES - 2 * n_heads)))

        xn, gates2, gct = _norm(h2, ln_mix_g[l][None, :], w_g_hi, w_g_lo, gate_params, n_heads=n_heads)
        later = [w_proj_a[l], w_proj_b[l], w_out[l], w_gate[l], w_up[l], w_down[l]]
        proj2, (wa_bf, wb_bf, wout_bf, wgate_bf, wup_bf, wdown_bf) = _inproj(
            xn, w_t, conv_qkv_w[l], later, col_gates=col_gates, gate_cols=rest - col_gates, seq_len=t)
        proj3 = proj2.reshape(b, t, -1)
        o_a, o_b = _gdn(proj3, gates2.reshape(b, t, GATE_LANES), gct,
                        gdn_norm_g[l][None, :], conv_sc_w[l], n_heads=n_heads, head_dim=head_dim,
                        col_q=col_q, col_k=col_k, col_v=col_v, col_z=col_z,
                        col_b=col_b, col_c=col_c, col_h=col_h, v_width=v_width)
        merged = _merge(o_a.reshape(b * t, v_width), o_b.reshape(b * t, sc_width), wa_bf, wb_bf, proj2,
                        col_ga=col_ga, col_gb=col_gb)
        h_mid, hn = _outproj(merged, wout_bf, h2, ln_ffn_g[l][None, :])
        last = l == depth - 1
        g_last = ln_final_g[None, :] if last else jnp.ones((1, d), F32)
        h2 = _ffn(hn, wgate_bf, wup_bf, wdown_bf, h_mid, g_last, final_norm=last)
    return h2.reshape(b, t, d)
```
